```python
import math
import jax, jax.numpy as jnp
from jax import lax
import numpy as np

D_MODEL = 4096
BATCH = 2
SEQ = 4096
DEPTH = 2

CONV_CH = D_MODEL // 2
CONV_WIDTH = 31
HEAD_DIM = 128
HEADS_PER_GROUP = 8
ATTN_GROUPS = ((128, 1), (512, 4), (2048, 16))
N_GROUPS = len(ATTN_GROUPS)
N_ATT_HEADS = N_GROUPS * HEADS_PER_GROUP
ATT_W = N_ATT_HEADS * HEAD_DIM
ATT_OUT_W = HEADS_PER_GROUP * HEAD_DIM
BLOCK = 128
NUM_BUCKETS = 32
MAX_DISTANCE = 2048
IN_W = 2 * CONV_CH + 3 * ATT_W + 2 * D_MODEL
D_FF = 11008
N_EXPERTS = 8
TOP_K = 2
D_FF_EXPERT = 3584
N_DENSE = (DEPTH + 1) // 2
N_MOE = DEPTH // 2

kernel_name = 'hybrid_conv_dilated_attn_moe_block'


def rms_norm(x, g, eps=1e-6):
    xf = x.astype(jnp.float32)
    y = xf * lax.rsqrt(jnp.mean(xf * xf, axis=-1, keepdims=True) + eps)
    return (y * g.astype(jnp.float32)).astype(x.dtype)


def layer_norm(x, g, b, eps=1e-5):
    xf = x.astype(jnp.float32)
    mu = jnp.mean(xf, axis=-1, keepdims=True)
    var = jnp.mean(jnp.square(xf - mu), axis=-1, keepdims=True)
    y = (xf - mu) * lax.rsqrt(var + eps)
    return (y * g.astype(jnp.float32) + b.astype(jnp.float32)).astype(x.dtype)


def modulate(h, shift, scale):
    return h * (1 + scale[:, None, :]) + shift[:, None, :]


def t5_bucket(dist):
    max_exact = NUM_BUCKETS // 2
    n = jnp.maximum(dist, 1).astype(jnp.float32)
    large = max_exact + (jnp.log(n / max_exact) / math.log(MAX_DISTANCE / max_exact)
                         * (NUM_BUCKETS - max_exact)).astype(jnp.int32)
    large = jnp.minimum(large, NUM_BUCKETS - 1)
    return jnp.where(dist < max_exact, dist, large)


def conv_module(glu_in, conv_w, conv_b, ln_g, ln_b, w_conv_out):
    a, b = jnp.split(glu_in, 2, axis=-1)
    u = a * jax.nn.sigmoid(b)
    u = lax.conv_general_dilated(u, conv_w[:, None, :].astype(u.dtype), window_strides=(1,),
                                 padding=[(CONV_WIDTH - 1, 0)],
                                 dimension_numbers=('NWC', 'WIO', 'NWC'),
                                 feature_group_count=CONV_CH) + conv_b
    u = jax.nn.silu(layer_norm(u, ln_g, ln_b))
    return u @ w_conv_out


def dilated_attention(q, k, v, rel_table, window, dilation):
    B, S, H, hd = q.shape
    L = S // dilation
    span = window // dilation
    blk = min(BLOCK, L)
    nb = -(-L // blk)
    Lp = nb * blk

    def to_sub(t):
        t = t.reshape(B, L, dilation, H, hd).transpose(0, 2, 1, 3, 4)
        t = jnp.pad(t, ((0, 0), (0, 0), (0, Lp - L), (0, 0), (0, 0)))
        return t.reshape(B, dilation, nb, blk, H, hd)

    def with_prev(t):
        prev = jnp.pad(t, ((0, 0), (0, 0), (1, 0), (0, 0), (0, 0), (0, 0)))[:, :, :-1]
        return jnp.concatenate([prev, t], axis=3)

    qs = to_sub(q)
    kw = with_prev(to_sub(k))
    vw = with_prev(to_sub(v))
    scores = jnp.einsum('brnqhd,brnkhd->brnhqk', qs, kw,
                        preferred_element_type=jnp.float32) * (HEAD_DIM ** -0.5)
    qi = jnp.arange(blk)[:, None] + blk
    kj = jnp.arange(2 * blk)[None, :]
    delta = qi - kj
    band = (delta >= 0) & (delta <= span)
    valid = band[None] & ((jnp.arange(nb)[:, None, None] > 0) | (kj >= blk)[None])
    bias = rel_table[t5_bucket(jnp.maximum(delta, 0) * dilation)]
    scores = scores + bias.astype(jnp.float32).transpose(2, 0, 1)
    scores = jnp.where(valid[:, None], scores, -jnp.inf)
    lse = jax.nn.logsumexp(scores, axis=-1)
    p = jnp.exp(scores - lse[..., None])
    out = jnp.einsum('brnhqk,brnkhd->brnqhd', p, vw.astype(jnp.float32))
    out = out.reshape(B, dilation, Lp, H, hd)[:, :, :L].transpose(0, 2, 1, 3, 4).reshape(B, S, H, hd)
    lse = lse.transpose(0, 1, 2, 4, 3).reshape(B, dilation, Lp, H)[:, :, :L]
    lse = lse.transpose(0, 2, 1, 3).reshape(B, S, H)
    return out, lse


def hybrid_mixer(h, w_in, conv_w, conv_b, ln_g, ln_b, w_conv_out, q_gain, k_gain, rel_bias,
                 w_attn_out, w_out):
    B, S, _ = h.shape
    p = h @ w_in
    o1 = 2 * CONV_CH
    cuts = [o1, o1 + ATT_W, o1 + 2 * ATT_W, o1 + 3 * ATT_W, o1 + 3 * ATT_W + D_MODEL]
    glu_in, q, k, v, gc, ga = jnp.split(p, cuts, axis=-1)
    conv_out = conv_module(glu_in, conv_w, conv_b, ln_g, ln_b, w_conv_out)
    shp = (B, S, N_GROUPS, HEADS_PER_GROUP, HEAD_DIM)
    q = rms_norm(q.reshape(shp), q_gain)
    k = rms_norm(k.reshape(shp), k_gain)
    v = v.reshape(shp)
    outs, lses = [], []
    for g, (window, dil) in enumerate(ATTN_GROUPS):
        o, lse = dilated_attention(q[:, :, g], k[:, :, g], v[:, :, g],
                                   rel_bias[:, g * HEADS_PER_GROUP:(g + 1) * HEADS_PER_GROUP],
                                   window, dil)
        outs.append(o)
        lses.append(lse)
    wgt = jax.nn.softmax(jnp.stack(lses), axis=0)
    o = jnp.sum(wgt[..., None] * jnp.stack(outs), axis=0).reshape(B, S, ATT_OUT_W).astype(h.dtype)
    attn_out = o @ w_attn_out
    merged = jax.nn.sigmoid(gc) * conv_out + jax.nn.sigmoid(ga) * attn_out
    return merged @ w_out


def swiglu(h, w1, w3, w2):
    return (jax.nn.silu(h @ w1) * (h @ w3)) @ w2


def moe_ffn(h, w_router, w1, w3, w2):
    logits = jnp.einsum('bsd,de->bse', h, w_router, preferred_element_type=jnp.float32)
    top_v, top_i = lax.top_k(logits, TOP_K)
    top_w = jax.nn.softmax(top_v, axis=-1)
    comb = jnp.sum(jax.nn.one_hot(top_i, N_EXPERTS, dtype=jnp.float32) * top_w[..., None], axis=-2)
    y = jnp.zeros(h.shape, jnp.float32)
    for e in range(N_EXPERTS):
        y = y + comb[..., e:e + 1] * swiglu(h, w1[e], w3[e], w2[e])
    return y.astype(h.dtype)


def setup_inputs(seed: int = 0) -> dict:
    key = jax.random.key(seed)
    ks = jax.random.split(key, 26)
    nrm = jax.random.normal
    D = D_MODEL
    f32 = jnp.float32
    return {
        'x': nrm(ks[0], (BATCH, SEQ, D), f32),
        'c': nrm(ks[1], (BATCH, D), f32),
        'w_ada': nrm(ks[2], (DEPTH, D, 6 * D), f32) * (0.5 * D ** -0.5),
        'b_ada': nrm(ks[3], (DEPTH, 6 * D), f32) * 0.02,
        'g_mix': 1.0 + 0.02 * nrm(ks[4], (DEPTH, D), f32),
        'g_ffn': 1.0 + 0.02 * nrm(ks[5], (DEPTH, D), f32),
        'w_in': nrm(ks[6], (DEPTH, D, IN_W), f32) * D ** -0.5,
        'conv_w': nrm(ks[7], (DEPTH, CONV_WIDTH, CONV_CH), f32) * CONV_WIDTH ** -0.5,
        'conv_b': nrm(ks[8], (DEPTH, CONV_CH), f32) * 0.02,
        'conv_ln_g': 1.0 + 0.02 * nrm(ks[9], (DEPTH, CONV_CH), f32),
        'conv_ln_b': nrm(ks[10], (DEPTH, CONV_CH), f32) * 0.02,
        'w_conv_out': nrm(ks[11], (DEPTH, CONV_CH, D), f32) * CONV_CH ** -0.5,
        'q_gain': 1.0 + 0.02 * nrm(ks[12], (DEPTH, HEAD_DIM), f32),
        'k_gain': 1.0 + 0.02 * nrm(ks[13], (DEPTH, HEAD_DIM), f32),
        'rel_bias': nrm(ks[14], (NUM_BUCKETS, N_ATT_HEADS), f32) * 0.2,
        'w_attn_out': nrm(ks[15], (DEPTH, ATT_OUT_W, D), f32) * ATT_OUT_W ** -0.5,
        'w_out': nrm(ks[16], (DEPTH, D, D), f32) * D ** -0.5,
        'ffn_w1': nrm(ks[17], (N_DENSE, D, D_FF), f32) * D ** -0.5,
        'ffn_w3': nrm(ks[18], (N_DENSE, D, D_FF), f32) * D ** -0.5,
        'ffn_w2': nrm(ks[19], (N_DENSE, D_FF, D), f32) * D_FF ** -0.5,
        'moe_router': nrm(ks[20], (N_MOE, D, N_EXPERTS), f32) * D ** -0.5,
        'moe_w1': nrm(ks[21], (N_MOE, N_EXPERTS, D, D_FF_EXPERT), f32) * D ** -0.5,
        'moe_w3': nrm(ks[22], (N_MOE, N_EXPERTS, D, D_FF_EXPERT), f32) * D ** -0.5,
        'moe_w2': nrm(ks[23], (N_MOE, N_EXPERTS, D_FF_EXPERT, D), f32) * D_FF_EXPERT ** -0.5,
    }


def reference(x, c, w_ada, b_ada, g_mix, g_ffn, w_in, conv_w, conv_b, conv_ln_g, conv_ln_b,
              w_conv_out, q_gain, k_gain, rel_bias, w_attn_out, w_out, ffn_w1, ffn_w3, ffn_w2,
              moe_router, moe_w1, moe_w3, moe_w2):
    c_act = jax.nn.silu(c)
    for l in range(DEPTH):
        mod = c_act @ w_ada[l] + b_ada[l]
        sh1, sc1, gt1, sh2, sc2, gt2 = jnp.split(mod, 6, axis=-1)
        h = modulate(rms_norm(x, g_mix[l]), sh1, sc1)
        y = hybrid_mixer(h, w_in[l], conv_w[l], conv_b[l], conv_ln_g[l], conv_ln_b[l],
                         w_conv_out[l], q_gain[l], k_gain[l], rel_bias, w_attn_out[l], w_out[l])
        x = x + gt1[:, None, :] * y
        h = modulate(rms_norm(x, g_ffn[l]), sh2, sc2)
        if l % 2 == 0:
            j = l // 2
            y = swiglu(h, ffn_w1[j], ffn_w3[j], ffn_w2[j])
        else:
            j = l // 2
            y = moe_ffn(h, moe_router[j], moe_w1[j], moe_w3[j], moe_w2[j])
        x = x + gt2[:, None, :] * y
    return x
```

```python
import functools
import math

import numpy as np
import jax
import jax.numpy as jnp
from jax import lax
from jax.experimental import pallas as pl
from jax.experimental.pallas import tpu as pltpu

F32 = jnp.float32
BF16 = jnp.bfloat16
I32 = jnp.int32

ATTN_GROUPS = ((128, 1), (512, 4), (2048, 16))
ATTN_BLOCK = 128
MAX_DISTANCE = 2048
TOP_K = 2

V7X_VMEM_BYTES = 64 * 1024 * 1024
V7X_LANES = 128
VMEM_REQUEST_CAP = V7X_VMEM_BYTES - 6 * 1024 * 1024
VMEM_SLACK = 8 * 1024 * 1024
NEG_INF = float("-inf")


def _fit(n, pref, unit=V7X_LANES):
    best = None
    for d in range(unit, min(n, pref) + 1, unit):
        if n % d == 0:
            best = d
    return best if best is not None else n


def _nbytes(shape, dtype):
    return int(np.prod(shape)) * jnp.dtype(dtype).itemsize


def _cparams(semantics, block_bytes, scratch_bytes=0):
    limit = min(VMEM_REQUEST_CAP, 2 * block_bytes + scratch_bytes + VMEM_SLACK)
    return pltpu.CompilerParams(dimension_semantics=semantics, vmem_limit_bytes=int(limit))


def _ada_kernel(c_ref, w_ref, b_ref, o_ref):
    c = c_ref[...]
    ca = c * jax.nn.sigmoid(c)
    o_ref[...] = jnp.dot(ca, w_ref[...], preferred_element_type=F32,
                         precision=lax.Precision.HIGHEST) + b_ref[...]


def _adaln(c, w_ada, b_ada):
    depth, d, n = w_ada.shape
    b = c.shape[0]
    rows = 8
    c8 = jnp.pad(c, ((0, rows - b), (0, 0)))
    bn = _fit(n, 512)
    blk = _nbytes((rows, d), F32) + _nbytes((d, bn), F32) + 2 * _nbytes((rows, bn), F32)
    out = pl.pallas_call(
        _ada_kernel,
        grid=(depth, n // bn),
        in_specs=[pl.BlockSpec((rows, d), lambda l, j: (0, 0)),
                  pl.BlockSpec((None, d, bn), lambda l, j: (l, 0, j)),
                  pl.BlockSpec((None, 1, bn), lambda l, j: (l, 0, j))],
        out_specs=pl.BlockSpec((None, rows, bn), lambda l, j: (l, 0, j)),
        out_shape=jax.ShapeDtypeStruct((depth, rows, n), F32),
        compiler_params=_cparams(("arbitrary", "arbitrary"), blk, 3 * _nbytes((d, bn), F32)),
        name="adaln",
    )(c8, w_ada, b_ada.reshape(depth, 1, n))
    return out[:, :b]


def _normed(x_ref, g_ref, sh_ref, sc_ref):
    x = x_ref[...]
    ms = jnp.mean(x * x, axis=-1, keepdims=True)
    y = x * lax.rsqrt(ms + 1e-6) * g_ref[...]
    return y * (1.0 + sc_ref[...]) + sh_ref[...]


def _rmsmod_kernel(x_ref, g_ref, sh_ref, sc_ref, o_ref):
    o_ref[...] = _normed(x_ref, g_ref, sh_ref, sc_ref).astype(o_ref.dtype)


def _rmsmod_route_kernel(x_ref, g_ref, sh_ref, sc_ref, wr_ref, h_ref, tw_ref, ti_ref, *, n_exp):
    h = _normed(x_ref, g_ref, sh_ref, sc_ref)
    h_ref[...] = h
    logits = jnp.dot(h, wr_ref[...], preferred_element_type=F32, precision=lax.Precision.HIGHEST)
    lane = lax.broadcasted_iota(I32, logits.shape, 1)
    lanef = lane.astype(F32)
    sentinel = float(V7X_LANES)
    lg = jnp.where(lane < n_exp, logits, NEG_INF)
    m1 = jnp.max(lg, axis=-1, keepdims=True)
    i1 = jnp.min(jnp.where(lg == m1, lanef, sentinel), axis=-1, keepdims=True)
    lg2 = jnp.where(lanef == i1, NEG_INF, lg)
    m2 = jnp.max(lg2, axis=-1, keepdims=True)
    i2 = jnp.min(jnp.where(lg2 == m2, lanef, sentinel), axis=-1, keepdims=True)
    e2 = jnp.exp(m2 - m1)
    den = 1.0 + e2
    tw_ref[...] = jnp.where(lane == 0, 1.0 / den, jnp.where(lane == 1, e2 / den, 0.0))
    ti_ref[...] = jnp.where(lane == 0, i1, jnp.where(lane == 1, i2, 0.0)).astype(I32)


def _rmsmod(x, g, shift, scale, out_dtype, w_router=None):
    b, s, d = x.shape
    ts = _fit(s, 256, 8)
    specs = [pl.BlockSpec((None, ts, d), lambda i, j: (i, j, 0)),
             pl.BlockSpec((1, d), lambda i, j: (0, 0)),
             pl.BlockSpec((None, 1, d), lambda i, j: (i, 0, 0)),
             pl.BlockSpec((None, 1, d), lambda i, j: (i, 0, 0))]
    args = [x, g.reshape(1, d), shift.reshape(b, 1, d), scale.reshape(b, 1, d)]
    tile = pl.BlockSpec((None, ts, d), lambda i, j: (i, j, 0))
    blk = _nbytes((ts, d), F32) * 2 + 3 * _nbytes((1, d), F32)
    if w_router is None:
        return pl.pallas_call(
            _rmsmod_kernel, grid=(b, s // ts), in_specs=specs, out_specs=tile,
            out_shape=jax.ShapeDtypeStruct((b, s, d), out_dtype),
            compiler_params=_cparams(("arbitrary", "arbitrary"), blk, 4 * _nbytes((ts, d), F32)),
            name="rmsmod")(*args)
    n_exp = w_router.shape[1]
    wr = jnp.pad(w_router, ((0, 0), (0, V7X_LANES - n_exp)))
    lanes = pl.BlockSpec((None, ts, V7X_LANES), lambda i, j: (i, j, 0))
    blk += _nbytes((d, V7X_LANES), F32) + 2 * _nbytes((ts, V7X_LANES), F32)
    return pl.pallas_call(
        functools.partial(_rmsmod_route_kernel, n_exp=n_exp),
        grid=(b, s // ts),
        in_specs=specs + [pl.BlockSpec((d, V7X_LANES), lambda i, j: (0, 0))],
        out_specs=[tile, lanes, lanes],
        out_shape=[jax.ShapeDtypeStruct((b, s, d), F32),
                   jax.ShapeDtypeStruct((b, s, V7X_LANES), F32),
                   jax.ShapeDtypeStruct((b, s, V7X_LANES), I32)],
        compiler_params=_cparams(("arbitrary", "arbitrary"), blk, 8 * _nbytes((ts, d), F32)),
        name="rmsmod_route")(*args, wr)


def _ws_kernel(te_ref, tf_ref, tv_ref, tr_ref, *refs, nx, nw, ne, compute):
    del te_ref, tr_ref
    x_refs = refs[:nx]
    w_refs = refs[nx:nx + nw]
    e_refs = refs[nx + nw:nx + nw + ne]
    o_ref = refs[nx + nw + ne]
    wb_refs = refs[nx + nw + ne + 1:]
    t = pl.program_id(1)

    @pl.when(tf_ref[t] == 1)
    def _cast_weights():
        for w_ref, wb_ref in zip(w_refs, wb_refs):
            wb_ref[...] = w_ref[...].astype(BF16)

    @pl.when(tv_ref[t] == 1)
    def _compute():
        o_ref[...] = compute(x_refs, wb_refs, e_refs).astype(o_ref.dtype)

    @pl.when(tv_ref[t] == 0)
    def _unused_tile():
        o_ref[...] = jnp.zeros(o_ref.shape, o_ref.dtype)


def _dense_sched(n_tiles):
    first = np.zeros((n_tiles,), np.int32)
    first[0] = 1
    return (jnp.zeros((n_tiles,), I32), jnp.asarray(first), jnp.ones((n_tiles,), I32),
            jnp.arange(n_tiles, dtype=I32))


def _tile_extra(bm, bn, col_off):
    return pl.BlockSpec((bm, bn), lambda j, t, te, tf, tv, tr: (tr[t], col_off + j))


def _batch_row_extra(bm, bn, rows_per_batch):
    return pl.BlockSpec((None, 1, bn), lambda j, t, te, tf, tv, tr: ((tr[t] * bm) // rows_per_batch, 0, j))


def _ws_matmul(name, compute, xs, ws, extras, out_rows, out_cols, out_dtype, bm, bn, sched):
    n_tiles = sched[0].shape[0]
    in_specs, args, blk = [], [], 0
    for arr, k, kb in xs:
        in_specs.append(pl.BlockSpec((bm, k), lambda j, t, te, tf, tv, tr, kb=kb: (tr[t], kb)))
        args.append(arr)
        blk += _nbytes((bm, k), arr.dtype)
    scratch, scratch_bytes = [], 0
    for arr, k, kb in ws:
        in_specs.append(pl.BlockSpec((None, k, bn), lambda j, t, te, tf, tv, tr, kb=kb: (te[t], kb, j)))
        args.append(arr)
        blk += _nbytes((k, bn), arr.dtype)
        scratch.append(pltpu.VMEM((k, bn), BF16))
        scratch_bytes += _nbytes((k, bn), BF16)
    for arr, spec in extras:
        in_specs.append(spec)
        args.append(arr)
        blk += _nbytes([d for d in spec.block_shape if d is not None], arr.dtype)
    blk += _nbytes((bm, bn), out_dtype)
    scratch_bytes += 2 * _nbytes((bm, bn), F32)
    kern = functools.partial(_ws_kernel, nx=len(xs), nw=len(ws), ne=len(extras), compute=compute)
    return pl.pallas_call(
        kern,
        grid_spec=pltpu.PrefetchScalarGridSpec(
            num_scalar_prefetch=4, grid=(out_cols // bn, n_tiles), in_specs=in_specs,
            out_specs=pl.BlockSpec((bm, bn), lambda j, t, te, tf, tv, tr: (t, j)),
            scratch_shapes=scratch),
        out_shape=jax.ShapeDtypeStruct((out_rows, out_cols), out_dtype),
        compiler_params=_cparams(("arbitrary", "arbitrary"), blk, scratch_bytes),
        name=name,
    )(*sched, *args)


def _mm(x, wb_ref):
    return jnp.dot(x, wb_ref[...], preferred_element_type=F32)


def _plain_compute(xr, wr, er):
    return _mm(xr[0][...], wr[0])


def _swiglu_compute(xr, wr, er):
    x = xr[0][...]
    a = _mm(x, wr[0])
    return a * jax.nn.sigmoid(a) * _mm(x, wr[1])


def _merge_compute(xr, wr, er):
    conv_out = _mm(xr[0][...], wr[0])
    attn_out = _mm(xr[1][...], wr[1])
    gc = er[0][...].astype(F32)
    ga = er[1][...].astype(F32)
    return jax.nn.sigmoid(gc) * conv_out + jax.nn.sigmoid(ga) * attn_out


def _residual_compute(xr, wr, er):
    return er[0][...] + er[1][...] * _mm(xr[0][...], wr[0])


CONV_HALO = 32
CONV_ROW_CHUNK = 32
CONV_CH_CHUNK = 512


def _conv_kernel(a_ref, b_ref, ah_ref, bh_ref, w_ref, cb_ref, g_ref, beta_ref, o_ref, ubuf, cbuf,
                 *, ts, width, ch):
    i = pl.program_id(1)
    a = a_ref[...].astype(F32)
    b = b_ref[...].astype(F32)
    ubuf[CONV_HALO:CONV_HALO + ts, :] = a * jax.nn.sigmoid(b)
    ah = ah_ref[...].astype(F32)
    bh = bh_ref[...].astype(F32)
    ubuf[0:CONV_HALO, :] = jnp.where(i > 0, ah * jax.nn.sigmoid(bh), 0.0)
    off = CONV_HALO - (width - 1)
    cch = min(CONV_CH_CHUNK, ch)
    rch = min(CONV_ROW_CHUNK, ts)
    for c0 in range(0, ch, cch):
        taps = [w_ref[j:j + 1, c0:c0 + cch] for j in range(width)]
        bias = cb_ref[:, c0:c0 + cch]
        for r0 in range(0, ts, rch):
            acc = jnp.zeros((rch, cch), F32)
            for j in range(width):
                acc = acc + taps[j] * ubuf[off + r0 + j:off + r0 + j + rch, c0:c0 + cch]
            cbuf[r0:r0 + rch, c0:c0 + cch] = acc + bias
    y = cbuf[...]
    mu = jnp.mean(y, axis=-1, keepdims=True)
    yc = y - mu
    var = jnp.mean(yc * yc, axis=-1, keepdims=True)
    yn = yc * lax.rsqrt(var + 1e-5) * g_ref[...] + beta_ref[...]
    o_ref[...] = (yn * jax.nn.sigmoid(yn)).astype(o_ref.dtype)


def _conv_branch(p3, conv_w, conv_b, ln_g, ln_b):
    b, s, _ = p3.shape
    width, ch = conv_w.shape
    assert width - 1 <= CONV_HALO
    ts = _fit(s, 128, CONV_HALO)
    hb = ts // CONV_HALO
    tile = lambda col: pl.BlockSpec((None, ts, ch), lambda bi, i, col=col: (bi, i, col))
    halo = lambda col: pl.BlockSpec((None, CONV_HALO, ch),
                                    lambda bi, i, col=col: (bi, jnp.maximum(i * hb - 1, 0), col))
    row = pl.BlockSpec((1, ch), lambda bi, i: (0, 0))
    blk = 2 * _nbytes((ts + CONV_HALO, ch), BF16) + _nbytes((width + 3, ch), F32) + _nbytes((ts, ch), BF16)
    scratch_bytes = _nbytes((2 * ts + CONV_HALO, ch), F32)
    return pl.pallas_call(
        functools.partial(_conv_kernel, ts=ts, width=width, ch=ch),
        grid=(b, s // ts),
        in_specs=[tile(0), tile(1), halo(0), halo(1),
                  pl.BlockSpec((width, ch), lambda bi, i: (0, 0)), row, row, row],
        out_specs=pl.BlockSpec((None, ts, ch), lambda bi, i: (bi, i, 0)),
        out_shape=jax.ShapeDtypeStruct((b, s, ch), BF16),
        scratch_shapes=[pltpu.VMEM((ts + CONV_HALO, ch), F32), pltpu.VMEM((ts, ch), F32)],
        compiler_params=_cparams(("arbitrary", "arbitrary"), blk, scratch_bytes + 4 * _nbytes((ts, ch), F32)),
        name="conv_module",
    )(p3, p3, p3, p3, conv_w, conv_b.reshape(1, ch), ln_g.reshape(1, ch), ln_b.reshape(1, ch))


def _bias_kernel(rel_ref, idx_ref, o_ref, *, n_buckets, nh, span, blk):
    g = pl.program_id(0)
    h = pl.program_id(1)
    idx = idx_ref[...]
    acc = jnp.zeros(idx.shape, F32)
    for bkt in range(n_buckets):
        acc = jnp.where(idx == bkt, rel_ref[bkt, g * nh + h], acc)
    qi = lax.broadcasted_iota(I32, idx.shape, 0) + blk
    kj = lax.broadcasted_iota(I32, idx.shape, 1)
    delta = qi - kj
    o_ref[...] = jnp.where((delta >= 0) & (delta <= span), acc, NEG_INF)


def _t5_bucket(dist, n_buckets):
    max_exact = n_buckets // 2
    n = jnp.maximum(dist, 1).astype(F32)
    large = max_exact + (jnp.log(n / max_exact) / math.log(MAX_DISTANCE / max_exact)
                         * (n_buckets - max_exact)).astype(I32)
    large = jnp.minimum(large, n_buckets - 1)
    return jnp.where(dist < max_exact, dist, large)


def _bias_tables(rel_bias, nh):
    n_buckets = rel_bias.shape[0]
    blk = ATTN_BLOCK
    span = ATTN_GROUPS[0][0] // ATTN_GROUPS[0][1]
    assert all(w // d == span for w, d in ATTN_GROUPS)
    delta = (jnp.arange(blk, dtype=I32)[:, None] + blk) - jnp.arange(2 * blk, dtype=I32)[None, :]
    idx = jnp.stack([_t5_bucket(jnp.maximum(delta, 0) * d, n_buckets) for _, d in ATTN_GROUPS])
    ng = len(ATTN_GROUPS)
    return pl.pallas_call(
        functools.partial(_bias_kernel, n_buckets=n_buckets, nh=nh, span=span, blk=blk),
        grid=(ng, nh),
        in_specs=[pl.BlockSpec(memory_space=pltpu.SMEM),
                  pl.BlockSpec((None, blk, 2 * blk), lambda g, h: (g, 0, 0))],
        out_specs=pl.BlockSpec((None, None, blk, 2 * blk), lambda g, h: (g, h, 0, 0)),
        out_shape=jax.ShapeDtypeStruct((ng, nh, blk, 2 * blk), F32),
        name="rel_bias_tables",
    )(rel_bias, idx)


def _attn_kernel(q_ref, kp_ref, ko_ref, vp_ref, vo_ref, qg_ref, kg_ref, bias_ref, o_ref, lse_ref,
                 *, nh, hd, blk):
    has_prev = pl.program_id(2) > 0
    qg = qg_ref[...]
    kg = kg_ref[...]
    scale = hd ** -0.5
    nt = (((1,), (1,)), ((), ()))

    def qk_norm(ref, sl, gain):
        v = ref[:, sl].astype(F32)
        return (v * lax.rsqrt(jnp.mean(v * v, axis=-1, keepdims=True) + 1e-6) * gain).astype(BF16)

    for h in range(nh):
        sl = slice(h * hd, (h + 1) * hd)
        q = qk_norm(q_ref, sl, qg)
        kp = qk_norm(kp_ref, sl, kg)
        ko = qk_norm(ko_ref, sl, kg)
        bias = bias_ref[h]
        s_p = lax.dot_general(q, kp, nt, preferred_element_type=F32) * scale \
            + jnp.where(has_prev, bias[:, :blk], NEG_INF)
        s_o = lax.dot_general(q, ko, nt, preferred_element_type=F32) * scale + bias[:, blk:]
        m = jnp.maximum(jnp.max(s_p, axis=-1, keepdims=True), jnp.max(s_o, axis=-1, keepdims=True))
        p_p = jnp.exp(s_p - m)
        p_o = jnp.exp(s_o - m)
        l = jnp.sum(p_p, axis=-1, keepdims=True) + jnp.sum(p_o, axis=-1, keepdims=True)
        acc = jnp.dot(p_p.astype(BF16), vp_ref[:, sl], preferred_element_type=F32) \
            + jnp.dot(p_o.astype(BF16), vo_ref[:, sl], preferred_element_type=F32)
        o_ref[:, sl] = (acc / l).astype(o_ref.dtype)
        lse_ref[:, sl] = jnp.broadcast_to(m + jnp.log(l), (blk, hd))


def _attn_group(p3, g, dil, in_w, q_off, att_w, nh, hd, q_gain, k_gain, bias_g):
    b, s, _ = p3.shape
    blk = ATTN_BLOCK
    seq = s // dil
    assert s % dil == 0 and seq % blk == 0
    nb = seq // blk
    aw = nh * hd
    assert in_w % aw == 0 and q_off % aw == 0 and att_w % aw == 0
    pv = p3.reshape(b, seq, dil * in_w)
    cols = lambda off: (lambda r: (r * in_w + off + g * aw) // aw)
    qc, kc, vc = cols(q_off), cols(q_off + att_w), cols(q_off + 2 * att_w)
    own = lambda c: pl.BlockSpec((None, blk, aw), lambda bi, r, n, c=c: (bi, n, c(r)))
    prev = lambda c: pl.BlockSpec((None, blk, aw), lambda bi, r, n, c=c: (bi, jnp.maximum(n - 1, 0), c(r)))
    gain = pl.BlockSpec((1, hd), lambda bi, r, n: (0, 0))
    out_spec = pl.BlockSpec((None, blk, aw), lambda bi, r, n: (bi, n, r))
    blk_bytes = 5 * _nbytes((blk, aw), BF16) + _nbytes((nh, blk, 2 * blk), F32) \
        + _nbytes((blk, aw), BF16) + _nbytes((blk, aw), F32)
    o, lse = pl.pallas_call(
        functools.partial(_attn_kernel, nh=nh, hd=hd, blk=blk),
        grid=(b, dil, nb),
        in_specs=[own(qc), prev(kc), own(kc), prev(vc), own(vc), gain, gain,
                  pl.BlockSpec((nh, blk, 2 * blk), lambda bi, r, n: (0, 0, 0))],
        out_specs=[out_spec, out_spec],
        out_shape=[jax.ShapeDtypeStruct((b, seq, dil * aw), BF16),
                   jax.ShapeDtypeStruct((b, seq, dil * aw), F32)],
        compiler_params=_cparams(("arbitrary", "arbitrary", "arbitrary"), blk_bytes),
        name=f"dilated_attn_g{g}",
    )(pv, pv, pv, pv, pv, q_gain.reshape(1, hd), k_gain.reshape(1, hd), bias_g)
    return o.reshape(b, s, aw), lse.reshape(b, s, aw)


def _group_merge_kernel(*refs, ng):
    o_refs, l_refs, out_ref = refs[:ng], refs[ng:2 * ng], refs[2 * ng]
    ls = [r[...] for r in l_refs]
    m = functools.reduce(jnp.maximum, ls)
    es = [jnp.exp(l - m) for l in ls]
    den = functools.reduce(lambda a, c: a + c, es)
    num = functools.reduce(lambda a, c: a + c, [e * r[...].astype(F32) for e, r in zip(es, o_refs)])
    out_ref[...] = (num / den).astype(out_ref.dtype)


def _group_merge(outs, lses):
    b, s, aw = outs[0].shape
    ng = len(outs)
    ts = _fit(s, 512, 16)
    tile = pl.BlockSpec((None, ts, aw), lambda i, j: (i, j, 0))
    blk = ng * (_nbytes((ts, aw), BF16) + _nbytes((ts, aw), F32)) + _nbytes((ts, aw), BF16)
    return pl.pallas_call(
        functools.partial(_group_merge_kernel, ng=ng),
        grid=(b, s // ts), in_specs=[tile] * (2 * ng), out_specs=tile,
        out_shape=jax.ShapeDtypeStruct((b, s, aw), BF16),
        compiler_params=_cparams(("arbitrary", "arbitrary"), blk, 4 * _nbytes((ts, aw), F32)),
        name="attn_group_merge")(*outs, *lses)


def _row_copy(src_hbm, row, dst, slot, sem):
    return pltpu.make_async_copy(src_hbm.at[pl.ds(row, 1), :], dst.at[pl.ds(slot, 1), :], sem)


def _gather_kernel(src_ref, h_hbm, o_ref, buf, sem, *, rows):
    base = pl.program_id(0) * rows

    def start(i, carry):
        _row_copy(h_hbm, src_ref[base + i], buf, i, sem).start()
        return carry

    def wait(i, carry):
        _row_copy(h_hbm, 0, buf, i, sem).wait()
        return carry

    lax.fori_loop(0, rows, start, 0)
    lax.fori_loop(0, rows, wait, 0)
    o_ref[...] = buf[...].astype(o_ref.dtype)


def _moe_gather(h2, src, rows):
    r_total = src.shape[0]
    d = h2.shape[1]
    return pl.pallas_call(
        functools.partial(_gather_kernel, rows=rows),
        grid_spec=pltpu.PrefetchScalarGridSpec(
            num_scalar_prefetch=1, grid=(r_total // rows,),
            in_specs=[pl.BlockSpec(memory_space=pl.ANY)],
            out_specs=pl.BlockSpec((rows, d), lambda t, src: (t, 0)),
            scratch_shapes=[pltpu.VMEM((rows, d), F32), pltpu.SemaphoreType.DMA(())]),
        out_shape=jax.ShapeDtypeStruct((r_total, d), BF16),
        compiler_params=_cparams(("arbitrary",), _nbytes((rows, d), BF16), 2 * _nbytes((rows, d), F32)),
        name="moe_gather",
    )(src, h2)


def _combine_kernel(pos_ref, y_hbm, x_ref, tw_ref, gt_ref, o_ref, buf, sem, *, rows):
    base = pl.program_id(0) * rows

    def start(i, carry):
        for k in range(TOP_K):
            _row_copy(y_hbm, pos_ref[(base + i) * TOP_K + k], buf.at[k], i, sem).start()
        return carry

    def wait(i, carry):
        for k in range(TOP_K):
            _row_copy(y_hbm, 0, buf.at[k], i, sem).wait()
        return carry

    lax.fori_loop(0, rows, start, 0)
    lax.fori_loop(0, rows, wait, 0)
    tw = tw_ref[...]
    y = tw[:, 0:1] * buf[0]
    for k in range(1, TOP_K):
        y = y + tw[:, k:k + 1] * buf[k]
    o_ref[...] = x_ref[...] + gt_ref[...] * y


def _moe_combine(y_sorted, pos, x2, topw2, gate, rows_per_batch):
    m, d = x2.shape
    rows = _fit(rows_per_batch, 128, 8)
    blk = 2 * _nbytes((rows, d), F32) + _nbytes((rows, V7X_LANES), F32) + _nbytes((1, d), F32)
    return pl.pallas_call(
        functools.partial(_combine_kernel, rows=rows),
        grid_spec=pltpu.PrefetchScalarGridSpec(
            num_scalar_prefetch=1, grid=(m // rows,),
            in_specs=[pl.BlockSpec(memory_space=pl.ANY),
                      pl.BlockSpec((rows, d), lambda t, pos: (t, 0)),
                      pl.BlockSpec((rows, V7X_LANES), lambda t, pos: (t, 0)),
                      pl.BlockSpec((None, 1, d), lambda t, pos: ((t * rows) // rows_per_batch, 0, 0))],
            out_specs=pl.BlockSpec((rows, d), lambda t, pos: (t, 0)),
            scratch_shapes=[pltpu.VMEM((TOP_K, rows, d), F32), pltpu.SemaphoreType.DMA(())]),
        out_shape=jax.ShapeDtypeStruct((m, d), F32),
        compiler_params=_cparams(("arbitrary",), blk, (TOP_K + 2) * _nbytes((rows, d), F32)),
        name="moe_combine",
    )(pos, y_sorted, x2, topw2, gate)


def _moe_schedule(top_i, n_exp, bm):
    m = top_i.shape[0]
    n_assign = m * TOP_K
    n_tiles = n_assign // bm + n_exp
    e = top_i.reshape(n_assign)
    onehot = (e[:, None] == jnp.arange(n_exp, dtype=I32)[None, :]).astype(I32)
    rank = jnp.sum((jnp.cumsum(onehot, axis=0) - onehot) * onehot, axis=1)
    counts = jnp.sum(onehot, axis=0)
    tiles_per = (counts + bm - 1) // bm
    tile_end = jnp.cumsum(tiles_per)
    row_start = (tile_end - tiles_per) * bm
    pos = jnp.sum(onehot * row_start[None, :], axis=1) + rank
    used = tile_end[-1]
    t = jnp.arange(n_tiles, dtype=I32)
    tr = jnp.minimum(t, used - 1)
    te = jnp.minimum(jnp.sum((tr[:, None] >= tile_end[None, :]).astype(I32), axis=1), n_exp - 1)
    tv = (t < used).astype(I32)
    prev_e = jnp.concatenate([jnp.full((1,), -1, I32), te[:-1]])
    tf = tv * ((te != prev_e).astype(I32))
    src = jnp.zeros((n_tiles * bm,), I32).at[pos].set(jnp.arange(n_assign, dtype=I32) // TOP_K)
    return (te.astype(I32), tf.astype(I32), tv, tr.astype(I32)), pos.astype(I32), src


def kernel(x, c, w_ada, b_ada, g_mix, g_ffn, w_in, conv_w, conv_b, conv_ln_g, conv_ln_b, w_conv_out,
           q_gain, k_gain, rel_bias, w_attn_out, w_out, ffn_w1, ffn_w3, ffn_w2, moe_router, moe_w1,
           moe_w3, moe_w2):
    b, s, d = x.shape
    m = b * s
    depth = w_ada.shape[0]
    in_w = w_in.shape[2]
    ch = conv_w.shape[2]
    hd = q_gain.shape[1]
    att_out_w = w_attn_out.shape[1]
    nh = att_out_w // hd
    ng = len(ATTN_GROUPS)
    att_w = rel_bias.shape[1] * hd
    q_off = 2 * ch
    gate_off = q_off + 3 * att_w
    assert in_w == gate_off + 2 * d and rel_bias.shape[1] == ng * nh

    bm = _fit(s, 1024)
    sched = _dense_sched(m // bm)
    bn = _fit(d, 512)
    assert gate_off % bn == 0

    mod = _adaln(c, w_ada, b_ada)
    bias = _bias_tables(rel_bias, nh)

    for l in range(depth):
        sh1, sc1, gt1, sh2, sc2, gt2 = [mod[l, :, i * d:(i + 1) * d] for i in range(6)]
        gt1 = gt1.reshape(b, 1, d)
        gt2 = gt2.reshape(b, 1, d)

        h = _rmsmod(x, g_mix[l], sh1, sc1, BF16).reshape(m, d)
        p = _ws_matmul("in_proj", _plain_compute, [(h, d, 0)], [(w_in[l][None], d, 0)], [],
                       m, in_w, BF16, bm, _fit(in_w, 512), sched)
        p3 = p.reshape(b, s, in_w)
        cu = _conv_branch(p3, conv_w[l], conv_b[l], conv_ln_g[l], conv_ln_b[l]).reshape(m, ch)
        outs, lses = [], []
        for g, (_, dil) in enumerate(ATTN_GROUPS):
            o_g, lse_g = _attn_group(p3, g, dil, in_w, q_off, att_w, nh, hd, q_gain[l], k_gain[l], bias[g])
            outs.append(o_g)
            lses.append(lse_g)
        o = _group_merge(outs, lses).reshape(m, att_out_w)
        merged = _ws_matmul(
            "branch_merge", _merge_compute, [(cu, ch, 0), (o, att_out_w, 0)],
            [(w_conv_out[l][None], ch, 0), (w_attn_out[l][None], att_out_w, 0)],
            [(p, _tile_extra(bm, bn, gate_off // bn)), (p, _tile_extra(bm, bn, (gate_off + d) // bn))],
            m, d, BF16, bm, bn, sched)
        x2 = _ws_matmul(
            "out_proj", _residual_compute, [(merged, d, 0)], [(w_out[l][None], d, 0)],
            [(x.reshape(m, d), _tile_extra(bm, bn, 0)), (gt1, _batch_row_extra(bm, bn, s))],
            m, d, F32, bm, bn, sched)
        x = x2.reshape(b, s, d)

        j = l // 2
        if l % 2 == 0:
            h = _rmsmod(x, g_ffn[l], sh2, sc2, BF16).reshape(m, d)
            d_ff = ffn_w1.shape[2]
            hid = _ws_matmul("ffn_up", _swiglu_compute, [(h, d, 0)],
                             [(ffn_w1[j][None], d, 0), (ffn_w3[j][None], d, 0)], [],
                             m, d_ff, BF16, bm, _fit(d_ff, 256), sched)
            ksplit = 2 if d_ff % (2 * V7X_LANES) == 0 else 1
            kc = d_ff // ksplit
            bm2 = _fit(s, 512)
            sched2 = _dense_sched(m // bm2)
            for kb in range(ksplit):
                x2 = _ws_matmul(
                    "ffn_down", _residual_compute, [(hid, kc, kb)], [(ffn_w2[j][None], kc, kb)],
                    [(x2, _tile_extra(bm2, bn, 0)), (gt2, _batch_row_extra(bm2, bn, s))],
                    m, d, F32, bm2, bn, sched2)
            x = x2.reshape(b, s, d)
        else:
            n_exp = moe_router.shape[2]
            d_fe = moe_w1.shape[3]
            hf, topw, topi = _rmsmod(x, g_ffn[l], sh2, sc2, F32, w_router=moe_router[j])
            bme = _fit(m, 512)
            msched, pos, src = _moe_schedule(topi.reshape(m, V7X_LANES)[:, :TOP_K], n_exp, bme)
            n_rows = src.shape[0]
            xs = _moe_gather(hf.reshape(m, d), src, _fit(bme, 128, 16))
            hid = _ws_matmul("moe_up", _swiglu_compute, [(xs, d, 0)],
                             [(moe_w1[j], d, 0), (moe_w3[j], d, 0)], [],
                             n_rows, d_fe, BF16, bme, _fit(d_fe, 256), msched)
            ys = _ws_matmul("moe_down", _plain_compute, [(hid, d_fe, 0)], [(moe_w2[j], d_fe, 0)], [],
                            n_rows, d, F32, bme, bn, msched)
            x = _moe_combine(ys, pos, x2, topw.reshape(m, V7X_LANES), gt2, s).reshape(b, s, d)
    return x
```

```python
import functools
import math

import numpy as np
import jax
import jax.numpy as jnp
from jax import lax
from jax.experimental import pallas as pl
from jax.experimental.pallas import tpu as pltpu

F32 = jnp.float32
BF16 = jnp.bfloat16
I32 = jnp.int32

ATTN_GROUPS = ((128, 1), (512, 4), (2048, 16))
ATTN_BLOCK = 128
MAX_DISTANCE = 2048
TOP_K = 2

V7X_VMEM_BYTES = 64 * 1024 * 1024
V7X_LANES = 128
VMEM_REQUEST_CAP = V7X_VMEM_BYTES - 6 * 1024 * 1024
VMEM_SLACK = 8 * 1024 * 1024
NEG_INF = float("-inf")


def _fit(n, pref, unit=V7X_LANES):
    best = None
    for d in range(unit, min(n, pref) + 1, unit):
        if n % d == 0:
            best = d
    return best if best is not None else n


def _nbytes(shape, dtype):
    return int(np.prod(shape)) * jnp.dtype(dtype).itemsize


def _cparams(semantics, block_bytes, scratch_bytes=0):
    limit = min(VMEM_REQUEST_CAP, 2 * block_bytes + scratch_bytes + VMEM_SLACK)
    return pltpu.CompilerParams(dimension_semantics=semantics, vmem_limit_bytes=int(limit))


def _ada_kernel(c_ref, w_ref, b_ref, o_ref):
    c = c_ref[...]
    ca = c * jax.nn.sigmoid(c)
    o_ref[...] = jnp.dot(ca, w_ref[...], preferred_element_type=F32,
                         precision=lax.Precision.HIGHEST) + b_ref[...]


def _adaln(c, w_ada, b_ada):
    depth, d, n = w_ada.shape
    b = c.shape[0]
    rows = 8
    c8 = jnp.pad(c, ((0, rows - b), (0, 0)))
    bn = _fit(n, 512)
    blk = _nbytes((rows, d), F32) + _nbytes((d, bn), F32) + 2 * _nbytes((rows, bn), F32)
    out = pl.pallas_call(
        _ada_kernel,
        grid=(depth, n // bn),
        in_specs=[pl.BlockSpec((rows, d), lambda l, j: (0, 0)),
                  pl.BlockSpec((None, d, bn), lambda l, j: (l, 0, j)),
                  pl.BlockSpec((None, 1, bn), lambda l, j: (l, 0, j))],
        out_specs=pl.BlockSpec((None, rows, bn), lambda l, j: (l, 0, j)),
        out_shape=jax.ShapeDtypeStruct((depth, rows, n), F32),
        compiler_params=_cparams(("arbitrary", "arbitrary"), blk, 3 * _nbytes((d, bn), F32)),
        name="adaln",
    )(c8, w_ada, b_ada.reshape(depth, 1, n))
    return out[:, :b]


def _normed(x_ref, g_ref, sh_ref, sc_ref):
    x = x_ref[...]
    ms = jnp.mean(x * x, axis=-1, keepdims=True)
    y = x * lax.rsqrt(ms + 1e-6) * g_ref[...]
    return y * (1.0 + sc_ref[...]) + sh_ref[...]


def _rmsmod_kernel(x_ref, g_ref, sh_ref, sc_ref, o_ref):
    o_ref[...] = _normed(x_ref, g_ref, sh_ref, sc_ref).astype(o_ref.dtype)


def _rmsmod_route_kernel(x_ref, g_ref, sh_ref, sc_ref, wr_ref, h_ref, tw_ref, ti_ref, cnt_ref, *, n_exp):
    h = _normed(x_ref, g_ref, sh_ref, sc_ref)
    h_ref[...] = h
    logits = jnp.dot(h, wr_ref[...], preferred_element_type=F32, precision=lax.Precision.HIGHEST)
    lane = lax.broadcasted_iota(I32, logits.shape, 1)
    lanef = lane.astype(F32)
    sentinel = float(V7X_LANES)
    lg = jnp.where(lane < n_exp, logits, NEG_INF)
    m1 = jnp.max(lg, axis=-1, keepdims=True)
    i1 = jnp.min(jnp.where(lg == m1, lanef, sentinel), axis=-1, keepdims=True)
    lg2 = jnp.where(lanef == i1, NEG_INF, lg)
    m2 = jnp.max(lg2, axis=-1, keepdims=True)
    i2 = jnp.min(jnp.where(lg2 == m2, lanef, sentinel), axis=-1, keepdims=True)
    e2 = jnp.exp(m2 - m1)
    den = 1.0 + e2
    tw_ref[...] = jnp.where(lane == 0, 1.0 / den, jnp.where(lane == 1, e2 / den, 0.0))

    @pl.when((pl.program_id(0) == 0) & (pl.program_id(1) == 0))
    def _init_counts():
        cnt_ref[...] = jnp.zeros(cnt_ref.shape, F32)

    oh1 = (lanef == i1).astype(F32)
    oh2 = (lanef == i2).astype(F32)
    oh = oh1 + oh2
    ts = oh.shape[0]
    earlier = (lax.broadcasted_iota(I32, (ts, ts), 1) < lax.broadcasted_iota(I32, (ts, ts), 0))
    before = cnt_ref[0:1, :] + jnp.dot(earlier.astype(BF16), oh.astype(BF16), preferred_element_type=F32)
    r1 = jnp.sum(oh1 * before, axis=-1, keepdims=True)
    r2 = jnp.sum(oh2 * before, axis=-1, keepdims=True)
    cnt_ref[...] = cnt_ref[...] + jnp.sum(oh, axis=0, keepdims=True)
    ti_ref[...] = jnp.where(lane == 0, i1, jnp.where(lane == 1, i2, jnp.where(
        lane == 2, r1, jnp.where(lane == 3, r2, 0.0)))).astype(I32)


def _rmsmod(x, g, shift, scale, out_dtype, w_router=None):
    b, s, d = x.shape
    ts = _fit(s, 256, 8)
    specs = [pl.BlockSpec((None, ts, d), lambda i, j: (i, j, 0)),
             pl.BlockSpec((1, d), lambda i, j: (0, 0)),
             pl.BlockSpec((None, 1, d), lambda i, j: (i, 0, 0)),
             pl.BlockSpec((None, 1, d), lambda i, j: (i, 0, 0))]
    args = [x, g.reshape(1, d), shift.reshape(b, 1, d), scale.reshape(b, 1, d)]
    tile = pl.BlockSpec((None, ts, d), lambda i, j: (i, j, 0))
    blk = _nbytes((ts, d), F32) * 2 + 3 * _nbytes((1, d), F32)
    if w_router is None:
        return pl.pallas_call(
            _rmsmod_kernel, grid=(b, s // ts), in_specs=specs, out_specs=tile,
            out_shape=jax.ShapeDtypeStruct((b, s, d), out_dtype),
            compiler_params=_cparams(("arbitrary", "arbitrary"), blk, 4 * _nbytes((ts, d), F32)),
            name="rmsmod")(*args)
    n_exp = w_router.shape[1]
    wr = jnp.pad(w_router, ((0, 0), (0, V7X_LANES - n_exp)))
    lanes = pl.BlockSpec((None, ts, V7X_LANES), lambda i, j: (i, j, 0))
    blk += _nbytes((d, V7X_LANES), F32) + 2 * _nbytes((ts, V7X_LANES), F32)
    return pl.pallas_call(
        functools.partial(_rmsmod_route_kernel, n_exp=n_exp),
        grid=(b, s // ts),
        in_specs=specs + [pl.BlockSpec((d, V7X_LANES), lambda i, j: (0, 0))],
        out_specs=[tile, lanes, lanes, pl.BlockSpec((8, V7X_LANES), lambda i, j: (0, 0))],
        out_shape=[jax.ShapeDtypeStruct((b, s, d), F32),
                   jax.ShapeDtypeStruct((b, s, V7X_LANES), F32),
                   jax.ShapeDtypeStruct((b, s, V7X_LANES), I32),
                   jax.ShapeDtypeStruct((8, V7X_LANES), F32)],
        compiler_params=_cparams(("arbitrary", "arbitrary"), blk, 8 * _nbytes((ts, d), F32)),
        name="rmsmod_route")(*args, wr)


def _ws_kernel(te_ref, tf_ref, tv_ref, tr_ref, wc_ref, *refs, nx, nw, ne, compute, dil):
    del te_ref, tr_ref, wc_ref
    x_refs = refs[:nx]
    w_refs = refs[nx:nx + nw]
    e_refs = refs[nx + nw:nx + nw + ne]
    o_ref = refs[nx + nw + ne]
    wb_refs = refs[nx + nw + ne + 1:nx + nw + ne + 1 + nw]
    t = pl.program_id(1)

    @pl.when(tf_ref[t] == 1)
    def _cast_weights():
        for w_ref, wb_ref in zip(w_refs, wb_refs):
            wb_ref[...] = w_ref[...].astype(BF16)

    @pl.when(tv_ref[t] == 1)
    def _compute():
        res = compute(x_refs, wb_refs, e_refs)
        if dil == 1:
            o_ref[...] = res.astype(o_ref.dtype)
        else:
            perm_ref = refs[-1]
            n_chunks, bm, lanes = perm_ref.shape
            rows = bm // dil
            for c in range(n_chunks):
                perm_ref[c] = res[:, c * lanes:(c + 1) * lanes]
            for r in range(dil):
                for c in range(n_chunks):
                    o_ref[r, :, c * lanes:(c + 1) * lanes] = \
                        perm_ref[c, pl.ds(r, rows, stride=dil), :].astype(o_ref.dtype)

    @pl.when(tv_ref[t] == 0)
    def _unused_tile():
        o_ref[...] = jnp.zeros(o_ref.shape, o_ref.dtype)


def _dense_sched(n_tiles):
    first = np.zeros((n_tiles,), np.int32)
    first[0] = 1
    return (jnp.zeros((n_tiles,), I32), jnp.asarray(first), jnp.ones((n_tiles,), I32),
            jnp.arange(n_tiles, dtype=I32))


def _tile_extra(bm, bn, col_off):
    return pl.BlockSpec((bm, bn), lambda j, t, te, tf, tv, tr, wc: (tr[t], col_off + j))


def _batch_row_extra(bm, bn, rows_per_batch):
    return pl.BlockSpec((None, 1, bn),
                        lambda j, t, te, tf, tv, tr, wc: ((tr[t] * bm) // rows_per_batch, 0, j))


def _ws_matmul(name, compute, xs, ws, extras, out_rows, out_cols, out_dtype, bm, bn, sched,
               w_cols=None, dil=1, rows_per_batch=None):
    n_tiles = sched[0].shape[0]
    nj = out_cols // bn
    if w_cols is None:
        w_cols = np.arange(nj)
    assert len(w_cols) == nj
    w_cols = jnp.asarray(np.asarray(w_cols, np.int32))
    in_specs, args, blk = [], [], 0
    for arr, k, kb in xs:
        in_specs.append(pl.BlockSpec((bm, k), lambda j, t, te, tf, tv, tr, wc, kb=kb: (tr[t], kb)))
        args.append(arr)
        blk += _nbytes((bm, k), arr.dtype)
    scratch, scratch_bytes = [], 0
    for arr, k, kb in ws:
        in_specs.append(pl.BlockSpec((None, k, bn),
                                     lambda j, t, te, tf, tv, tr, wc, kb=kb: (te[t], kb, wc[j])))
        args.append(arr)
        blk += _nbytes((k, bn), arr.dtype)
        scratch.append(pltpu.VMEM((k, bn), BF16))
        scratch_bytes += _nbytes((k, bn), BF16)
    for arr, spec in extras:
        in_specs.append(spec)
        args.append(arr)
        blk += _nbytes([d for d in spec.block_shape if d is not None], arr.dtype)
    blk += _nbytes((bm, bn), out_dtype)
    scratch_bytes += 2 * _nbytes((bm, bn), F32)
    if dil == 1:
        out_spec = pl.BlockSpec((bm, bn), lambda j, t, te, tf, tv, tr, wc: (t, j))
        out_shape = jax.ShapeDtypeStruct((out_rows, out_cols), out_dtype)
    else:
        tpb = rows_per_batch // bm
        assert rows_per_batch % bm == 0 and bm % dil == 0
        out_spec = pl.BlockSpec((None, dil, bm // dil, bn),
                                lambda j, t, te, tf, tv, tr, wc: (t // tpb, 0, t % tpb, j))
        out_shape = jax.ShapeDtypeStruct((out_rows // rows_per_batch, dil, rows_per_batch // dil, out_cols),
                                         out_dtype)
        assert bn % V7X_LANES == 0
        scratch.append(pltpu.VMEM((bn // V7X_LANES, bm, V7X_LANES), F32))
        scratch_bytes += _nbytes((bm, bn), F32)
    kern = functools.partial(_ws_kernel, nx=len(xs), nw=len(ws), ne=len(extras), compute=compute, dil=dil)
    return pl.pallas_call(
        kern,
        grid_spec=pltpu.PrefetchScalarGridSpec(
            num_scalar_prefetch=5, grid=(nj, n_tiles), in_specs=in_specs,
            out_specs=out_spec, scratch_shapes=scratch),
        out_shape=out_shape,
        compiler_params=_cparams(("arbitrary", "arbitrary"), blk, scratch_bytes),
        name=name,
    )(*sched, w_cols, *args)


def _mm(x, wb_ref):
    return jnp.dot(x, wb_ref[...], preferred_element_type=F32)


def _plain_compute(xr, wr, er):
    return _mm(xr[0][...], wr[0])


def _swiglu_compute(xr, wr, er):
    x = xr[0][...]
    a = _mm(x, wr[0])
    return a * jax.nn.sigmoid(a) * _mm(x, wr[1])


def _merge_compute(xr, wr, er):
    conv_out = _mm(xr[0][...], wr[0])
    attn_out = _mm(xr[1][...], wr[1])
    gc = er[0][...].astype(F32)
    ga = er[1][...].astype(F32)
    return jax.nn.sigmoid(gc) * conv_out + jax.nn.sigmoid(ga) * attn_out


def _residual_compute(xr, wr, er):
    return er[0][...] + er[1][...] * _mm(xr[0][...], wr[0])


CONV_HALO = 32
CONV_ROW_CHUNK = 32
CONV_CH_CHUNK = 512


def _conv_kernel(a_ref, b_ref, ah_ref, bh_ref, w_ref, cb_ref, g_ref, beta_ref, o_ref, ubuf, cbuf,
                 *, ts, width, ch):
    i = pl.program_id(1)
    a = a_ref[...].astype(F32)
    b = b_ref[...].astype(F32)
    ubuf[CONV_HALO:CONV_HALO + ts, :] = a * jax.nn.sigmoid(b)
    ah = ah_ref[...].astype(F32)
    bh = bh_ref[...].astype(F32)
    ubuf[0:CONV_HALO, :] = jnp.where(i > 0, ah * jax.nn.sigmoid(bh), 0.0)
    off = CONV_HALO - (width - 1)
    cch = min(CONV_CH_CHUNK, ch)
    rch = min(CONV_ROW_CHUNK, ts)
    for c0 in range(0, ch, cch):
        taps = [w_ref[j:j + 1, c0:c0 + cch] for j in range(width)]
        bias = cb_ref[:, c0:c0 + cch]
        for r0 in range(0, ts, rch):
            acc = jnp.zeros((rch, cch), F32)
            for j in range(width):
                acc = acc + taps[j] * ubuf[off + r0 + j:off + r0 + j + rch, c0:c0 + cch]
            cbuf[r0:r0 + rch, c0:c0 + cch] = acc + bias
    y = cbuf[...]
    mu = jnp.mean(y, axis=-1, keepdims=True)
    yc = y - mu
    var = jnp.mean(yc * yc, axis=-1, keepdims=True)
    yn = yc * lax.rsqrt(var + 1e-5) * g_ref[...] + beta_ref[...]
    o_ref[...] = (yn * jax.nn.sigmoid(yn)).astype(o_ref.dtype)


def _conv_branch(p3, conv_w, conv_b, ln_g, ln_b):
    b, s, _ = p3.shape
    width, ch = conv_w.shape
    assert width - 1 <= CONV_HALO
    ts = _fit(s, 128, CONV_HALO)
    hb = ts // CONV_HALO
    tile = lambda col: pl.BlockSpec((None, ts, ch), lambda bi, i, col=col: (bi, i, col))
    halo = lambda col: pl.BlockSpec((None, CONV_HALO, ch),
                                    lambda bi, i, col=col: (bi, jnp.maximum(i * hb - 1, 0), col))
    row = pl.BlockSpec((1, ch), lambda bi, i: (0, 0))
    blk = 2 * _nbytes((ts + CONV_HALO, ch), BF16) + _nbytes((width + 3, ch), F32) + _nbytes((ts, ch), BF16)
    scratch_bytes = _nbytes((2 * ts + CONV_HALO, ch), F32)
    return pl.pallas_call(
        functools.partial(_conv_kernel, ts=ts, width=width, ch=ch),
        grid=(b, s // ts),
        in_specs=[tile(0), tile(1), halo(0), halo(1),
                  pl.BlockSpec((width, ch), lambda bi, i: (0, 0)), row, row, row],
        out_specs=pl.BlockSpec((None, ts, ch), lambda bi, i: (bi, i, 0)),
        out_shape=jax.ShapeDtypeStruct((b, s, ch), BF16),
        scratch_shapes=[pltpu.VMEM((ts + CONV_HALO, ch), F32), pltpu.VMEM((ts, ch), F32)],
        compiler_params=_cparams(("arbitrary", "arbitrary"), blk, scratch_bytes + 4 * _nbytes((ts, ch), F32)),
        name="conv_module",
    )(p3, p3, p3, p3, conv_w, conv_b.reshape(1, ch), ln_g.reshape(1, ch), ln_b.reshape(1, ch))


def _bias_kernel(rel_ref, idx_ref, o_ref, *, n_buckets, nh, span, blk):
    g = pl.program_id(0)
    h = pl.program_id(1)
    idx = idx_ref[...]
    acc = jnp.zeros(idx.shape, F32)
    for bkt in range(n_buckets):
        acc = jnp.where(idx == bkt, rel_ref[bkt, g * nh + h], acc)
    qi = lax.broadcasted_iota(I32, idx.shape, 0) + blk
    kj = lax.broadcasted_iota(I32, idx.shape, 1)
    delta = qi - kj
    o_ref[...] = jnp.where((delta >= 0) & (delta <= span), acc, NEG_INF)


def _t5_bucket(dist, n_buckets):
    max_exact = n_buckets // 2
    n = jnp.maximum(dist, 1).astype(F32)
    large = max_exact + (jnp.log(n / max_exact) / math.log(MAX_DISTANCE / max_exact)
                         * (n_buckets - max_exact)).astype(I32)
    large = jnp.minimum(large, n_buckets - 1)
    return jnp.where(dist < max_exact, dist, large)


def _bias_tables(rel_bias, nh):
    n_buckets = rel_bias.shape[0]
    blk = ATTN_BLOCK
    span = ATTN_GROUPS[0][0] // ATTN_GROUPS[0][1]
    assert all(w // d == span for w, d in ATTN_GROUPS)
    delta = (jnp.arange(blk, dtype=I32)[:, None] + blk) - jnp.arange(2 * blk, dtype=I32)[None, :]
    idx = jnp.stack([_t5_bucket(jnp.maximum(delta, 0) * d, n_buckets) for _, d in ATTN_GROUPS])
    ng = len(ATTN_GROUPS)
    return pl.pallas_call(
        functools.partial(_bias_kernel, n_buckets=n_buckets, nh=nh, span=span, blk=blk),
        grid=(ng, nh),
        in_specs=[pl.BlockSpec(memory_space=pltpu.SMEM),
                  pl.BlockSpec((None, blk, 2 * blk), lambda g, h: (g, 0, 0))],
        out_specs=pl.BlockSpec((None, None, blk, 2 * blk), lambda g, h: (g, h, 0, 0)),
        out_shape=jax.ShapeDtypeStruct((ng, nh, blk, 2 * blk), F32),
        name="rel_bias_tables",
    )(rel_bias, idx)


def _attn_kernel(q_ref, kp_ref, ko_ref, vp_ref, vo_ref, qg_ref, kg_ref, bias_ref, o_ref, lse_ref,
                 *, nh, hd, blk):
    has_prev = pl.program_id(2) > 0
    qg = qg_ref[...]
    kg = kg_ref[...]
    scale = hd ** -0.5
    nt = (((1,), (1,)), ((), ()))

    def qk_norm(ref, sl, gain):
        v = ref[:, sl].astype(F32)
        return (v * lax.rsqrt(jnp.mean(v * v, axis=-1, keepdims=True) + 1e-6) * gain).astype(BF16)

    for h in range(nh):
        sl = slice(h * hd, (h + 1) * hd)
        q = qk_norm(q_ref, sl, qg)
        kp = qk_norm(kp_ref, sl, kg)
        ko = qk_norm(ko_ref, sl, kg)
        bias = bias_ref[h]
        s_p = lax.dot_general(q, kp, nt, preferred_element_type=F32) * scale \
            + jnp.where(has_prev, bias[:, :blk], NEG_INF)
        s_o = lax.dot_general(q, ko, nt, preferred_element_type=F32) * scale + bias[:, blk:]
        m = jnp.maximum(jnp.max(s_p, axis=-1, keepdims=True), jnp.max(s_o, axis=-1, keepdims=True))
        p_p = jnp.exp(s_p - m)
        p_o = jnp.exp(s_o - m)
        l = jnp.sum(p_p, axis=-1, keepdims=True) + jnp.sum(p_o, axis=-1, keepdims=True)
        acc = jnp.dot(p_p.astype(BF16), vp_ref[:, sl], preferred_element_type=F32) \
            + jnp.dot(p_o.astype(BF16), vo_ref[:, sl], preferred_element_type=F32)
        o_ref[:, sl] = (acc / l).astype(o_ref.dtype)
        lse_ref[:, sl] = jnp.broadcast_to(m + jnp.log(l), (blk, hd))


def _attn_group(pa, g, dil, qb, kb, vb, nh, hd, q_gain, k_gain, bias_g):
    b, s, _ = pa.shape
    blk = ATTN_BLOCK
    seq = s // dil
    assert s % dil == 0 and seq % blk == 0
    nb = seq // blk
    aw = nh * hd
    own = lambda c: pl.BlockSpec((None, blk, aw), lambda bi, r, n, c=c: (bi, r * nb + n, c))
    prev = lambda c: pl.BlockSpec((None, blk, aw),
                                  lambda bi, r, n, c=c: (bi, r * nb + jnp.maximum(n - 1, 0), c))
    gain = pl.BlockSpec((1, hd), lambda bi, r, n: (0, 0))
    out_spec = pl.BlockSpec((None, blk, aw), lambda bi, r, n: (bi, r * nb + n, 0))
    blk_bytes = 5 * _nbytes((blk, aw), BF16) + _nbytes((nh, blk, 2 * blk), F32) \
        + _nbytes((blk, aw), BF16) + _nbytes((blk, aw), F32)
    o, lse = pl.pallas_call(
        functools.partial(_attn_kernel, nh=nh, hd=hd, blk=blk),
        grid=(b, dil, nb),
        in_specs=[own(qb), prev(kb), own(kb), prev(vb), own(vb), gain, gain,
                  pl.BlockSpec((nh, blk, 2 * blk), lambda bi, r, n: (0, 0, 0))],
        out_specs=[out_spec, out_spec],
        out_shape=[jax.ShapeDtypeStruct((b, s, aw), BF16), jax.ShapeDtypeStruct((b, s, aw), F32)],
        compiler_params=_cparams(("arbitrary", "arbitrary", "arbitrary"), blk_bytes),
        name=f"dilated_attn_g{g}",
    )(pa, pa, pa, pa, pa, q_gain.reshape(1, hd), k_gain.reshape(1, hd), bias_g)
    return o.reshape(b, dil, seq, aw), lse.reshape(b, dil, seq, aw)


def _group_merge_kernel(*refs, dils, ts):
    ng = len(dils)
    o_refs, l_refs, out_ref = refs[:ng], refs[ng:2 * ng], refs[2 * ng]
    scratch = list(refs[2 * ng + 1:])

    def token_order(ref, dil):
        if dil == 1:
            return ref[0].astype(F32)
        buf = scratch.pop(0)
        n_chunks, _, lanes = buf.shape
        rows = ts // dil
        for r in range(dil):
            for c in range(n_chunks):
                buf[c, pl.ds(r, rows, stride=dil), :] = ref[r, :, c * lanes:(c + 1) * lanes].astype(F32)
        return jnp.concatenate([buf[c] for c in range(n_chunks)], axis=1)

    ls = [token_order(r, dil) for r, dil in zip(l_refs, dils)]
    m = functools.reduce(jnp.maximum, ls)
    es = [jnp.exp(l - m) for l in ls]
    den = functools.reduce(lambda a, c: a + c, es)
    num = None
    for e, r, dil in zip(es, o_refs, dils):
        term = e * token_order(r, dil)
        num = term if num is None else num + term
    out_ref[...] = (num / den).astype(out_ref.dtype)


def _group_merge(outs, lses, dils):
    b, _, _, aw = outs[0].shape
    s = outs[0].shape[1] * outs[0].shape[2]
    ts = _fit(s, 256, 16 * max(dils))
    spec = lambda dil: pl.BlockSpec((None, dil, ts // dil, aw), lambda i, j: (i, 0, j, 0))
    specs = [spec(dil) for dil in dils]
    n_perm = 2 * sum(1 for dil in dils if dil > 1)
    blk = len(dils) * (_nbytes((ts, aw), BF16) + _nbytes((ts, aw), F32)) + _nbytes((ts, aw), BF16)
    return pl.pallas_call(
        functools.partial(_group_merge_kernel, dils=tuple(dils), ts=ts),
        grid=(b, s // ts), in_specs=specs + specs,
        out_specs=pl.BlockSpec((None, ts, aw), lambda i, j: (i, j, 0)),
        out_shape=jax.ShapeDtypeStruct((b, s, aw), BF16),
        scratch_shapes=[pltpu.VMEM((aw // V7X_LANES, ts, V7X_LANES), F32)] * n_perm,
        compiler_params=_cparams(("arbitrary", "arbitrary"), blk, (n_perm + 4) * _nbytes((ts, aw), F32)),
        name="attn_group_merge")(*outs, *lses)


def _row_copy(src_hbm, row, dst, slot, sem):
    return pltpu.make_async_copy(src_hbm.at[pl.ds(row, 1), :], dst.at[pl.ds(slot, 1), :], sem)


ROW_DMA_UNROLL = 8


def _gather_kernel(src_ref, h_hbm, o_ref, buf, sem, *, rows):
    t = pl.program_id(0)
    slot = t % 2

    def fetch(step, dst_slot):
        def start(i, carry):
            _row_copy(h_hbm, src_ref[step * rows + i], buf.at[dst_slot], i, sem.at[dst_slot]).start()
            return carry
        lax.fori_loop(0, rows, start, 0, unroll=ROW_DMA_UNROLL)

    @pl.when(t == 0)
    def _first():
        fetch(0, 0)

    @pl.when(t + 1 < pl.num_programs(0))
    def _next():
        fetch(t + 1, 1 - slot)

    def wait(i, carry):
        _row_copy(h_hbm, 0, buf.at[slot], i, sem.at[slot]).wait()
        return carry

    lax.fori_loop(0, rows, wait, 0, unroll=ROW_DMA_UNROLL)
    o_ref[...] = buf[slot].astype(o_ref.dtype)


def _moe_gather(h2, src, rows):
    r_total = src.shape[0]
    d = h2.shape[1]
    return pl.pallas_call(
        functools.partial(_gather_kernel, rows=rows),
        grid_spec=pltpu.PrefetchScalarGridSpec(
            num_scalar_prefetch=1, grid=(r_total // rows,),
            in_specs=[pl.BlockSpec(memory_space=pl.ANY)],
            out_specs=pl.BlockSpec((rows, d), lambda t, src: (t, 0)),
            scratch_shapes=[pltpu.VMEM((2, rows, d), F32), pltpu.SemaphoreType.DMA((2,))]),
        out_shape=jax.ShapeDtypeStruct((r_total, d), BF16),
        compiler_params=_cparams(("arbitrary",), _nbytes((rows, d), BF16), 3 * _nbytes((rows, d), F32)),
        name="moe_gather",
    )(src, h2)


def _combine_kernel(pos_ref, y_hbm, x_ref, tw_ref, gt_ref, o_ref, buf, sem, *, rows):
    t = pl.program_id(0)
    slot = t % 2

    def fetch(step, dst_slot):
        def start(i, carry):
            for k in range(TOP_K):
                _row_copy(y_hbm, pos_ref[(step * rows + i) * TOP_K + k], buf.at[dst_slot, k], i,
                          sem.at[dst_slot]).start()
            return carry
        lax.fori_loop(0, rows, start, 0, unroll=ROW_DMA_UNROLL // TOP_K)

    @pl.when(t == 0)
    def _first():
        fetch(0, 0)

    @pl.when(t + 1 < pl.num_programs(0))
    def _next():
        fetch(t + 1, 1 - slot)

    def wait(i, carry):
        for k in range(TOP_K):
            _row_copy(y_hbm, 0, buf.at[slot, k], i, sem.at[slot]).wait()
        return carry

    lax.fori_loop(0, rows, wait, 0, unroll=ROW_DMA_UNROLL // TOP_K)
    tw = tw_ref[...]
    y = tw[:, 0:1] * buf[slot, 0]
    for k in range(1, TOP_K):
        y = y + tw[:, k:k + 1] * buf[slot, k]
    o_ref[...] = x_ref[...] + gt_ref[...] * y


def _moe_combine(y_sorted, pos, x2, topw2, gate, rows_per_batch):
    m, d = x2.shape
    rows = _fit(rows_per_batch, 128, 8)
    blk = 2 * _nbytes((rows, d), F32) + _nbytes((rows, V7X_LANES), F32) + _nbytes((1, d), F32)
    return pl.pallas_call(
        functools.partial(_combine_kernel, rows=rows),
        grid_spec=pltpu.PrefetchScalarGridSpec(
            num_scalar_prefetch=1, grid=(m // rows,),
            in_specs=[pl.BlockSpec(memory_space=pl.ANY),
                      pl.BlockSpec((rows, d), lambda t, pos: (t, 0)),
                      pl.BlockSpec((rows, V7X_LANES), lambda t, pos: (t, 0)),
                      pl.BlockSpec((None, 1, d), lambda t, pos: ((t * rows) // rows_per_batch, 0, 0))],
            out_specs=pl.BlockSpec((rows, d), lambda t, pos: (t, 0)),
            scratch_shapes=[pltpu.VMEM((2, TOP_K, rows, d), F32), pltpu.SemaphoreType.DMA((2,))]),
        out_shape=jax.ShapeDtypeStruct((m, d), F32),
        compiler_params=_cparams(("arbitrary",), blk, (2 * TOP_K + 2) * _nbytes((rows, d), F32)),
        name="moe_combine",
    )(pos, y_sorted, x2, topw2, gate)


def _moe_schedule(route_i, counts, n_exp, bm):
    m = route_i.shape[0]
    n_assign = m * TOP_K
    n_tiles = n_assign // bm + n_exp
    e = route_i[:, :TOP_K].reshape(n_assign)
    rank = route_i[:, TOP_K:2 * TOP_K].reshape(n_assign)
    onehot = (e[:, None] == jnp.arange(n_exp, dtype=I32)[None, :]).astype(I32)
    tiles_per = (counts + bm - 1) // bm
    tile_end = jnp.cumsum(tiles_per)
    row_start = (tile_end - tiles_per) * bm
    pos = jnp.sum(onehot * row_start[None, :], axis=1) + rank
    used = tile_end[-1]
    t = jnp.arange(n_tiles, dtype=I32)
    tr = jnp.minimum(t, used - 1)
    te = jnp.minimum(jnp.sum((tr[:, None] >= tile_end[None, :]).astype(I32), axis=1), n_exp - 1)
    tv = (t < used).astype(I32)
    prev_e = jnp.concatenate([jnp.full((1,), -1, I32), te[:-1]])
    tf = tv * ((te != prev_e).astype(I32))
    src = jnp.zeros((n_tiles * bm,), I32).at[pos].set(jnp.arange(n_assign, dtype=I32) // TOP_K)
    return (te.astype(I32), tf.astype(I32), tv, tr.astype(I32)), pos.astype(I32), src


def kernel(x, c, w_ada, b_ada, g_mix, g_ffn, w_in, conv_w, conv_b, conv_ln_g, conv_ln_b, w_conv_out,
           q_gain, k_gain, rel_bias, w_attn_out, w_out, ffn_w1, ffn_w3, ffn_w2, moe_router, moe_w1,
           moe_w3, moe_w2):
    b, s, d = x.shape
    m = b * s
    depth = w_ada.shape[0]
    in_w = w_in.shape[2]
    ch = conv_w.shape[2]
    hd = q_gain.shape[1]
    att_out_w = w_attn_out.shape[1]
    nh = att_out_w // hd
    ng = len(ATTN_GROUPS)
    att_w = rel_bias.shape[1] * hd
    q_off = 2 * ch
    gate_off = q_off + 3 * att_w
    assert in_w == gate_off + 2 * d and rel_bias.shape[1] == ng * nh

    bm = _fit(s, 1024)
    sched = _dense_sched(m // bm)
    bn = _fit(d, 512)
    aw = nh * hd
    dils = [dil for _, dil in ATTN_GROUPS]
    assert dils[0] == 1 and q_off % bn == 0 and aw % bn == 0 and q_off % aw == 0

    cb = lambda start, width: list(range(start // bn, (start + width) // bn))
    qkv_cols = lambda g: sum([cb(q_off + i * att_w + g * aw, aw) for i in range(3)], [])
    nat_cols = cb(0, q_off) + qkv_cols(0) + cb(gate_off, 2 * d)
    nat_gate = q_off + 3 * aw

    mod = _adaln(c, w_ada, b_ada)
    bias = _bias_tables(rel_bias, nh)

    for l in range(depth):
        sh1, sc1, gt1, sh2, sc2, gt2 = [mod[l, :, i * d:(i + 1) * d] for i in range(6)]
        gt1 = gt1.reshape(b, 1, d)
        gt2 = gt2.reshape(b, 1, d)

        h = _rmsmod(x, g_mix[l], sh1, sc1, BF16).reshape(m, d)
        w_in_l = w_in[l][None]
        p = _ws_matmul("in_proj", _plain_compute, [(h, d, 0)], [(w_in_l, d, 0)], [],
                       m, len(nat_cols) * bn, BF16, bm, bn, sched, w_cols=nat_cols)
        p3 = p.reshape(b, s, len(nat_cols) * bn)
        cu = _conv_branch(p3, conv_w[l], conv_b[l], conv_ln_g[l], conv_ln_b[l]).reshape(m, ch)
        outs, lses = [], []
        for g, dil in enumerate(dils):
            if dil == 1:
                pa, qb = p3, q_off // aw
            else:
                pa = _ws_matmul(f"in_proj_g{g}", _plain_compute, [(h, d, 0)], [(w_in_l, d, 0)], [],
                                m, 3 * aw, BF16, bm, bn, sched, w_cols=qkv_cols(g), dil=dil,
                                rows_per_batch=s).reshape(b, s, 3 * aw)
                qb = 0
            o_g, lse_g = _attn_group(pa, g, dil, qb, qb + 1, qb + 2, nh, hd, q_gain[l], k_gain[l], bias[g])
            outs.append(o_g)
            lses.append(lse_g)
        o = _group_merge(outs, lses, dils).reshape(m, att_out_w)
        merged = _ws_matmul(
            "branch_merge", _merge_compute, [(cu, ch, 0), (o, att_out_w, 0)],
            [(w_conv_out[l][None], ch, 0), (w_attn_out[l][None], att_out_w, 0)],
            [(p, _tile_extra(bm, bn, nat_gate // bn)), (p, _tile_extra(bm, bn, (nat_gate + d) // bn))],
            m, d, BF16, bm, bn, sched)
        x2 = _ws_matmul(
            "out_proj", _residual_compute, [(merged, d, 0)], [(w_out[l][None], d, 0)],
            [(x.reshape(m, d), _tile_extra(bm, bn, 0)), (gt1, _batch_row_extra(bm, bn, s))],
            m, d, F32, bm, bn, sched)
        x = x2.reshape(b, s, d)

        j = l // 2
        if l % 2 == 0:
            h = _rmsmod(x, g_ffn[l], sh2, sc2, BF16).reshape(m, d)
            d_ff = ffn_w1.shape[2]
            hid = _ws_matmul("ffn_up", _swiglu_compute, [(h, d, 0)],
                             [(ffn_w1[j][None], d, 0), (ffn_w3[j][None], d, 0)], [],
                             m, d_ff, BF16, bm, _fit(d_ff, 256), sched)
            ksplit = 2 if d_ff % (2 * V7X_LANES) == 0 else 1
            kc = d_ff // ksplit
            bm2 = _fit(s, 512)
            sched2 = _dense_sched(m // bm2)
            for kb in range(ksplit):
                x2 = _ws_matmul(
                    "ffn_down", _residual_compute, [(hid, kc, kb)], [(ffn_w2[j][None], kc, kb)],
                    [(x2, _tile_extra(bm2, bn, 0)), (gt2, _batch_row_extra(bm2, bn, s))],
                    m, d, F32, bm2, bn, sched2)
            x = x2.reshape(b, s, d)
        else:
            n_exp = moe_router.shape[2]
            d_fe = moe_w1.shape[3]
            hf, topw, topi, cnt = _rmsmod(x, g_ffn[l], sh2, sc2, F32, w_router=moe_router[j])
            bme = _fit(m, 512)
            msched, pos, src = _moe_schedule(topi.reshape(m, V7X_LANES), cnt[0, :n_exp].astype(I32), n_exp, bme)
            n_rows = src.shape[0]
            xs = _moe_gather(hf.reshape(m, d), src, _fit(bme, 128, 16))
            hid = _ws_matmul("moe_up", _swiglu_compute, [(xs, d, 0)],
                             [(moe_w1[j], d, 0), (moe_w3[j], d, 0)], [],
                             n_rows, d_fe, BF16, bme, _fit(d_fe, 256), msched)
            ys = _ws_matmul("moe_down", _plain_compute, [(hid, d_fe, 0)], [(moe_w2[j], d_fe, 0)], [],
                            n_rows, d, F32, bme, bn, msched)
            x = _moe_combine(ys, pos, x2, topw.reshape(m, V7X_LANES), gt2, s).reshape(b, s, d)
    return x
```

```python
import functools
import math

import numpy as np
import jax
import jax.numpy as jnp
from jax import lax
from jax.experimental import pallas as pl
from jax.experimental.pallas import tpu as pltpu

F32 = jnp.float32
BF16 = jnp.bfloat16
I32 = jnp.int32

ATTN_GROUPS = ((128, 1), (512, 4), (2048, 16))
ATTN_BLOCK = 128
MAX_DISTANCE = 2048
TOP_K = 2

V7X_VMEM_BYTES = 64 * 1024 * 1024
V7X_LANES = 128
VMEM_REQUEST_CAP = V7X_VMEM_BYTES - 6 * 1024 * 1024
VMEM_SLACK = 8 * 1024 * 1024
NEG_INF = float("-inf")


def _fit(n, pref, unit=V7X_LANES):
    best = None
    for d in range(unit, min(n, pref) + 1, unit):
        if n % d == 0:
            best = d
    return best if best is not None else n


def _nbytes(shape, dtype):
    return int(np.prod(shape)) * jnp.dtype(dtype).itemsize


def _cparams(semantics, block_bytes, scratch_bytes=0):
    limit = min(VMEM_REQUEST_CAP, 2 * block_bytes + scratch_bytes + VMEM_SLACK)
    return pltpu.CompilerParams(dimension_semantics=semantics, vmem_limit_bytes=int(limit))


def _ada_kernel(c_ref, w_ref, b_ref, o_ref):
    c = c_ref[...]
    ca = c * jax.nn.sigmoid(c)
    o_ref[...] = jnp.dot(ca, w_ref[...], preferred_element_type=F32,
                         precision=lax.Precision.HIGHEST) + b_ref[...]


def _adaln(c, w_ada, b_ada):
    depth, d, n = w_ada.shape
    b = c.shape[0]
    rows = 8
    c8 = jnp.pad(c, ((0, rows - b), (0, 0)))
    bn = _fit(n, 512)
    blk = _nbytes((rows, d), F32) + _nbytes((d, bn), F32) + 2 * _nbytes((rows, bn), F32)
    out = pl.pallas_call(
        _ada_kernel,
        grid=(depth, n // bn),
        in_specs=[pl.BlockSpec((rows, d), lambda l, j: (0, 0)),
                  pl.BlockSpec((None, d, bn), lambda l, j: (l, 0, j)),
                  pl.BlockSpec((None, 1, bn), lambda l, j: (l, 0, j))],
        out_specs=pl.BlockSpec((None, rows, bn), lambda l, j: (l, 0, j)),
        out_shape=jax.ShapeDtypeStruct((depth, rows, n), F32),
        compiler_params=_cparams(("arbitrary", "arbitrary"), blk, 3 * _nbytes((d, bn), F32)),
        name="adaln",
    )(c8, w_ada, b_ada.reshape(depth, 1, n))
    return out[:, :b]


def _normed(x_ref, g_ref, sh_ref, sc_ref):
    x = x_ref[...]
    ms = jnp.mean(x * x, axis=-1, keepdims=True)
    y = x * lax.rsqrt(ms + 1e-6) * g_ref[...]
    return y * (1.0 + sc_ref[...]) + sh_ref[...]


def _rmsmod_kernel(x_ref, g_ref, sh_ref, sc_ref, o_ref):
    o_ref[...] = _normed(x_ref, g_ref, sh_ref, sc_ref).astype(o_ref.dtype)


def _rmsmod_route_kernel(x_ref, g_ref, sh_ref, sc_ref, wr_ref, h_ref, tw_ref, ti_ref, cnt_ref, *, n_exp):
    h = _normed(x_ref, g_ref, sh_ref, sc_ref)
    h_ref[...] = h
    logits = jnp.dot(h, wr_ref[...], preferred_element_type=F32, precision=lax.Precision.HIGHEST)
    lane = lax.broadcasted_iota(I32, logits.shape, 1)
    lanef = lane.astype(F32)
    sentinel = float(V7X_LANES)
    lg = jnp.where(lane < n_exp, logits, NEG_INF)
    m1 = jnp.max(lg, axis=-1, keepdims=True)
    i1 = jnp.min(jnp.where(lg == m1, lanef, sentinel), axis=-1, keepdims=True)
    lg2 = jnp.where(lanef == i1, NEG_INF, lg)
    m2 = jnp.max(lg2, axis=-1, keepdims=True)
    i2 = jnp.min(jnp.where(lg2 == m2, lanef, sentinel), axis=-1, keepdims=True)
    e2 = jnp.exp(m2 - m1)
    den = 1.0 + e2
    tw_ref[...] = jnp.where(lane == 0, 1.0 / den, jnp.where(lane == 1, e2 / den, 0.0))

    @pl.when((pl.program_id(0) == 0) & (pl.program_id(1) == 0))
    def _init_counts():
        cnt_ref[...] = jnp.zeros(cnt_ref.shape, F32)

    oh1 = (lanef == i1).astype(F32)
    oh2 = (lanef == i2).astype(F32)
    oh = oh1 + oh2
    ts = oh.shape[0]
    earlier = (lax.broadcasted_iota(I32, (ts, ts), 1) < lax.broadcasted_iota(I32, (ts, ts), 0))
    before = cnt_ref[0:1, :] + jnp.dot(earlier.astype(BF16), oh.astype(BF16), preferred_element_type=F32)
    r1 = jnp.sum(oh1 * before, axis=-1, keepdims=True)
    r2 = jnp.sum(oh2 * before, axis=-1, keepdims=True)
    cnt_ref[...] = cnt_ref[...] + jnp.sum(oh, axis=0, keepdims=True)
    ti_ref[...] = jnp.where(lane == 0, i1, jnp.where(lane == 1, i2, jnp.where(
        lane == 2, r1, jnp.where(lane == 3, r2, 0.0)))).astype(I32)


def _rmsmod(x, g, shift, scale, out_dtype, w_router=None):
    b, s, d = x.shape
    ts = _fit(s, 256, 8)
    specs = [pl.BlockSpec((None, ts, d), lambda i, j: (i, j, 0)),
             pl.BlockSpec((1, d), lambda i, j: (0, 0)),
             pl.BlockSpec((None, 1, d), lambda i, j: (i, 0, 0)),
             pl.BlockSpec((None, 1, d), lambda i, j: (i, 0, 0))]
    args = [x, g.reshape(1, d), shift.reshape(b, 1, d), scale.reshape(b, 1, d)]
    tile = pl.BlockSpec((None, ts, d), lambda i, j: (i, j, 0))
    blk = _nbytes((ts, d), F32) * 2 + 3 * _nbytes((1, d), F32)
    if w_router is None:
        return pl.pallas_call(
            _rmsmod_kernel, grid=(b, s // ts), in_specs=specs, out_specs=tile,
            out_shape=jax.ShapeDtypeStruct((b, s, d), out_dtype),
            compiler_params=_cparams(("arbitrary", "arbitrary"), blk, 4 * _nbytes((ts, d), F32)),
            name="rmsmod")(*args)
    n_exp = w_router.shape[1]
    wr = jnp.pad(w_router, ((0, 0), (0, V7X_LANES - n_exp)))
    lanes = pl.BlockSpec((None, ts, V7X_LANES), lambda i, j: (i, j, 0))
    blk += _nbytes((d, V7X_LANES), F32) + 2 * _nbytes((ts, V7X_LANES), F32)
    return pl.pallas_call(
        functools.partial(_rmsmod_route_kernel, n_exp=n_exp),
        grid=(b, s // ts),
        in_specs=specs + [pl.BlockSpec((d, V7X_LANES), lambda i, j: (0, 0))],
        out_specs=[tile, lanes, lanes, pl.BlockSpec((8, V7X_LANES), lambda i, j: (0, 0))],
        out_shape=[jax.ShapeDtypeStruct((b, s, d), F32),
                   jax.ShapeDtypeStruct((b, s, V7X_LANES), F32),
                   jax.ShapeDtypeStruct((b, s, V7X_LANES), I32),
                   jax.ShapeDtypeStruct((8, V7X_LANES), F32)],
        compiler_params=_cparams(("arbitrary", "arbitrary"), blk, 8 * _nbytes((ts, d), F32)),
        name="rmsmod_route")(*args, wr)


def _ws_kernel(te_ref, tf_ref, tv_ref, tr_ref, tn_ref, wc_ref, *refs, nx, nw, ne, compute, dil, w_blocks, bn):
    del tr_ref
    x_refs = refs[:nx]
    w_refs = refs[nx:nx + nw]
    e_refs = refs[nx + nw:nx + nw + ne]
    o_ref = refs[nx + nw + ne]
    rest = refs[nx + nw + ne + 1:]
    wb_refs, stage_refs, sem = rest[:nw], rest[nw:2 * nw], rest[2 * nw]
    j = pl.program_id(0)
    t = pl.program_id(1)

    def weight_copy(i, expert, col_tile):
        k, kb = w_blocks[i]
        col = pl.multiple_of(wc_ref[col_tile] * bn, bn)
        return pltpu.make_async_copy(w_refs[i].at[expert, pl.ds(kb * k, k), pl.ds(col, bn)],
                                     stage_refs[i], sem.at[i])

    @pl.when(tf_ref[t] == 1)
    def _new_weights():
        @pl.when((j == 0) & (t == 0))
        def _first_block():
            for i in range(nw):
                weight_copy(i, te_ref[0], 0).start()

        for i in range(nw):
            weight_copy(i, te_ref[t], j).wait()
            wb_refs[i][...] = stage_refs[i][...].astype(BF16)

        nxt = tn_ref[t]

        @pl.when(nxt >= 0)
        def _next_run():
            for i in range(nw):
                weight_copy(i, te_ref[jnp.maximum(nxt, 0)], j).start()

        @pl.when((nxt < 0) & (j + 1 < pl.num_programs(0)))
        def _next_column_tile():
            for i in range(nw):
                weight_copy(i, te_ref[0], j + 1).start()

    @pl.when(tv_ref[t] == 1)
    def _compute():
        res = compute(x_refs, wb_refs, e_refs)
        if dil == 1:
            o_ref[...] = res.astype(o_ref.dtype)
        else:
            perm_ref = rest[-1]
            n_chunks, bm, lanes = perm_ref.shape
            rows = bm // dil
            for c in range(n_chunks):
                perm_ref[c] = res[:, c * lanes:(c + 1) * lanes]
            for r in range(dil):
                for c in range(n_chunks):
                    o_ref[r, :, c * lanes:(c + 1) * lanes] = \
                        perm_ref[c, pl.ds(r, rows, stride=dil), :].astype(o_ref.dtype)

    @pl.when(tv_ref[t] == 0)
    def _unused_tile():
        o_ref[...] = jnp.zeros(o_ref.shape, o_ref.dtype)


def _dense_sched(n_tiles, expert=0):
    first = np.zeros((n_tiles,), np.int32)
    first[0] = 1
    return (jnp.full((n_tiles,), expert, I32), jnp.asarray(first), jnp.ones((n_tiles,), I32),
            jnp.arange(n_tiles, dtype=I32), jnp.full((n_tiles,), -1, I32))


def _tile_extra(bm, bn, col_off):
    return pl.BlockSpec((bm, bn), lambda j, t, te, tf, tv, tr, *_: (tr[t], col_off + j))


def _batch_row_extra(bm, bn, rows_per_batch):
    return pl.BlockSpec((None, 1, bn), lambda j, t, te, tf, tv, tr, *_: ((tr[t] * bm) // rows_per_batch, 0, j))


def _ws_matmul(name, compute, xs, ws, extras, out_rows, out_cols, out_dtype, bm, bn, sched,
               w_cols=None, dil=1, rows_per_batch=None):
    n_tiles = sched[0].shape[0]
    nj = out_cols // bn
    if w_cols is None:
        w_cols = np.arange(nj)
    assert len(w_cols) == nj
    w_cols = jnp.asarray(np.asarray(w_cols, np.int32))
    in_specs, args, blk = [], [], 0
    for arr, k, kb in xs:
        in_specs.append(pl.BlockSpec((bm, k), lambda j, t, te, tf, tv, tr, *_, kb=kb: (tr[t], kb)))
        args.append(arr)
        blk += _nbytes((bm, k), arr.dtype)
    wb, stage, scratch_bytes = [], [], 0
    for arr, k, kb in ws:
        in_specs.append(pl.BlockSpec(memory_space=pl.ANY))
        args.append(arr)
        wb.append(pltpu.VMEM((k, bn), BF16))
        stage.append(pltpu.VMEM((k, bn), arr.dtype))
        scratch_bytes += _nbytes((k, bn), BF16) + _nbytes((k, bn), arr.dtype)
    scratch = wb + stage + [pltpu.SemaphoreType.DMA((len(ws),))]
    for arr, spec in extras:
        in_specs.append(spec)
        args.append(arr)
        blk += _nbytes([d for d in spec.block_shape if d is not None], arr.dtype)
    blk += _nbytes((bm, bn), out_dtype)
    scratch_bytes += 2 * _nbytes((bm, bn), F32)
    if dil == 1:
        out_spec = pl.BlockSpec((bm, bn), lambda j, t, *_: (t, j))
        out_shape = jax.ShapeDtypeStruct((out_rows, out_cols), out_dtype)
    else:
        tpb = rows_per_batch // bm
        assert rows_per_batch % bm == 0 and bm % dil == 0
        out_spec = pl.BlockSpec((None, dil, bm // dil, bn), lambda j, t, *_: (t // tpb, 0, t % tpb, j))
        out_shape = jax.ShapeDtypeStruct((out_rows // rows_per_batch, dil, rows_per_batch // dil, out_cols),
                                         out_dtype)
        assert bn % V7X_LANES == 0
        scratch.append(pltpu.VMEM((bn // V7X_LANES, bm, V7X_LANES), F32))
        scratch_bytes += _nbytes((bm, bn), F32)
    kern = functools.partial(_ws_kernel, nx=len(xs), nw=len(ws), ne=len(extras), compute=compute, dil=dil,
                             w_blocks=tuple((k, kb) for _, k, kb in ws), bn=bn)
    return pl.pallas_call(
        kern,
        grid_spec=pltpu.PrefetchScalarGridSpec(
            num_scalar_prefetch=6, grid=(nj, n_tiles), in_specs=in_specs,
            out_specs=out_spec, scratch_shapes=scratch),
        out_shape=out_shape,
        compiler_params=_cparams(("arbitrary", "arbitrary"), blk, scratch_bytes),
        name=name,
    )(*sched, w_cols, *args)


def _mm(x, wb_ref):
    return jnp.dot(x, wb_ref[...], preferred_element_type=F32)


def _plain_compute(xr, wr, er):
    return _mm(xr[0][...], wr[0])


def _swiglu_compute(xr, wr, er):
    x = xr[0][...]
    a = _mm(x, wr[0])
    return a * jax.nn.sigmoid(a) * _mm(x, wr[1])


def _merge_compute(xr, wr, er):
    conv_out = _mm(xr[0][...], wr[0])
    attn_out = _mm(xr[1][...], wr[1])
    gc = er[0][...].astype(F32)
    ga = er[1][...].astype(F32)
    return jax.nn.sigmoid(gc) * conv_out + jax.nn.sigmoid(ga) * attn_out


def _residual_compute(xr, wr, er):
    return er[0][...] + er[1][...] * _mm(xr[0][...], wr[0])


CONV_HALO = 32
CONV_ROW_CHUNK = 32
CONV_CH_CHUNK = 512


def _conv_kernel(a_ref, b_ref, ah_ref, bh_ref, w_ref, cb_ref, g_ref, beta_ref, o_ref, ubuf, cbuf,
                 *, ts, width, ch):
    i = pl.program_id(1)
    a = a_ref[...].astype(F32)
    b = b_ref[...].astype(F32)
    ubuf[CONV_HALO:CONV_HALO + ts, :] = a * jax.nn.sigmoid(b)
    ah = ah_ref[...].astype(F32)
    bh = bh_ref[...].astype(F32)
    ubuf[0:CONV_HALO, :] = jnp.where(i > 0, ah * jax.nn.sigmoid(bh), 0.0)
    off = CONV_HALO - (width - 1)
    cch = min(CONV_CH_CHUNK, ch)
    rch = min(CONV_ROW_CHUNK, ts)
    for c0 in range(0, ch, cch):
        taps = [w_ref[j:j + 1, c0:c0 + cch] for j in range(width)]
        bias = cb_ref[:, c0:c0 + cch]
        for r0 in range(0, ts, rch):
            acc = jnp.zeros((rch, cch), F32)
            for j in range(width):
                acc = acc + taps[j] * ubuf[off + r0 + j:off + r0 + j + rch, c0:c0 + cch]
            cbuf[r0:r0 + rch, c0:c0 + cch] = acc + bias
    y = cbuf[...]
    mu = jnp.mean(y, axis=-1, keepdims=True)
    yc = y - mu
    var = jnp.mean(yc * yc, axis=-1, keepdims=True)
    yn = yc * lax.rsqrt(var + 1e-5) * g_ref[...] + beta_ref[...]
    o_ref[...] = (yn * jax.nn.sigmoid(yn)).astype(o_ref.dtype)


def _conv_branch(p3, conv_w, conv_b, ln_g, ln_b):
    b, s, _ = p3.shape
    width, ch = conv_w.shape
    assert width - 1 <= CONV_HALO
    ts = _fit(s, 128, CONV_HALO)
    hb = ts // CONV_HALO
    tile = lambda col: pl.BlockSpec((None, ts, ch), lambda bi, i, col=col: (bi, i, col))
    halo = lambda col: pl.BlockSpec((None, CONV_HALO, ch),
                                    lambda bi, i, col=col: (bi, jnp.maximum(i * hb - 1, 0), col))
    row = pl.BlockSpec((1, ch), lambda bi, i: (0, 0))
    blk = 2 * _nbytes((ts + CONV_HALO, ch), BF16) + _nbytes((width + 3, ch), F32) + _nbytes((ts, ch), BF16)
    scratch_bytes = _nbytes((2 * ts + CONV_HALO, ch), F32)
    return pl.pallas_call(
        functools.partial(_conv_kernel, ts=ts, width=width, ch=ch),
        grid=(b, s // ts),
        in_specs=[tile(0), tile(1), halo(0), halo(1),
                  pl.BlockSpec((width, ch), lambda bi, i: (0, 0)), row, row, row],
        out_specs=pl.BlockSpec((None, ts, ch), lambda bi, i: (bi, i, 0)),
        out_shape=jax.ShapeDtypeStruct((b, s, ch), BF16),
        scratch_shapes=[pltpu.VMEM((ts + CONV_HALO, ch), F32), pltpu.VMEM((ts, ch), F32)],
        compiler_params=_cparams(("arbitrary", "arbitrary"), blk, scratch_bytes + 4 * _nbytes((ts, ch), F32)),
        name="conv_module",
    )(p3, p3, p3, p3, conv_w, conv_b.reshape(1, ch), ln_g.reshape(1, ch), ln_b.reshape(1, ch))


def _bias_kernel(rel_ref, idx_ref, o_ref, *, n_buckets, nh, span, blk):
    g = pl.program_id(0)
    h = pl.program_id(1)
    idx = idx_ref[...]
    acc = jnp.zeros(idx.shape, F32)
    for bkt in range(n_buckets):
        acc = jnp.where(idx == bkt, rel_ref[bkt, g * nh + h], acc)
    qi = lax.broadcasted_iota(I32, idx.shape, 0) + blk
    kj = lax.broadcasted_iota(I32, idx.shape, 1)
    delta = qi - kj
    o_ref[...] = jnp.where((delta >= 0) & (delta <= span), acc, NEG_INF)


def _t5_bucket(dist, n_buckets):
    max_exact = n_buckets // 2
    n = jnp.maximum(dist, 1).astype(F32)
    large = max_exact + (jnp.log(n / max_exact) / math.log(MAX_DISTANCE / max_exact)
                         * (n_buckets - max_exact)).astype(I32)
    large = jnp.minimum(large, n_buckets - 1)
    return jnp.where(dist < max_exact, dist, large)


def _bias_tables(rel_bias, nh):
    n_buckets = rel_bias.shape[0]
    blk = ATTN_BLOCK
    span = ATTN_GROUPS[0][0] // ATTN_GROUPS[0][1]
    assert all(w // d == span for w, d in ATTN_GROUPS)
    delta = (jnp.arange(blk, dtype=I32)[:, None] + blk) - jnp.arange(2 * blk, dtype=I32)[None, :]
    idx = jnp.stack([_t5_bucket(jnp.maximum(delta, 0) * d, n_buckets) for _, d in ATTN_GROUPS])
    ng = len(ATTN_GROUPS)
    return pl.pallas_call(
        functools.partial(_bias_kernel, n_buckets=n_buckets, nh=nh, span=span, blk=blk),
        grid=(ng, nh),
        in_specs=[pl.BlockSpec(memory_space=pltpu.SMEM),
                  pl.BlockSpec((None, blk, 2 * blk), lambda g, h: (g, 0, 0))],
        out_specs=pl.BlockSpec((None, None, blk, 2 * blk), lambda g, h: (g, h, 0, 0)),
        out_shape=jax.ShapeDtypeStruct((ng, nh, blk, 2 * blk), F32),
        name="rel_bias_tables",
    )(rel_bias, idx)


def _attn_kernel(q_ref, kp_ref, ko_ref, vp_ref, vo_ref, qg_ref, kg_ref, bias_ref, o_ref, lse_ref,
                 *, nh, hd, blk):
    has_prev = pl.program_id(2) > 0
    qg = qg_ref[...]
    kg = kg_ref[...]
    scale = hd ** -0.5
    nt = (((1,), (1,)), ((), ()))

    def qk_norm(ref, sl, gain):
        v = ref[:, sl].astype(F32)
        return (v * lax.rsqrt(jnp.mean(v * v, axis=-1, keepdims=True) + 1e-6) * gain).astype(BF16)

    for h in range(nh):
        sl = slice(h * hd, (h + 1) * hd)
        q = qk_norm(q_ref, sl, qg)
        kp = qk_norm(kp_ref, sl, kg)
        ko = qk_norm(ko_ref, sl, kg)
        bias = bias_ref[h]
        s_p = lax.dot_general(q, kp, nt, preferred_element_type=F32) * scale \
            + jnp.where(has_prev, bias[:, :blk], NEG_INF)
        s_o = lax.dot_general(q, ko, nt, preferred_element_type=F32) * scale + bias[:, blk:]
        m = jnp.maximum(jnp.max(s_p, axis=-1, keepdims=True), jnp.max(s_o, axis=-1, keepdims=True))
        p_p = jnp.exp(s_p - m)
        p_o = jnp.exp(s_o - m)
        l = jnp.sum(p_p, axis=-1, keepdims=True) + jnp.sum(p_o, axis=-1, keepdims=True)
        acc = jnp.dot(p_p.astype(BF16), vp_ref[:, sl], preferred_element_type=F32) \
            + jnp.dot(p_o.astype(BF16), vo_ref[:, sl], preferred_element_type=F32)
        o_ref[:, sl] = (acc / l).astype(o_ref.dtype)
        lse_ref[:, sl] = jnp.broadcast_to(m + jnp.log(l), (blk, hd))


def _attn_group(pa, g, dil, qb, kb, vb, nh, hd, q_gain, k_gain, bias_g):
    b, s, _ = pa.shape
    blk = ATTN_BLOCK
    seq = s // dil
    assert s % dil == 0 and seq % blk == 0
    nb = seq // blk
    aw = nh * hd
    own = lambda c: pl.BlockSpec((None, blk, aw), lambda bi, r, n, c=c: (bi, r * nb + n, c))
    prev = lambda c: pl.BlockSpec((None, blk, aw),
                                  lambda bi, r, n, c=c: (bi, r * nb + jnp.maximum(n - 1, 0), c))
    gain = pl.BlockSpec((1, hd), lambda bi, r, n: (0, 0))
    out_spec = pl.BlockSpec((None, blk, aw), lambda bi, r, n: (bi, r * nb + n, 0))
    blk_bytes = 5 * _nbytes((blk, aw), BF16) + _nbytes((nh, blk, 2 * blk), F32) \
        + _nbytes((blk, aw), BF16) + _nbytes((blk, aw), F32)
    o, lse = pl.pallas_call(
        functools.partial(_attn_kernel, nh=nh, hd=hd, blk=blk),
        grid=(b, dil, nb),
        in_specs=[own(qb), prev(kb), own(kb), prev(vb), own(vb), gain, gain,
                  pl.BlockSpec((nh, blk, 2 * blk), lambda bi, r, n: (0, 0, 0))],
        out_specs=[out_spec, out_spec],
        out_shape=[jax.ShapeDtypeStruct((b, s, aw), BF16), jax.ShapeDtypeStruct((b, s, aw), F32)],
        compiler_params=_cparams(("arbitrary", "arbitrary", "arbitrary"), blk_bytes),
        name=f"dilated_attn_g{g}",
    )(pa, pa, pa, pa, pa, q_gain.reshape(1, hd), k_gain.reshape(1, hd), bias_g)
    return o.reshape(b, dil, seq, aw), lse.reshape(b, dil, seq, aw)


def _group_merge_kernel(*refs, dils, ts):
    ng = len(dils)
    o_refs, l_refs, out_ref = refs[:ng], refs[ng:2 * ng], refs[2 * ng]
    scratch = list(refs[2 * ng + 1:])

    def token_order(ref, dil):
        if dil == 1:
            return ref[0].astype(F32)
        buf = scratch.pop(0)
        n_chunks, _, lanes = buf.shape
        rows = ts // dil
        for r in range(dil):
            for c in range(n_chunks):
                buf[c, pl.ds(r, rows, stride=dil), :] = ref[r, :, c * lanes:(c + 1) * lanes].astype(F32)
        return jnp.concatenate([buf[c] for c in range(n_chunks)], axis=1)

    ls = [token_order(r, dil) for r, dil in zip(l_refs, dils)]
    m = functools.reduce(jnp.maximum, ls)
    es = [jnp.exp(l - m) for l in ls]
    den = functools.reduce(lambda a, c: a + c, es)
    num = None
    for e, r, dil in zip(es, o_refs, dils):
        term = e * token_order(r, dil)
        num = term if num is None else num + term
    out_ref[...] = (num / den).astype(out_ref.dtype)


def _group_merge(outs, lses, dils):
    b, _, _, aw = outs[0].shape
    s = outs[0].shape[1] * outs[0].shape[2]
    ts = _fit(s, 256, 16 * max(dils))
    spec = lambda dil: pl.BlockSpec((None, dil, ts // dil, aw), lambda i, j: (i, 0, j, 0))
    specs = [spec(dil) for dil in dils]
    n_perm = 2 * sum(1 for dil in dils if dil > 1)
    blk = len(dils) * (_nbytes((ts, aw), BF16) + _nbytes((ts, aw), F32)) + _nbytes((ts, aw), BF16)
    return pl.pallas_call(
        functools.partial(_group_merge_kernel, dils=tuple(dils), ts=ts),
        grid=(b, s // ts), in_specs=specs + specs,
        out_specs=pl.BlockSpec((None, ts, aw), lambda i, j: (i, j, 0)),
        out_shape=jax.ShapeDtypeStruct((b, s, aw), BF16),
        scratch_shapes=[pltpu.VMEM((aw // V7X_LANES, ts, V7X_LANES), F32)] * n_perm,
        compiler_params=_cparams(("arbitrary", "arbitrary"), blk, (n_perm + 4) * _nbytes((ts, aw), F32)),
        name="attn_group_merge")(*outs, *lses)


def _row_copy(src_hbm, row, dst, slot, sem):
    return pltpu.make_async_copy(src_hbm.at[pl.ds(row, 1), :], dst.at[pl.ds(slot, 1), :], sem)


ROW_DMA_UNROLL = 8
DMA_PRIORITIES = 2


def _gather_kernel(src_ref, h_hbm, o_ref, buf, sem, *, rows):
    t = pl.program_id(0)
    slot = t % 2

    def fetch(step, dst_slot):
        def start(i, carry):
            for p in range(DMA_PRIORITIES):
                row = i * DMA_PRIORITIES + p
                _row_copy(h_hbm, src_ref[step * rows + row], buf.at[dst_slot], row,
                          sem.at[dst_slot]).start(priority=p)
            return carry
        lax.fori_loop(0, rows // DMA_PRIORITIES, start, 0, unroll=ROW_DMA_UNROLL // DMA_PRIORITIES)

    @pl.when(t == 0)
    def _first():
        fetch(0, 0)

    @pl.when(t + 1 < pl.num_programs(0))
    def _next():
        fetch(t + 1, 1 - slot)

    def wait(i, carry):
        _row_copy(h_hbm, 0, buf.at[slot], i, sem.at[slot]).wait()
        return carry

    lax.fori_loop(0, rows, wait, 0, unroll=ROW_DMA_UNROLL)
    o_ref[...] = buf[slot].astype(o_ref.dtype)


def _moe_gather(h2, src, rows):
    r_total = src.shape[0]
    d = h2.shape[1]
    return pl.pallas_call(
        functools.partial(_gather_kernel, rows=rows),
        grid_spec=pltpu.PrefetchScalarGridSpec(
            num_scalar_prefetch=1, grid=(r_total // rows,),
            in_specs=[pl.BlockSpec(memory_space=pl.ANY)],
            out_specs=pl.BlockSpec((rows, d), lambda t, src: (t, 0)),
            scratch_shapes=[pltpu.VMEM((2, rows, d), F32), pltpu.SemaphoreType.DMA((2,))]),
        out_shape=jax.ShapeDtypeStruct((r_total, d), BF16),
        compiler_params=_cparams(("arbitrary",), _nbytes((rows, d), BF16), 3 * _nbytes((rows, d), F32)),
        name="moe_gather",
    )(src, h2)


def _combine_kernel(pos_ref, y_hbm, x_ref, tw_ref, gt_ref, o_ref, buf, sem, *, rows):
    t = pl.program_id(0)
    slot = t % 2

    def fetch(step, dst_slot):
        def start(i, carry):
            for k in range(TOP_K):
                _row_copy(y_hbm, pos_ref[(step * rows + i) * TOP_K + k], buf.at[dst_slot, k], i,
                          sem.at[dst_slot]).start(priority=k % DMA_PRIORITIES)
            return carry
        lax.fori_loop(0, rows, start, 0, unroll=ROW_DMA_UNROLL // TOP_K)

    @pl.when(t == 0)
    def _first():
        fetch(0, 0)

    @pl.when(t + 1 < pl.num_programs(0))
    def _next():
        fetch(t + 1, 1 - slot)

    def wait(i, carry):
        for k in range(TOP_K):
            _row_copy(y_hbm, 0, buf.at[slot, k], i, sem.at[slot]).wait()
        return carry

    lax.fori_loop(0, rows, wait, 0, unroll=ROW_DMA_UNROLL // TOP_K)
    tw = tw_ref[...]
    y = tw[:, 0:1] * buf[slot, 0]
    for k in range(1, TOP_K):
        y = y + tw[:, k:k + 1] * buf[slot, k]
    o_ref[...] = x_ref[...] + gt_ref[...] * y


def _moe_combine(y_sorted, pos, x2, topw2, gate, rows_per_batch):
    m, d = x2.shape
    rows = _fit(rows_per_batch, 128, 8)
    blk = 2 * _nbytes((rows, d), F32) + _nbytes((rows, V7X_LANES), F32) + _nbytes((1, d), F32)
    return pl.pallas_call(
        functools.partial(_combine_kernel, rows=rows),
        grid_spec=pltpu.PrefetchScalarGridSpec(
            num_scalar_prefetch=1, grid=(m // rows,),
            in_specs=[pl.BlockSpec(memory_space=pl.ANY),
                      pl.BlockSpec((rows, d), lambda t, pos: (t, 0)),
                      pl.BlockSpec((rows, V7X_LANES), lambda t, pos: (t, 0)),
                      pl.BlockSpec((None, 1, d), lambda t, pos: ((t * rows) // rows_per_batch, 0, 0))],
            out_specs=pl.BlockSpec((rows, d), lambda t, pos: (t, 0)),
            scratch_shapes=[pltpu.VMEM((2, TOP_K, rows, d), F32), pltpu.SemaphoreType.DMA((2,))]),
        out_shape=jax.ShapeDtypeStruct((m, d), F32),
        compiler_params=_cparams(("arbitrary",), blk, (2 * TOP_K + 2) * _nbytes((rows, d), F32)),
        name="moe_combine",
    )(pos, y_sorted, x2, topw2, gate)


def _moe_schedule(route_i, counts, n_exp, bm, expert_base):
    m = route_i.shape[0]
    n_assign = m * TOP_K
    n_tiles = n_assign // bm + n_exp
    e = route_i[:, :TOP_K].reshape(n_assign)
    rank = route_i[:, TOP_K:2 * TOP_K].reshape(n_assign)
    onehot = (e[:, None] == jnp.arange(n_exp, dtype=I32)[None, :]).astype(I32)
    tiles_per = (counts + bm - 1) // bm
    tile_end = jnp.cumsum(tiles_per)
    row_start = (tile_end - tiles_per) * bm
    pos = jnp.sum(onehot * row_start[None, :], axis=1) + rank
    used = tile_end[-1]
    t = jnp.arange(n_tiles, dtype=I32)
    tr = jnp.minimum(t, used - 1)
    te = jnp.minimum(jnp.sum((tr[:, None] >= tile_end[None, :]).astype(I32), axis=1), n_exp - 1)
    tv = (t < used).astype(I32)
    prev_e = jnp.concatenate([jnp.full((1,), -1, I32), te[:-1]])
    tf = tv * ((te != prev_e).astype(I32))
    later_start = (t[None, :] > t[:, None]) & (tf[None, :] == 1)
    tn = jnp.min(jnp.where(later_start, t[None, :], n_tiles), axis=1)
    tn = jnp.where(tn == n_tiles, -1, tn)
    src = jnp.zeros((n_tiles * bm,), I32).at[pos].set(jnp.arange(n_assign, dtype=I32) // TOP_K)
    sched = (te.astype(I32) + expert_base, tf.astype(I32), tv, tr.astype(I32), tn.astype(I32))
    return sched, pos.astype(I32), src


def kernel(x, c, w_ada, b_ada, g_mix, g_ffn, w_in, conv_w, conv_b, conv_ln_g, conv_ln_b, w_conv_out,
           q_gain, k_gain, rel_bias, w_attn_out, w_out, ffn_w1, ffn_w3, ffn_w2, moe_router, moe_w1,
           moe_w3, moe_w2):
    b, s, d = x.shape
    m = b * s
    depth = w_ada.shape[0]
    in_w = w_in.shape[2]
    ch = conv_w.shape[2]
    hd = q_gain.shape[1]
    att_out_w = w_attn_out.shape[1]
    nh = att_out_w // hd
    ng = len(ATTN_GROUPS)
    att_w = rel_bias.shape[1] * hd
    q_off = 2 * ch
    gate_off = q_off + 3 * att_w
    assert in_w == gate_off + 2 * d and rel_bias.shape[1] == ng * nh

    bm = _fit(s, 1024)
    stack = lambda w: w.reshape((w.shape[0] * w.shape[1],) + w.shape[2:])
    bn = _fit(d, 512)
    aw = nh * hd
    dils = [dil for _, dil in ATTN_GROUPS]
    assert dils[0] == 1 and q_off % bn == 0 and aw % bn == 0 and q_off % aw == 0

    cb = lambda start, width: list(range(start // bn, (start + width) // bn))
    qkv_cols = lambda g: sum([cb(q_off + i * att_w + g * aw, aw) for i in range(3)], [])
    nat_cols = cb(0, q_off) + qkv_cols(0) + cb(gate_off, 2 * d)
    nat_gate = q_off + 3 * aw

    mod = _adaln(c, w_ada, b_ada)
    bias = _bias_tables(rel_bias, nh)

    for l in range(depth):
        sh1, sc1, gt1, sh2, sc2, gt2 = [mod[l, :, i * d:(i + 1) * d] for i in range(6)]
        gt1 = gt1.reshape(b, 1, d)
        gt2 = gt2.reshape(b, 1, d)

        h = _rmsmod(x, g_mix[l], sh1, sc1, BF16).reshape(m, d)
        sched = _dense_sched(m // bm, l)
        p = _ws_matmul("in_proj", _plain_compute, [(h, d, 0)], [(w_in, d, 0)], [],
                       m, len(nat_cols) * bn, BF16, bm, bn, sched, w_cols=nat_cols)
        p3 = p.reshape(b, s, len(nat_cols) * bn)
        cu = _conv_branch(p3, conv_w[l], conv_b[l], conv_ln_g[l], conv_ln_b[l]).reshape(m, ch)
        outs, lses = [], []
        for g, dil in enumerate(dils):
            if dil == 1:
                pa, qb = p3, q_off // aw
            else:
                pa = _ws_matmul(f"in_proj_g{g}", _plain_compute, [(h, d, 0)], [(w_in, d, 0)], [],
                                m, 3 * aw, BF16, bm, bn, sched, w_cols=qkv_cols(g), dil=dil,
                                rows_per_batch=s).reshape(b, s, 3 * aw)
                qb = 0
            o_g, lse_g = _attn_group(pa, g, dil, qb, qb + 1, qb + 2, nh, hd, q_gain[l], k_gain[l], bias[g])
            outs.append(o_g)
            lses.append(lse_g)
        o = _group_merge(outs, lses, dils).reshape(m, att_out_w)
        merged = _ws_matmul(
            "branch_merge", _merge_compute, [(cu, ch, 0), (o, att_out_w, 0)],
            [(w_conv_out, ch, 0), (w_attn_out, att_out_w, 0)],
            [(p, _tile_extra(bm, bn, nat_gate // bn)), (p, _tile_extra(bm, bn, (nat_gate + d) // bn))],
            m, d, BF16, bm, bn, sched)
        x2 = _ws_matmul(
            "out_proj", _residual_compute, [(merged, d, 0)], [(w_out, d, 0)],
            [(x.reshape(m, d), _tile_extra(bm, bn, 0)), (gt1, _batch_row_extra(bm, bn, s))],
            m, d, F32, bm, bn, sched)
        x = x2.reshape(b, s, d)

        j = l // 2
        if l % 2 == 0:
            h = _rmsmod(x, g_ffn[l], sh2, sc2, BF16).reshape(m, d)
            d_ff = ffn_w1.shape[2]
            hid = _ws_matmul("ffn_up", _swiglu_compute, [(h, d, 0)],
                             [(ffn_w1, d, 0), (ffn_w3, d, 0)], [],
                             m, d_ff, BF16, bm, _fit(d_ff, 256), _dense_sched(m // bm, j))
            ksplit = 2 if d_ff % (2 * V7X_LANES) == 0 else 1
            kc = d_ff // ksplit
            bm2 = _fit(s, 512)
            sched2 = _dense_sched(m // bm2, j)
            for kb in range(ksplit):
                x2 = _ws_matmul(
                    "ffn_down", _residual_compute, [(hid, kc, kb)], [(ffn_w2, kc, kb)],
                    [(x2, _tile_extra(bm2, bn, 0)), (gt2, _batch_row_extra(bm2, bn, s))],
                    m, d, F32, bm2, bn, sched2)
            x = x2.reshape(b, s, d)
        else:
            n_exp = moe_router.shape[2]
            d_fe = moe_w1.shape[3]
            hf, topw, topi, cnt = _rmsmod(x, g_ffn[l], sh2, sc2, F32, w_router=moe_router[j])
            bme = _fit(m, 512)
            msched, pos, src = _moe_schedule(topi.reshape(m, V7X_LANES), cnt[0, :n_exp].astype(I32), n_exp, bme,
                                             j * n_exp)
            n_rows = src.shape[0]
            xs = _moe_gather(hf.reshape(m, d), src, _fit(bme, 128, 16))
            hid = _ws_matmul("moe_up", _swiglu_compute, [(xs, d, 0)],
                             [(stack(moe_w1), d, 0), (stack(moe_w3), d, 0)], [],
                             n_rows, d_fe, BF16, bme, _fit(d_fe, 256), msched)
            ys = _ws_matmul("moe_down", _plain_compute, [(hid, d_fe, 0)], [(stack(moe_w2), d_fe, 0)], [],
                            n_rows, d, F32, bme, bn, msched)
            x = _moe_combine(ys, pos, x2, topw.reshape(m, V7X_LANES), gt2, s).reshape(b, s, d)
    return x
```

```python
import functools
import math

import numpy as np
import jax
import jax.numpy as jnp
from jax import lax
from jax.experimental import pallas as pl
from jax.experimental.pallas import tpu as pltpu

F32 = jnp.float32
BF16 = jnp.bfloat16
I32 = jnp.int32

ATTN_GROUPS = ((128, 1), (512, 4), (2048, 16))
ATTN_BLOCK = 128
MAX_DISTANCE = 2048
TOP_K = 2

V7X_VMEM_BYTES = 64 * 1024 * 1024
V7X_LANES = 128
V7X_SUBLANES = 8
VMEM_REQUEST_CAP = V7X_VMEM_BYTES - 6 * 1024 * 1024
VMEM_SLACK = 8 * 1024 * 1024
NEG_INF = float("-inf")


def _fit(n, pref, unit=V7X_LANES):
    best = None
    for d in range(unit, min(n, pref) + 1, unit):
        if n % d == 0:
            best = d
    return best if best is not None else n


def _nbytes(shape, dtype):
    return int(np.prod(shape)) * jnp.dtype(dtype).itemsize


def _cparams(semantics, block_bytes, scratch_bytes=0):
    limit = min(VMEM_REQUEST_CAP, 2 * block_bytes + scratch_bytes + VMEM_SLACK)
    return pltpu.CompilerParams(dimension_semantics=semantics, vmem_limit_bytes=int(limit))


def _ada_kernel(c_ref, w_ref, b_ref, o_ref):
    c = c_ref[...]
    ca = (c * jax.nn.sigmoid(c)).astype(BF16)
    o_ref[...] = jnp.dot(ca, w_ref[...].astype(BF16), preferred_element_type=F32) + b_ref[...]


def _adaln(c, w_ada, b_ada):
    depth, d, n = w_ada.shape
    b = c.shape[0]
    rows = 8
    c8 = jnp.pad(c, ((0, rows - b), (0, 0)))
    bn = _fit(n, 1024)
    blk = _nbytes((rows, d), F32) + _nbytes((d, bn), F32) + 2 * _nbytes((rows, bn), F32)
    out = pl.pallas_call(
        _ada_kernel,
        grid=(depth, n // bn),
        in_specs=[pl.BlockSpec((rows, d), lambda l, j: (0, 0)),
                  pl.BlockSpec((None, d, bn), lambda l, j: (l, 0, j)),
                  pl.BlockSpec((None, 1, bn), lambda l, j: (l, 0, j))],
        out_specs=pl.BlockSpec((None, rows, bn), lambda l, j: (l, 0, j)),
        out_shape=jax.ShapeDtypeStruct((depth, rows, n), F32),
        compiler_params=_cparams(("arbitrary", "arbitrary"), blk, _nbytes((d, bn), BF16)),
        name="adaln",
    )(c8, w_ada, b_ada.reshape(depth, 1, n))
    return out[:, :b]


def _normed(x_ref, g_ref, sh_ref, sc_ref):
    x = x_ref[...]
    ms = jnp.mean(x * x, axis=-1, keepdims=True)
    y = x * lax.rsqrt(ms + 1e-6) * g_ref[...]
    return y * (1.0 + sc_ref[...]) + sh_ref[...]


def _rmsmod_kernel(x_ref, g_ref, sh_ref, sc_ref, o_ref):
    o_ref[...] = _normed(x_ref, g_ref, sh_ref, sc_ref).astype(o_ref.dtype)


def _rmsmod_route_kernel(x_ref, g_ref, sh_ref, sc_ref, wr_ref, h_ref, tw_ref, ti_ref, cnt_ref, *, n_exp):
    h = _normed(x_ref, g_ref, sh_ref, sc_ref)
    h_ref[...] = h
    logits = jnp.dot(h, wr_ref[...], preferred_element_type=F32, precision=lax.Precision.HIGHEST)
    lane = lax.broadcasted_iota(I32, logits.shape, 1)
    lanef = lane.astype(F32)
    sentinel = float(V7X_LANES)
    lg = jnp.where(lane < n_exp, logits, NEG_INF)
    m1 = jnp.max(lg, axis=-1, keepdims=True)
    i1 = jnp.min(jnp.where(lg == m1, lanef, sentinel), axis=-1, keepdims=True)
    lg2 = jnp.where(lanef == i1, NEG_INF, lg)
    m2 = jnp.max(lg2, axis=-1, keepdims=True)
    i2 = jnp.min(jnp.where(lg2 == m2, lanef, sentinel), axis=-1, keepdims=True)
    e2 = jnp.exp(m2 - m1)
    den = 1.0 + e2
    tw_ref[...] = jnp.where(lane == 0, 1.0 / den, jnp.where(lane == 1, e2 / den, 0.0))

    @pl.when((pl.program_id(0) == 0) & (pl.program_id(1) == 0))
    def _init_counts():
        cnt_ref[...] = jnp.zeros(cnt_ref.shape, F32)

    oh1 = (lanef == i1).astype(F32)
    oh2 = (lanef == i2).astype(F32)
    oh = oh1 + oh2
    ts = oh.shape[0]
    earlier = (lax.broadcasted_iota(I32, (ts, ts), 1) < lax.broadcasted_iota(I32, (ts, ts), 0))
    before = cnt_ref[0:1, :] + jnp.dot(earlier.astype(BF16), oh.astype(BF16), preferred_element_type=F32)
    r1 = jnp.sum(oh1 * before, axis=-1, keepdims=True)
    r2 = jnp.sum(oh2 * before, axis=-1, keepdims=True)
    cnt_ref[...] = cnt_ref[...] + jnp.sum(oh, axis=0, keepdims=True)
    ti_ref[...] = jnp.where(lane == 0, i1, jnp.where(lane == 1, i2, jnp.where(
        lane == 2, r1, jnp.where(lane == 3, r2, 0.0)))).astype(I32)


def _rmsmod(x, g, shift, scale, out_dtype, w_router=None):
    b, s, d = x.shape
    ts = _fit(s, 256, 8)
    specs = [pl.BlockSpec((None, ts, d), lambda i, j: (i, j, 0)),
             pl.BlockSpec((1, d), lambda i, j: (0, 0)),
             pl.BlockSpec((None, 1, d), lambda i, j: (i, 0, 0)),
             pl.BlockSpec((None, 1, d), lambda i, j: (i, 0, 0))]
    args = [x, g.reshape(1, d), shift.reshape(b, 1, d), scale.reshape(b, 1, d)]
    tile = pl.BlockSpec((None, ts, d), lambda i, j: (i, j, 0))
    blk = _nbytes((ts, d), F32) * 2 + 3 * _nbytes((1, d), F32)
    if w_router is None:
        return pl.pallas_call(
            _rmsmod_kernel, grid=(b, s // ts), in_specs=specs, out_specs=tile,
            out_shape=jax.ShapeDtypeStruct((b, s, d), out_dtype),
            compiler_params=_cparams(("arbitrary", "arbitrary"), blk, 4 * _nbytes((ts, d), F32)),
            name="rmsmod")(*args)
    n_exp = w_router.shape[1]
    wr = jnp.pad(w_router, ((0, 0), (0, V7X_LANES - n_exp)))
    lanes = pl.BlockSpec((None, ts, V7X_LANES), lambda i, j: (i, j, 0))
    blk += _nbytes((d, V7X_LANES), F32) + 2 * _nbytes((ts, V7X_LANES), F32)
    return pl.pallas_call(
        functools.partial(_rmsmod_route_kernel, n_exp=n_exp),
        grid=(b, s // ts),
        in_specs=specs + [pl.BlockSpec((d, V7X_LANES), lambda i, j: (0, 0))],
        out_specs=[tile, lanes, lanes, pl.BlockSpec((8, V7X_LANES), lambda i, j: (0, 0))],
        out_shape=[jax.ShapeDtypeStruct((b, s, d), F32),
                   jax.ShapeDtypeStruct((b, s, V7X_LANES), F32),
                   jax.ShapeDtypeStruct((b, s, V7X_LANES), I32),
                   jax.ShapeDtypeStruct((8, V7X_LANES), F32)],
        compiler_params=_cparams(("arbitrary", "arbitrary"), blk, 8 * _nbytes((ts, d), F32)),
        name="rmsmod_route")(*args, wr)


def _ws_kernel(te_ref, tf_ref, tv_ref, tr_ref, tn_ref, wc_ref, *refs, nx, nw, ne, compute, dil, w_blocks, bn):
    del tr_ref
    x_refs = refs[:nx]
    w_refs = refs[nx:nx + nw]
    e_refs = refs[nx + nw:nx + nw + ne]
    o_ref = refs[nx + nw + ne]
    rest = refs[nx + nw + ne + 1:]
    wb_refs, stage_refs, sem = rest[:nw], rest[nw:2 * nw], rest[2 * nw]
    j = pl.program_id(0)
    t = pl.program_id(1)

    def weight_copy(i, expert, col_tile):
        k, kb = w_blocks[i]
        col = pl.multiple_of(wc_ref[col_tile] * bn, bn)
        return pltpu.make_async_copy(w_refs[i].at[expert, pl.ds(kb * k, k), pl.ds(col, bn)],
                                     stage_refs[i], sem.at[i])

    @pl.when(tf_ref[t] == 1)
    def _new_weights():
        @pl.when((j == 0) & (t == 0))
        def _first_block():
            for i in range(nw):
                weight_copy(i, te_ref[0], 0).start()

        for i in range(nw):
            weight_copy(i, te_ref[t], j).wait()
            wb_refs[i][...] = stage_refs[i][...].astype(BF16)

        nxt = tn_ref[t]

        @pl.when(nxt >= 0)
        def _next_run():
            for i in range(nw):
                weight_copy(i, te_ref[jnp.maximum(nxt, 0)], j).start()

        @pl.when((nxt < 0) & (j + 1 < pl.num_programs(0)))
        def _next_column_tile():
            for i in range(nw):
                weight_copy(i, te_ref[0], j + 1).start()

    @pl.when(tv_ref[t] == 1)
    def _compute():
        res = compute(x_refs, wb_refs, e_refs)
        if dil == 1:
            o_ref[...] = res.astype(o_ref.dtype)
        else:
            perm_ref = rest[-1]
            n_chunks, bm, lanes = perm_ref.shape
            rows = bm // dil
            for c in range(n_chunks):
                perm_ref[c] = res[:, c * lanes:(c + 1) * lanes]
            for r in range(dil):
                for c in range(n_chunks):
                    o_ref[r, :, c * lanes:(c + 1) * lanes] = \
                        perm_ref[c, pl.ds(r, rows, stride=dil), :].astype(o_ref.dtype)

    @pl.when(tv_ref[t] == 0)
    def _unused_tile():
        o_ref[...] = jnp.zeros(o_ref.shape, o_ref.dtype)


def _dense_sched(n_tiles, expert=0):
    first = np.zeros((n_tiles,), np.int32)
    first[0] = 1
    return (jnp.full((n_tiles,), expert, I32), jnp.asarray(first), jnp.ones((n_tiles,), I32),
            jnp.arange(n_tiles, dtype=I32), jnp.full((n_tiles,), -1, I32))


def _tile_extra(bm, bn, col_off):
    return pl.BlockSpec((bm, bn), lambda j, t, te, tf, tv, tr, *_: (tr[t], col_off + j))


def _batch_row_extra(bm, bn, rows_per_batch):
    return pl.BlockSpec((None, 1, bn), lambda j, t, te, tf, tv, tr, *_: ((tr[t] * bm) // rows_per_batch, 0, j))


def _ws_matmul(name, compute, xs, ws, extras, out_rows, out_cols, out_dtype, bm, bn, sched,
               w_cols=None, dil=1, rows_per_batch=None):
    n_tiles = sched[0].shape[0]
    nj = out_cols // bn
    if w_cols is None:
        w_cols = np.arange(nj)
    assert len(w_cols) == nj
    w_cols = jnp.asarray(np.asarray(w_cols, np.int32))
    in_specs, args, blk = [], [], 0
    for arr, k, kb in xs:
        in_specs.append(pl.BlockSpec((bm, k), lambda j, t, te, tf, tv, tr, *_, kb=kb: (tr[t], kb)))
        args.append(arr)
        blk += _nbytes((bm, k), arr.dtype)
    wb, stage, scratch_bytes = [], [], 0
    for arr, k, kb in ws:
        in_specs.append(pl.BlockSpec(memory_space=pl.ANY))
        args.append(arr)
        wb.append(pltpu.VMEM((k, bn), BF16))
        stage.append(pltpu.VMEM((k, bn), arr.dtype))
        scratch_bytes += _nbytes((k, bn), BF16) + _nbytes((k, bn), arr.dtype)
    scratch = wb + stage + [pltpu.SemaphoreType.DMA((len(ws),))]
    for arr, spec in extras:
        in_specs.append(spec)
        args.append(arr)
        blk += _nbytes([d for d in spec.block_shape if d is not None], arr.dtype)
    blk += _nbytes((bm, bn), out_dtype)
    scratch_bytes += 2 * _nbytes((bm, bn), F32)
    if dil == 1:
        out_spec = pl.BlockSpec((bm, bn), lambda j, t, *_: (t, j))
        out_shape = jax.ShapeDtypeStruct((out_rows, out_cols), out_dtype)
    else:
        tpb = rows_per_batch // bm
        assert rows_per_batch % bm == 0 and bm % dil == 0
        out_spec = pl.BlockSpec((None, dil, bm // dil, bn), lambda j, t, *_: (t // tpb, 0, t % tpb, j))
        out_shape = jax.ShapeDtypeStruct((out_rows // rows_per_batch, dil, rows_per_batch // dil, out_cols),
                                         out_dtype)
        assert bn % V7X_LANES == 0
        scratch.append(pltpu.VMEM((bn // V7X_LANES, bm, V7X_LANES), F32))
        scratch_bytes += _nbytes((bm, bn), F32)
    kern = functools.partial(_ws_kernel, nx=len(xs), nw=len(ws), ne=len(extras), compute=compute, dil=dil,
                             w_blocks=tuple((k, kb) for _, k, kb in ws), bn=bn)
    return pl.pallas_call(
        kern,
        grid_spec=pltpu.PrefetchScalarGridSpec(
            num_scalar_prefetch=6, grid=(nj, n_tiles), in_specs=in_specs,
            out_specs=out_spec, scratch_shapes=scratch),
        out_shape=out_shape,
        compiler_params=_cparams(("arbitrary", "arbitrary"), blk, scratch_bytes),
        name=name,
    )(*sched, w_cols, *args)


def _mm(x, wb_ref):
    return jnp.dot(x, wb_ref[...], preferred_element_type=F32)


def _plain_compute(xr, wr, er):
    return _mm(xr[0][...], wr[0])


def _swiglu_compute(xr, wr, er):
    x = xr[0][...]
    a = _mm(x, wr[0])
    return a * jax.nn.sigmoid(a) * _mm(x, wr[1])


def _merge_compute(xr, wr, er):
    conv_out = _mm(xr[0][...], wr[0])
    attn_out = _mm(xr[1][...], wr[1])
    gc = er[0][...].astype(F32)
    ga = er[1][...].astype(F32)
    return jax.nn.sigmoid(gc) * conv_out + jax.nn.sigmoid(ga) * attn_out


def _residual_compute(xr, wr, er):
    return er[0][...] + er[1][...] * _mm(xr[0][...], wr[0])


CONV_HALO = 32


def _conv_kernel(a_ref, b_ref, ah_ref, bh_ref, w_ref, cb_ref, g_ref, beta_ref, o_ref, ubuf, sbuf, cbuf,
                 *, ts, width, ch):
    i = pl.program_id(1)
    a = a_ref[...].astype(F32)
    b = b_ref[...].astype(F32)
    ubuf[CONV_HALO:CONV_HALO + ts, :] = a * jax.nn.sigmoid(b)
    ah = ah_ref[...].astype(F32)
    bh = bh_ref[...].astype(F32)
    ubuf[0:CONV_HALO, :] = jnp.where(i > 0, ah * jax.nn.sigmoid(bh), 0.0)
    span = sbuf.shape[1]
    for s in range(1, V7X_SUBLANES):
        sbuf[s - 1] = ubuf[s:s + span, :]
    off = CONV_HALO - (width - 1)
    rch = V7X_SUBLANES
    for r0 in range(0, ts, rch):
        acc = jnp.zeros((rch, ch), F32)
        for j in range(width):
            shift, row = (off + j) % V7X_SUBLANES, r0 + (off + j) // V7X_SUBLANES * V7X_SUBLANES
            src = ubuf if shift == 0 else sbuf.at[shift - 1]
            acc = acc + w_ref[j] * src[row:row + rch, :]
        cbuf[r0:r0 + rch, :] = acc + cb_ref[...]
    y = cbuf[...]
    mu = jnp.mean(y, axis=-1, keepdims=True)
    yc = y - mu
    var = jnp.mean(yc * yc, axis=-1, keepdims=True)
    yn = yc * lax.rsqrt(var + 1e-5) * g_ref[...] + beta_ref[...]
    o_ref[...] = (yn * jax.nn.sigmoid(yn)).astype(o_ref.dtype)


def _conv_branch(p3, conv_w, conv_b, ln_g, ln_b):
    b, s, _ = p3.shape
    width, ch = conv_w.shape
    assert width - 1 <= CONV_HALO
    ts = _fit(s, 128, CONV_HALO)
    hb = ts // CONV_HALO
    tile = lambda col: pl.BlockSpec((None, ts, ch), lambda bi, i, col=col: (bi, i, col))
    halo = lambda col: pl.BlockSpec((None, CONV_HALO, ch),
                                    lambda bi, i, col=col: (bi, jnp.maximum(i * hb - 1, 0), col))
    row = pl.BlockSpec((1, ch), lambda bi, i: (0, 0))
    blk = 2 * _nbytes((ts + CONV_HALO, ch), BF16) + _nbytes((width * V7X_SUBLANES + 3, ch), F32) + _nbytes((ts, ch), BF16)
    shifted = (V7X_SUBLANES - 1, ts + CONV_HALO - V7X_SUBLANES, ch)
    scratch_bytes = _nbytes((2 * ts + CONV_HALO, ch), F32) + _nbytes(shifted, F32)
    return pl.pallas_call(
        functools.partial(_conv_kernel, ts=ts, width=width, ch=ch),
        grid=(b, s // ts),
        in_specs=[tile(0), tile(1), halo(0), halo(1),
                  pl.BlockSpec((width, V7X_SUBLANES, ch), lambda bi, i: (0, 0, 0)), row, row, row],
        out_specs=pl.BlockSpec((None, ts, ch), lambda bi, i: (bi, i, 0)),
        out_shape=jax.ShapeDtypeStruct((b, s, ch), BF16),
        scratch_shapes=[pltpu.VMEM((ts + CONV_HALO, ch), F32), pltpu.VMEM(shifted, F32),
                        pltpu.VMEM((ts, ch), F32)],
        compiler_params=_cparams(("arbitrary", "arbitrary"), blk, scratch_bytes + 4 * _nbytes((ts, ch), F32)),
        name="conv_module",
    )(p3, p3, p3, p3, jnp.broadcast_to(conv_w[:, None, :], (width, V7X_SUBLANES, ch)),
      conv_b.reshape(1, ch), ln_g.reshape(1, ch), ln_b.reshape(1, ch))


def _bias_kernel(rel_ref, idx_ref, o_ref, *, n_buckets, nh, span, blk):
    g = pl.program_id(0)
    h = pl.program_id(1)
    idx = idx_ref[...]
    acc = jnp.zeros(idx.shape, F32)
    for bkt in range(n_buckets):
        acc = jnp.where(idx == bkt, rel_ref[bkt, g * nh + h], acc)
    qi = lax.broadcasted_iota(I32, idx.shape, 0) + blk
    kj = lax.broadcasted_iota(I32, idx.shape, 1)
    delta = qi - kj
    o_ref[...] = jnp.where((delta >= 0) & (delta <= span), acc, NEG_INF)


def _t5_bucket(dist, n_buckets):
    max_exact = n_buckets // 2
    n = jnp.maximum(dist, 1).astype(F32)
    large = max_exact + (jnp.log(n / max_exact) / math.log(MAX_DISTANCE / max_exact)
                         * (n_buckets - max_exact)).astype(I32)
    large = jnp.minimum(large, n_buckets - 1)
    return jnp.where(dist < max_exact, dist, large)


def _bias_tables(rel_bias, nh):
    n_buckets = rel_bias.shape[0]
    blk = ATTN_BLOCK
    span = ATTN_GROUPS[0][0] // ATTN_GROUPS[0][1]
    assert all(w // d == span for w, d in ATTN_GROUPS)
    delta = (jnp.arange(blk, dtype=I32)[:, None] + blk) - jnp.arange(2 * blk, dtype=I32)[None, :]
    idx = jnp.stack([_t5_bucket(jnp.maximum(delta, 0) * d, n_buckets) for _, d in ATTN_GROUPS])
    ng = len(ATTN_GROUPS)
    return pl.pallas_call(
        functools.partial(_bias_kernel, n_buckets=n_buckets, nh=nh, span=span, blk=blk),
        grid=(ng, nh),
        in_specs=[pl.BlockSpec(memory_space=pltpu.SMEM),
                  pl.BlockSpec((None, blk, 2 * blk), lambda g, h: (g, 0, 0))],
        out_specs=pl.BlockSpec((None, None, blk, 2 * blk), lambda g, h: (g, h, 0, 0)),
        out_shape=jax.ShapeDtypeStruct((ng, nh, blk, 2 * blk), F32),
        name="rel_bias_tables",
    )(rel_bias, idx)


ATTN_HEAD_PAIR = 2


def _attn_kernel(q_ref, k_ref, v_ref, qg_ref, kg_ref, bias_ref, seg_ref, o_ref, lse_ref, kprev, vprev,
                 *, nh, hd, blk):
    has_prev = pl.program_id(2) > 0
    scale = hd ** -0.5
    nt = (((1,), (1,)), ((), ()))
    pw = ATTN_HEAD_PAIR * hd
    seg = seg_ref[...]
    ones = jnp.ones((blk, hd), BF16)

    @pl.when(jnp.logical_not(has_prev))
    def _no_previous_block():
        kprev[...] = jnp.zeros(kprev.shape, kprev.dtype)
        vprev[...] = jnp.zeros(vprev.shape, vprev.dtype)

    def qk_norm(ref, c0, gain):
        x = ref[:, c0:c0 + pw].astype(F32)
        sq = x * x
        hi = sq.astype(BF16)
        lo = (sq - hi.astype(F32)).astype(BF16)
        ss = jnp.dot(hi, seg, preferred_element_type=F32) + jnp.dot(lo, seg, preferred_element_type=F32)
        return (x * lax.rsqrt(ss * (1.0 / hd) + 1e-6) * gain).astype(BF16)

    for c0 in range(0, nh * hd, pw):
        qn = qk_norm(q_ref, c0, qg_ref[...])
        kn = qk_norm(k_ref, c0, kg_ref[...])
        kpn = kprev[:, c0:c0 + pw]
        for i in range(ATTN_HEAD_PAIR):
            h = c0 // hd + i
            sl = slice(i * hd, (i + 1) * hd)
            hs = slice(h * hd, (h + 1) * hd)
            bias = bias_ref[h]
            s_p = lax.dot_general(qn[:, sl], kpn[:, sl], nt, preferred_element_type=F32) * scale \
                + jnp.where(has_prev, bias[:, :blk], NEG_INF)
            s_o = lax.dot_general(qn[:, sl], kn[:, sl], nt, preferred_element_type=F32) * scale + bias[:, blk:]
            m = jnp.max(jnp.maximum(s_p, s_o), axis=-1, keepdims=True)
            p_p = jnp.exp(s_p - m).astype(BF16)
            p_o = jnp.exp(s_o - m).astype(BF16)
            res = jnp.dot(p_p, jnp.concatenate([vprev[:, hs], ones], axis=1), preferred_element_type=F32) \
                + jnp.dot(p_o, jnp.concatenate([v_ref[:, hs], ones], axis=1), preferred_element_type=F32)
            l = res[:, hd:]
            o_ref[:, hs] = (res[:, :hd] / l).astype(o_ref.dtype)
            lse_ref[:, hs] = m + jnp.log(l)
        kprev[:, c0:c0 + pw] = kn
    vprev[...] = v_ref[...]


def _attn_group(pa, g, dil, qb, kb, vb, nh, hd, q_gain, k_gain, bias_g):
    b, s, _ = pa.shape
    blk = ATTN_BLOCK
    seq = s // dil
    assert s % dil == 0 and seq % blk == 0
    nb = seq // blk
    aw = nh * hd
    assert nh % ATTN_HEAD_PAIR == 0
    pw = ATTN_HEAD_PAIR * hd
    own = lambda c: pl.BlockSpec((None, blk, aw), lambda bi, r, n, c=c: (bi, r * nb + n, c))
    gain = pl.BlockSpec((1, pw), lambda bi, r, n: (0, 0))
    out_spec = pl.BlockSpec((None, blk, aw), lambda bi, r, n: (bi, r * nb + n, 0))
    lane_head = np.arange(pw) // hd
    seg = jnp.asarray((lane_head[:, None] == lane_head[None, :]).astype(np.float32), BF16)
    tile_gain = lambda gn: jnp.tile(gn, ATTN_HEAD_PAIR).reshape(1, pw)
    blk_bytes = 3 * _nbytes((blk, aw), BF16) + _nbytes((nh, blk, 2 * blk), F32) \
        + _nbytes((blk, aw), BF16) + _nbytes((blk, aw), F32) + _nbytes((pw, pw), BF16)
    o, lse = pl.pallas_call(
        functools.partial(_attn_kernel, nh=nh, hd=hd, blk=blk),
        grid=(b, dil, nb),
        in_specs=[own(qb), own(kb), own(vb), gain, gain,
                  pl.BlockSpec((nh, blk, 2 * blk), lambda bi, r, n: (0, 0, 0)),
                  pl.BlockSpec((pw, pw), lambda bi, r, n: (0, 0))],
        out_specs=[out_spec, out_spec],
        out_shape=[jax.ShapeDtypeStruct((b, s, aw), BF16), jax.ShapeDtypeStruct((b, s, aw), F32)],
        scratch_shapes=[pltpu.VMEM((blk, aw), BF16), pltpu.VMEM((blk, aw), BF16)],
        compiler_params=_cparams(("arbitrary", "arbitrary", "arbitrary"), blk_bytes,
                                 2 * _nbytes((blk, aw), BF16)),
        name=f"dilated_attn_g{g}",
    )(pa, pa, pa, tile_gain(q_gain), tile_gain(k_gain), bias_g, seg)
    return o.reshape(b, dil, seq, aw), lse.reshape(b, dil, seq, aw)


def _group_merge_kernel(*refs, dils, ts):
    ng = len(dils)
    o_refs, l_refs, out_ref = refs[:ng], refs[ng:2 * ng], refs[2 * ng]
    scratch = list(refs[2 * ng + 1:])

    def token_order(ref, dil):
        if dil == 1:
            return ref[0].astype(F32)
        buf = scratch.pop(0)
        n_chunks, _, lanes = buf.shape
        rows = ts // dil
        for r in range(dil):
            for c in range(n_chunks):
                buf[c, pl.ds(r, rows, stride=dil), :] = ref[r, :, c * lanes:(c + 1) * lanes].astype(F32)
        return jnp.concatenate([buf[c] for c in range(n_chunks)], axis=1)

    ls = [token_order(r, dil) for r, dil in zip(l_refs, dils)]
    m = functools.reduce(jnp.maximum, ls)
    es = [jnp.exp(l - m) for l in ls]
    den = functools.reduce(lambda a, c: a + c, es)
    num = None
    for e, r, dil in zip(es, o_refs, dils):
        term = e * token_order(r, dil)
        num = term if num is None else num + term
    out_ref[...] = (num / den).astype(out_ref.dtype)


def _group_merge(outs, lses, dils):
    b, _, _, aw = outs[0].shape
    s = outs[0].shape[1] * outs[0].shape[2]
    ts = _fit(s, 256, 16 * max(dils))
    spec = lambda dil: pl.BlockSpec((None, dil, ts // dil, aw), lambda i, j: (i, 0, j, 0))
    specs = [spec(dil) for dil in dils]
    n_perm = 2 * sum(1 for dil in dils if dil > 1)
    blk = len(dils) * (_nbytes((ts, aw), BF16) + _nbytes((ts, aw), F32)) + _nbytes((ts, aw), BF16)
    return pl.pallas_call(
        functools.partial(_group_merge_kernel, dils=tuple(dils), ts=ts),
        grid=(b, s // ts), in_specs=specs + specs,
        out_specs=pl.BlockSpec((None, ts, aw), lambda i, j: (i, j, 0)),
        out_shape=jax.ShapeDtypeStruct((b, s, aw), BF16),
        scratch_shapes=[pltpu.VMEM((aw // V7X_LANES, ts, V7X_LANES), F32)] * n_perm,
        compiler_params=_cparams(("arbitrary", "arbitrary"), blk, (n_perm + 4) * _nbytes((ts, aw), F32)),
        name="attn_group_merge")(*outs, *lses)


def _row_copy(src_hbm, row, dst, slot, sem):
    return pltpu.make_async_copy(src_hbm.at[pl.ds(row, 1), :], dst.at[pl.ds(slot, 1), :], sem)


ROW_DMA_UNROLL = 8
DMA_PRIORITIES = 2


def _gather_kernel(src_ref, h_hbm, o_ref, buf, sem, *, rows):
    t = pl.program_id(0)
    slot = t % 2

    def fetch(step, dst_slot):
        def start(i, carry):
            for p in range(DMA_PRIORITIES):
                row = i * DMA_PRIORITIES + p
                _row_copy(h_hbm, src_ref[step * rows + row], buf.at[dst_slot], row,
                          sem.at[dst_slot]).start(priority=p)
            return carry
        lax.fori_loop(0, rows // DMA_PRIORITIES, start, 0, unroll=ROW_DMA_UNROLL // DMA_PRIORITIES)

    @pl.when(t == 0)
    def _first():
        fetch(0, 0)

    @pl.when(t + 1 < pl.num_programs(0))
    def _next():
        fetch(t + 1, 1 - slot)

    def wait(i, carry):
        _row_copy(h_hbm, 0, buf.at[slot], i, sem.at[slot]).wait()
        return carry

    lax.fori_loop(0, rows, wait, 0, unroll=ROW_DMA_UNROLL)
    o_ref[...] = buf[slot].astype(o_ref.dtype)


def _moe_gather(h2, src, rows):
    r_total = src.shape[0]
    d = h2.shape[1]
    return pl.pallas_call(
        functools.partial(_gather_kernel, rows=rows),
        grid_spec=pltpu.PrefetchScalarGridSpec(
            num_scalar_prefetch=1, grid=(r_total // rows,),
            in_specs=[pl.BlockSpec(memory_space=pl.ANY)],
            out_specs=pl.BlockSpec((rows, d), lambda t, src: (t, 0)),
            scratch_shapes=[pltpu.VMEM((2, rows, d), F32), pltpu.SemaphoreType.DMA((2,))]),
        out_shape=jax.ShapeDtypeStruct((r_total, d), BF16),
        compiler_params=_cparams(("arbitrary",), _nbytes((rows, d), BF16), 3 * _nbytes((rows, d), F32)),
        name="moe_gather",
    )(src, h2)


def _combine_kernel(pos_ref, y_hbm, x_ref, tw_ref, gt_ref, o_ref, buf, sem, *, rows):
    t = pl.program_id(0)
    slot = t % 2

    def fetch(step, dst_slot):
        def start(i, carry):
            for k in range(TOP_K):
                _row_copy(y_hbm, pos_ref[(step * rows + i) * TOP_K + k], buf.at[dst_slot, k], i,
                          sem.at[dst_slot]).start(priority=k % DMA_PRIORITIES)
            return carry
        lax.fori_loop(0, rows, start, 0, unroll=ROW_DMA_UNROLL // TOP_K)

    @pl.when(t == 0)
    def _first():
        fetch(0, 0)

    @pl.when(t + 1 < pl.num_programs(0))
    def _next():
        fetch(t + 1, 1 - slot)

    def wait(i, carry):
        for k in range(TOP_K):
            _row_copy(y_hbm, 0, buf.at[slot, k], i, sem.at[slot]).wait()
        return carry

    lax.fori_loop(0, rows, wait, 0, unroll=ROW_DMA_UNROLL // TOP_K)
    tw = tw_ref[...]
    y = tw[:, 0:1] * buf[slot, 0]
    for k in range(1, TOP_K):
        y = y + tw[:, k:k + 1] * buf[slot, k]
    o_ref[...] = x_ref[...] + gt_ref[...] * y


def _moe_combine(y_sorted, pos, x2, topw2, gate, rows_per_batch):
    m, d = x2.shape
    rows = _fit(rows_per_batch, 128, 8)
    blk = 2 * _nbytes((rows, d), F32) + _nbytes((rows, V7X_LANES), F32) + _nbytes((1, d), F32)
    return pl.pallas_call(
        functools.partial(_combine_kernel, rows=rows),
        grid_spec=pltpu.PrefetchScalarGridSpec(
            num_scalar_prefetch=1, grid=(m // rows,),
            in_specs=[pl.BlockSpec(memory_space=pl.ANY),
                      pl.BlockSpec((rows, d), lambda t, pos: (t, 0)),
                      pl.BlockSpec((rows, V7X_LANES), lambda t, pos: (t, 0)),
                      pl.BlockSpec((None, 1, d), lambda t, pos: ((t * rows) // rows_per_batch, 0, 0))],
            out_specs=pl.BlockSpec((rows, d), lambda t, pos: (t, 0)),
            scratch_shapes=[pltpu.VMEM((2, TOP_K, rows, d), F32), pltpu.SemaphoreType.DMA((2,))]),
        out_shape=jax.ShapeDtypeStruct((m, d), F32),
        compiler_params=_cparams(("arbitrary",), blk, (2 * TOP_K + 2) * _nbytes((rows, d), F32)),
        name="moe_combine",
    )(pos, y_sorted, x2, topw2, gate)


def _moe_schedule(route_i, counts, n_exp, bm, expert_base):
    m = route_i.shape[0]
    n_assign = m * TOP_K
    n_tiles = n_assign // bm + n_exp
    e = route_i[:, :TOP_K].reshape(n_assign)
    rank = route_i[:, TOP_K:2 * TOP_K].reshape(n_assign)
    onehot = (e[:, None] == jnp.arange(n_exp, dtype=I32)[None, :]).astype(I32)
    tiles_per = (counts + bm - 1) // bm
    tile_end = jnp.cumsum(tiles_per)
    row_start = (tile_end - tiles_per) * bm
    pos = jnp.sum(onehot * row_start[None, :], axis=1) + rank
    used = tile_end[-1]
    t = jnp.arange(n_tiles, dtype=I32)
    tr = jnp.minimum(t, used - 1)
    te = jnp.minimum(jnp.sum((tr[:, None] >= tile_end[None, :]).astype(I32), axis=1), n_exp - 1)
    tv = (t < used).astype(I32)
    prev_e = jnp.concatenate([jnp.full((1,), -1, I32), te[:-1]])
    tf = tv * ((te != prev_e).astype(I32))
    later_start = (t[None, :] > t[:, None]) & (tf[None, :] == 1)
    tn = jnp.min(jnp.where(later_start, t[None, :], n_tiles), axis=1)
    tn = jnp.where(tn == n_tiles, -1, tn)
    src = (jnp.arange(n_tiles * bm, dtype=I32) % m).at[pos].set(jnp.arange(n_assign, dtype=I32) // TOP_K)
    sched = (te.astype(I32) + expert_base, tf.astype(I32), tv, tr.astype(I32), tn.astype(I32))
    return sched, pos.astype(I32), src


def kernel(x, c, w_ada, b_ada, g_mix, g_ffn, w_in, conv_w, conv_b, conv_ln_g, conv_ln_b, w_conv_out,
           q_gain, k_gain, rel_bias, w_attn_out, w_out, ffn_w1, ffn_w3, ffn_w2, moe_router, moe_w1,
           moe_w3, moe_w2):
    b, s, d = x.shape
    m = b * s
    depth = w_ada.shape[0]
    in_w = w_in.shape[2]
    ch = conv_w.shape[2]
    hd = q_gain.shape[1]
    att_out_w = w_attn_out.shape[1]
    nh = att_out_w // hd
    ng = len(ATTN_GROUPS)
    att_w = rel_bias.shape[1] * hd
    q_off = 2 * ch
    gate_off = q_off + 3 * att_w
    assert in_w == gate_off + 2 * d and rel_bias.shape[1] == ng * nh

    bm = _fit(s, 1024)
    stack = lambda w: w.reshape((w.shape[0] * w.shape[1],) + w.shape[2:])
    bn = _fit(d, 512)
    aw = nh * hd
    dils = [dil for _, dil in ATTN_GROUPS]
    assert dils[0] == 1 and q_off % bn == 0 and aw % bn == 0 and q_off % aw == 0

    cb = lambda start, width: list(range(start // bn, (start + width) // bn))
    qkv_cols = lambda g: sum([cb(q_off + i * att_w + g * aw, aw) for i in range(3)], [])
    nat_cols = cb(0, q_off) + qkv_cols(0) + cb(gate_off, 2 * d)
    nat_gate = q_off + 3 * aw

    mod = _adaln(c, w_ada, b_ada)
    bias = _bias_tables(rel_bias, nh)

    for l in range(depth):
        sh1, sc1, gt1, sh2, sc2, gt2 = [mod[l, :, i * d:(i + 1) * d] for i in range(6)]
        gt1 = gt1.reshape(b, 1, d)
        gt2 = gt2.reshape(b, 1, d)

        h = _rmsmod(x, g_mix[l], sh1, sc1, BF16).reshape(m, d)
        sched = _dense_sched(m // bm, l)
        p = _ws_matmul("in_proj", _plain_compute, [(h, d, 0)], [(w_in, d, 0)], [],
                       m, len(nat_cols) * bn, BF16, bm, bn, sched, w_cols=nat_cols)
        p3 = p.reshape(b, s, len(nat_cols) * bn)
        cu = _conv_branch(p3, conv_w[l], conv_b[l], conv_ln_g[l], conv_ln_b[l]).reshape(m, ch)
        outs, lses = [], []
        for g, dil in enumerate(dils):
            if dil == 1:
                pa, qb = p3, q_off // aw
            else:
                pa = _ws_matmul(f"in_proj_g{g}", _plain_compute, [(h, d, 0)], [(w_in, d, 0)], [],
                                m, 3 * aw, BF16, bm, bn, sched, w_cols=qkv_cols(g), dil=dil,
                                rows_per_batch=s).reshape(b, s, 3 * aw)
                qb = 0
            o_g, lse_g = _attn_group(pa, g, dil, qb, qb + 1, qb + 2, nh, hd, q_gain[l], k_gain[l], bias[g])
            outs.append(o_g)
            lses.append(lse_g)
        o = _group_merge(outs, lses, dils).reshape(m, att_out_w)
        merged = _ws_matmul(
            "branch_merge", _merge_compute, [(cu, ch, 0), (o, att_out_w, 0)],
            [(w_conv_out, ch, 0), (w_attn_out, att_out_w, 0)],
            [(p, _tile_extra(bm, bn, nat_gate // bn)), (p, _tile_extra(bm, bn, (nat_gate + d) // bn))],
            m, d, BF16, bm, bn, sched)
        x2 = _ws_matmul(
            "out_proj", _residual_compute, [(merged, d, 0)], [(w_out, d, 0)],
            [(x.reshape(m, d), _tile_extra(bm, bn, 0)), (gt1, _batch_row_extra(bm, bn, s))],
            m, d, F32, bm, bn, sched)
        x = x2.reshape(b, s, d)

        j = l // 2
        if l % 2 == 0:
            h = _rmsmod(x, g_ffn[l], sh2, sc2, BF16).reshape(m, d)
            d_ff = ffn_w1.shape[2]
            hid = _ws_matmul("ffn_up", _swiglu_compute, [(h, d, 0)],
                             [(ffn_w1, d, 0), (ffn_w3, d, 0)], [],
                             m, d_ff, BF16, bm, _fit(d_ff, 256), _dense_sched(m // bm, j))
            ksplit = 2 if d_ff % (2 * V7X_LANES) == 0 else 1
            kc = d_ff // ksplit
            bm2 = _fit(s, 512)
            sched2 = _dense_sched(m // bm2, j)
            for kb in range(ksplit):
                x2 = _ws_matmul(
                    "ffn_down", _residual_compute, [(hid, kc, kb)], [(ffn_w2, kc, kb)],
                    [(x2, _tile_extra(bm2, bn, 0)), (gt2, _batch_row_extra(bm2, bn, s))],
                    m, d, F32, bm2, bn, sched2)
            x = x2.reshape(b, s, d)
        else:
            n_exp = moe_router.shape[2]
            d_fe = moe_w1.shape[3]
            hf, topw, topi, cnt = _rmsmod(x, g_ffn[l], sh2, sc2, F32, w_router=moe_router[j])
            bme = _fit(m, 512)
            msched, pos, src = _moe_schedule(topi.reshape(m, V7X_LANES), cnt[0, :n_exp].astype(I32), n_exp, bme,
                                             j * n_exp)
            n_rows = src.shape[0]
            xs = _moe_gather(hf.reshape(m, d), src, _fit(bme, 128, 16))
            hid = _ws_matmul("moe_up", _swiglu_compute, [(xs, d, 0)],
                             [(stack(moe_w1), d, 0), (stack(moe_w3), d, 0)], [],
                             n_rows, d_fe, BF16, bme, _fit(d_fe, 256), msched)
            ys = _ws_matmul("moe_down", _plain_compute, [(hid, d_fe, 0)], [(stack(moe_w2), d_fe, 0)], [],
                            n_rows, d, F32, bme, bn, msched)
            x = _moe_combine(ys, pos, x2, topw.reshape(m, V7X_LANES), gt2, s).reshape(b, s, d)
    return x
```

```python
import functools
import math

import numpy as np
import jax
import jax.numpy as jnp
from jax import lax
from jax.experimental import pallas as pl
from jax.experimental.pallas import tpu as pltpu

F32 = jnp.float32
BF16 = jnp.bfloat16
I32 = jnp.int32

ATTN_GROUPS = ((128, 1), (512, 4), (2048, 16))
ATTN_BLOCK = 128
MAX_DISTANCE = 2048
TOP_K = 2

V7X_VMEM_BYTES = 64 * 1024 * 1024
V7X_LANES = 128
V7X_SUBLANES = 8
VMEM_REQUEST_CAP = V7X_VMEM_BYTES - 6 * 1024 * 1024
VMEM_SLACK = 8 * 1024 * 1024
NEG_INF = float("-inf")


def _fit(n, pref, unit=V7X_LANES):
    best = None
    for d in range(unit, min(n, pref) + 1, unit):
        if n % d == 0:
            best = d
    return best if best is not None else n


def _nbytes(shape, dtype):
    return int(np.prod(shape)) * jnp.dtype(dtype).itemsize


def _cparams(semantics, block_bytes, scratch_bytes=0):
    limit = min(VMEM_REQUEST_CAP, 2 * block_bytes + scratch_bytes + VMEM_SLACK)
    return pltpu.CompilerParams(dimension_semantics=semantics, vmem_limit_bytes=int(limit))


def _ada_kernel(c_ref, w_ref, b_ref, o_ref):
    c = c_ref[...]
    ca = (c * jax.nn.sigmoid(c)).astype(BF16)
    o_ref[...] = jnp.dot(ca, w_ref[...].astype(BF16), preferred_element_type=F32) + b_ref[...]


def _adaln(c, w_ada, b_ada):
    depth, d, n = w_ada.shape
    b = c.shape[0]
    rows = 8
    c8 = jnp.pad(c, ((0, rows - b), (0, 0)))
    bn = _fit(n, 1024)
    blk = _nbytes((rows, d), F32) + _nbytes((d, bn), F32) + 2 * _nbytes((rows, bn), F32)
    out = pl.pallas_call(
        _ada_kernel,
        grid=(depth, n // bn),
        in_specs=[pl.BlockSpec((rows, d), lambda l, j: (0, 0)),
                  pl.BlockSpec((None, d, bn), lambda l, j: (l, 0, j)),
                  pl.BlockSpec((None, 1, bn), lambda l, j: (l, 0, j))],
        out_specs=pl.BlockSpec((None, rows, bn), lambda l, j: (l, 0, j)),
        out_shape=jax.ShapeDtypeStruct((depth, rows, n), F32),
        compiler_params=_cparams(("arbitrary", "arbitrary"), blk, _nbytes((d, bn), BF16)),
        name="adaln",
    )(c8, w_ada, b_ada.reshape(depth, 1, n))
    return out[:, :b]


def _normed(x_ref, g_ref, sh_ref, sc_ref):
    x = x_ref[...]
    ms = jnp.mean(x * x, axis=-1, keepdims=True)
    y = x * lax.rsqrt(ms + 1e-6) * g_ref[...]
    return y * (1.0 + sc_ref[...]) + sh_ref[...]


def _rmsmod_kernel(x_ref, g_ref, sh_ref, sc_ref, o_ref):
    o_ref[...] = _normed(x_ref, g_ref, sh_ref, sc_ref).astype(o_ref.dtype)


def _rmsmod_route_kernel(x_ref, g_ref, sh_ref, sc_ref, wr_ref, h_ref, tw_ref, ti_ref, cnt_ref, *, n_exp):
    h = _normed(x_ref, g_ref, sh_ref, sc_ref)
    h_ref[...] = h
    logits = jnp.dot(h, wr_ref[...], preferred_element_type=F32, precision=lax.Precision.HIGHEST)
    lane = lax.broadcasted_iota(I32, logits.shape, 1)
    lanef = lane.astype(F32)
    sentinel = float(V7X_LANES)
    lg = jnp.where(lane < n_exp, logits, NEG_INF)
    m1 = jnp.max(lg, axis=-1, keepdims=True)
    i1 = jnp.min(jnp.where(lg == m1, lanef, sentinel), axis=-1, keepdims=True)
    lg2 = jnp.where(lanef == i1, NEG_INF, lg)
    m2 = jnp.max(lg2, axis=-1, keepdims=True)
    i2 = jnp.min(jnp.where(lg2 == m2, lanef, sentinel), axis=-1, keepdims=True)
    e2 = jnp.exp(m2 - m1)
    den = 1.0 + e2
    tw_ref[...] = jnp.where(lane == 0, 1.0 / den, jnp.where(lane == 1, e2 / den, 0.0))

    @pl.when((pl.program_id(0) == 0) & (pl.program_id(1) == 0))
    def _init_counts():
        cnt_ref[...] = jnp.zeros(cnt_ref.shape, F32)

    oh1 = (lanef == i1).astype(F32)
    oh2 = (lanef == i2).astype(F32)
    oh = oh1 + oh2
    ts = oh.shape[0]
    earlier = (lax.broadcasted_iota(I32, (ts, ts), 1) < lax.broadcasted_iota(I32, (ts, ts), 0))
    before = cnt_ref[0:1, :] + jnp.dot(earlier.astype(BF16), oh.astype(BF16), preferred_element_type=F32)
    r1 = jnp.sum(oh1 * before, axis=-1, keepdims=True)
    r2 = jnp.sum(oh2 * before, axis=-1, keepdims=True)
    cnt_ref[...] = cnt_ref[...] + jnp.sum(oh, axis=0, keepdims=True)
    ti_ref[...] = jnp.where(lane == 0, i1, jnp.where(lane == 1, i2, jnp.where(
        lane == 2, r1, jnp.where(lane == 3, r2, 0.0)))).astype(I32)


def _rmsmod(x, g, shift, scale, out_dtype, w_router=None):
    b, s, d = x.shape
    ts = _fit(s, 256, 8)
    specs = [pl.BlockSpec((None, ts, d), lambda i, j: (i, j, 0)),
             pl.BlockSpec((1, d), lambda i, j: (0, 0)),
             pl.BlockSpec((None, 1, d), lambda i, j: (i, 0, 0)),
             pl.BlockSpec((None, 1, d), lambda i, j: (i, 0, 0))]
    args = [x, g.reshape(1, d), shift.reshape(b, 1, d), scale.reshape(b, 1, d)]
    tile = pl.BlockSpec((None, ts, d), lambda i, j: (i, j, 0))
    blk = _nbytes((ts, d), F32) * 2 + 3 * _nbytes((1, d), F32)
    if w_router is None:
        return pl.pallas_call(
            _rmsmod_kernel, grid=(b, s // ts), in_specs=specs, out_specs=tile,
            out_shape=jax.ShapeDtypeStruct((b, s, d), out_dtype),
            compiler_params=_cparams(("arbitrary", "arbitrary"), blk, 4 * _nbytes((ts, d), F32)),
            name="rmsmod")(*args)
    n_exp = w_router.shape[1]
    wr = jnp.pad(w_router, ((0, 0), (0, V7X_LANES - n_exp)))
    lanes = pl.BlockSpec((None, ts, V7X_LANES), lambda i, j: (i, j, 0))
    blk += _nbytes((d, V7X_LANES), F32) + 2 * _nbytes((ts, V7X_LANES), F32)
    return pl.pallas_call(
        functools.partial(_rmsmod_route_kernel, n_exp=n_exp),
        grid=(b, s // ts),
        in_specs=specs + [pl.BlockSpec((d, V7X_LANES), lambda i, j: (0, 0))],
        out_specs=[tile, lanes, lanes, pl.BlockSpec((8, V7X_LANES), lambda i, j: (0, 0))],
        out_shape=[jax.ShapeDtypeStruct((b, s, d), F32),
                   jax.ShapeDtypeStruct((b, s, V7X_LANES), F32),
                   jax.ShapeDtypeStruct((b, s, V7X_LANES), I32),
                   jax.ShapeDtypeStruct((8, V7X_LANES), F32)],
        compiler_params=_cparams(("arbitrary", "arbitrary"), blk, 8 * _nbytes((ts, d), F32)),
        name="rmsmod_route")(*args, wr)


def _ws_kernel(te_ref, tf_ref, tv_ref, tr_ref, tn_ref, wc_ref, *refs, nx, nw, ne, compute, dil, w_blocks, bn):
    del tr_ref
    x_refs = refs[:nx]
    w_refs = refs[nx:nx + nw]
    e_refs = refs[nx + nw:nx + nw + ne]
    o_ref = refs[nx + nw + ne]
    rest = refs[nx + nw + ne + 1:]
    wb_refs, stage_refs, sem = rest[:nw], rest[nw:2 * nw], rest[2 * nw]
    j = pl.program_id(0)
    t = pl.program_id(1)

    def weight_copy(i, expert, col_tile):
        k, kb = w_blocks[i]
        col = pl.multiple_of(wc_ref[col_tile] * bn, bn)
        return pltpu.make_async_copy(w_refs[i].at[expert, pl.ds(kb * k, k), pl.ds(col, bn)],
                                     stage_refs[i], sem.at[i])

    @pl.when(tf_ref[t] == 1)
    def _new_weights():
        @pl.when((j == 0) & (t == 0))
        def _first_block():
            for i in range(nw):
                weight_copy(i, te_ref[0], 0).start()

        for i in range(nw):
            weight_copy(i, te_ref[t], j).wait()
            wb_refs[i][...] = stage_refs[i][...].astype(BF16)

        nxt = tn_ref[t]

        @pl.when(nxt >= 0)
        def _next_run():
            for i in range(nw):
                weight_copy(i, te_ref[jnp.maximum(nxt, 0)], j).start()

        @pl.when((nxt < 0) & (j + 1 < pl.num_programs(0)))
        def _next_column_tile():
            for i in range(nw):
                weight_copy(i, te_ref[0], j + 1).start()

    @pl.when(tv_ref[t] == 1)
    def _compute():
        res = compute(x_refs, wb_refs, e_refs)
        if dil == 1:
            o_ref[...] = res.astype(o_ref.dtype)
        else:
            perm_ref = rest[-1]
            n_chunks, bm, lanes = perm_ref.shape
            rows = bm // dil
            for c in range(n_chunks):
                perm_ref[c] = res[:, c * lanes:(c + 1) * lanes]
            for r in range(dil):
                for c in range(n_chunks):
                    o_ref[r, :, c * lanes:(c + 1) * lanes] = \
                        perm_ref[c, pl.ds(r, rows, stride=dil), :].astype(o_ref.dtype)

    @pl.when(tv_ref[t] == 0)
    def _unused_tile():
        o_ref[...] = jnp.zeros(o_ref.shape, o_ref.dtype)


def _dense_sched(n_tiles, expert=0):
    first = np.zeros((n_tiles,), np.int32)
    first[0] = 1
    return (jnp.full((n_tiles,), expert, I32), jnp.asarray(first), jnp.ones((n_tiles,), I32),
            jnp.arange(n_tiles, dtype=I32), jnp.full((n_tiles,), -1, I32))


def _tile_extra(bm, bn, col_off):
    return pl.BlockSpec((bm, bn), lambda j, t, te, tf, tv, tr, *_: (tr[t], col_off + j))


def _batch_row_extra(bm, bn, rows_per_batch):
    return pl.BlockSpec((None, 1, bn), lambda j, t, te, tf, tv, tr, *_: ((tr[t] * bm) // rows_per_batch, 0, j))


def _ws_matmul(name, compute, xs, ws, extras, out_rows, out_cols, out_dtype, bm, bn, sched,
               w_cols=None, dil=1, rows_per_batch=None):
    n_tiles = sched[0].shape[0]
    nj = out_cols // bn
    if w_cols is None:
        w_cols = np.arange(nj)
    assert len(w_cols) == nj
    w_cols = jnp.asarray(np.asarray(w_cols, np.int32))
    in_specs, args, blk = [], [], 0
    for arr, k, kb in xs:
        in_specs.append(pl.BlockSpec((bm, k), lambda j, t, te, tf, tv, tr, *_, kb=kb: (tr[t], kb)))
        args.append(arr)
        blk += _nbytes((bm, k), arr.dtype)
    wb, stage, scratch_bytes = [], [], 0
    for arr, k, kb in ws:
        in_specs.append(pl.BlockSpec(memory_space=pl.ANY))
        args.append(arr)
        wb.append(pltpu.VMEM((k, bn), BF16))
        stage.append(pltpu.VMEM((k, bn), arr.dtype))
        scratch_bytes += _nbytes((k, bn), BF16) + _nbytes((k, bn), arr.dtype)
    scratch = wb + stage + [pltpu.SemaphoreType.DMA((len(ws),))]
    for arr, spec in extras:
        in_specs.append(spec)
        args.append(arr)
        blk += _nbytes([d for d in spec.block_shape if d is not None], arr.dtype)
    blk += _nbytes((bm, bn), out_dtype)
    scratch_bytes += 2 * _nbytes((bm, bn), F32)
    if dil == 1:
        out_spec = pl.BlockSpec((bm, bn), lambda j, t, *_: (t, j))
        out_shape = jax.ShapeDtypeStruct((out_rows, out_cols), out_dtype)
    else:
        tpb = rows_per_batch // bm
        assert rows_per_batch % bm == 0 and bm % dil == 0
        out_spec = pl.BlockSpec((None, dil, bm // dil, bn), lambda j, t, *_: (t // tpb, 0, t % tpb, j))
        out_shape = jax.ShapeDtypeStruct((out_rows // rows_per_batch, dil, rows_per_batch // dil, out_cols),
                                         out_dtype)
        assert bn % V7X_LANES == 0
        scratch.append(pltpu.VMEM((bn // V7X_LANES, bm, V7X_LANES), F32))
        scratch_bytes += _nbytes((bm, bn), F32)
    kern = functools.partial(_ws_kernel, nx=len(xs), nw=len(ws), ne=len(extras), compute=compute, dil=dil,
                             w_blocks=tuple((k, kb) for _, k, kb in ws), bn=bn)
    return pl.pallas_call(
        kern,
        grid_spec=pltpu.PrefetchScalarGridSpec(
            num_scalar_prefetch=6, grid=(nj, n_tiles), in_specs=in_specs,
            out_specs=out_spec, scratch_shapes=scratch),
        out_shape=out_shape,
        compiler_params=_cparams(("arbitrary", "arbitrary"), blk, scratch_bytes),
        name=name,
    )(*sched, w_cols, *args)


def _mm(x, wb_ref):
    return jnp.dot(x, wb_ref[...], preferred_element_type=F32)


def _plain_compute(xr, wr, er):
    return _mm(xr[0][...], wr[0])


def _swiglu_compute(xr, wr, er):
    x = xr[0][...]
    a = _mm(x, wr[0])
    return a * jax.nn.sigmoid(a) * _mm(x, wr[1])


def _merge_compute(xr, wr, er):
    conv_out = _mm(xr[0][...], wr[0])
    attn_out = _mm(xr[1][...], wr[1])
    gc = er[0][...].astype(F32)
    ga = er[1][...].astype(F32)
    return jax.nn.sigmoid(gc) * conv_out + jax.nn.sigmoid(ga) * attn_out


def _residual_compute(xr, wr, er):
    return er[0][...] + er[1][...] * _mm(xr[0][...], wr[0])


CONV_HALO = 32


def _conv_kernel(a_ref, b_ref, ah_ref, bh_ref, w_ref, cb_ref, g_ref, beta_ref, o_ref, ubuf, sbuf, cbuf,
                 *, ts, width, ch):
    i = pl.program_id(1)
    a = a_ref[...].astype(F32)
    b = b_ref[...].astype(F32)
    ubuf[CONV_HALO:CONV_HALO + ts, :] = a * jax.nn.sigmoid(b)
    ah = ah_ref[...].astype(F32)
    bh = bh_ref[...].astype(F32)
    ubuf[0:CONV_HALO, :] = jnp.where(i > 0, ah * jax.nn.sigmoid(bh), 0.0)
    span = sbuf.shape[1]
    for s in range(1, V7X_SUBLANES):
        sbuf[s - 1] = ubuf[s:s + span, :]
    off = CONV_HALO - (width - 1)
    rch = V7X_SUBLANES
    for r0 in range(0, ts, rch):
        acc = jnp.zeros((rch, ch), F32)
        for j in range(width):
            shift, row = (off + j) % V7X_SUBLANES, r0 + (off + j) // V7X_SUBLANES * V7X_SUBLANES
            src = ubuf if shift == 0 else sbuf.at[shift - 1]
            acc = acc + w_ref[j] * src[row:row + rch, :]
        cbuf[r0:r0 + rch, :] = acc + cb_ref[...]
    y = cbuf[...]
    mu = jnp.mean(y, axis=-1, keepdims=True)
    yc = y - mu
    var = jnp.mean(yc * yc, axis=-1, keepdims=True)
    yn = yc * lax.rsqrt(var + 1e-5) * g_ref[...] + beta_ref[...]
    o_ref[...] = (yn * jax.nn.sigmoid(yn)).astype(o_ref.dtype)


def _conv_branch(p3, conv_w, conv_b, ln_g, ln_b):
    b, s, _ = p3.shape
    width, ch = conv_w.shape
    assert width - 1 <= CONV_HALO
    ts = _fit(s, 128, CONV_HALO)
    hb = ts // CONV_HALO
    tile = lambda col: pl.BlockSpec((None, ts, ch), lambda bi, i, col=col: (bi, i, col))
    halo = lambda col: pl.BlockSpec((None, CONV_HALO, ch),
                                    lambda bi, i, col=col: (bi, jnp.maximum(i * hb - 1, 0), col))
    row = pl.BlockSpec((1, ch), lambda bi, i: (0, 0))
    blk = 2 * _nbytes((ts + CONV_HALO, ch), BF16) + _nbytes((width * V7X_SUBLANES + 3, ch), F32) + _nbytes((ts, ch), BF16)
    shifted = (V7X_SUBLANES - 1, ts + CONV_HALO - V7X_SUBLANES, ch)
    scratch_bytes = _nbytes((2 * ts + CONV_HALO, ch), F32) + _nbytes(shifted, F32)
    return pl.pallas_call(
        functools.partial(_conv_kernel, ts=ts, width=width, ch=ch),
        grid=(b, s // ts),
        in_specs=[tile(0), tile(1), halo(0), halo(1),
                  pl.BlockSpec((width, V7X_SUBLANES, ch), lambda bi, i: (0, 0, 0)), row, row, row],
        out_specs=pl.BlockSpec((None, ts, ch), lambda bi, i: (bi, i, 0)),
        out_shape=jax.ShapeDtypeStruct((b, s, ch), BF16),
        scratch_shapes=[pltpu.VMEM((ts + CONV_HALO, ch), F32), pltpu.VMEM(shifted, F32),
                        pltpu.VMEM((ts, ch), F32)],
        compiler_params=_cparams(("arbitrary", "arbitrary"), blk, scratch_bytes + 4 * _nbytes((ts, ch), F32)),
        name="conv_module",
    )(p3, p3, p3, p3, jnp.broadcast_to(conv_w[:, None, :], (width, V7X_SUBLANES, ch)),
      conv_b.reshape(1, ch), ln_g.reshape(1, ch), ln_b.reshape(1, ch))


def _bias_kernel(rel_ref, idx_ref, o_ref, *, n_buckets, nh, span, blk):
    g = pl.program_id(0)
    h = pl.program_id(1)
    idx = idx_ref[...]
    acc = jnp.zeros(idx.shape, F32)
    for bkt in range(n_buckets):
        acc = jnp.where(idx == bkt, rel_ref[bkt, g * nh + h], acc)
    qi = lax.broadcasted_iota(I32, idx.shape, 0) + blk
    kj = lax.broadcasted_iota(I32, idx.shape, 1)
    delta = qi - kj
    o_ref[...] = jnp.where((delta >= 0) & (delta <= span), acc, NEG_INF)


def _t5_bucket(dist, n_buckets):
    max_exact = n_buckets // 2
    n = jnp.maximum(dist, 1).astype(F32)
    large = max_exact + (jnp.log(n / max_exact) / math.log(MAX_DISTANCE / max_exact)
                         * (n_buckets - max_exact)).astype(I32)
    large = jnp.minimum(large, n_buckets - 1)
    return jnp.where(dist < max_exact, dist, large)


def _bias_tables(rel_bias, nh):
    n_buckets = rel_bias.shape[0]
    blk = ATTN_BLOCK
    span = ATTN_GROUPS[0][0] // ATTN_GROUPS[0][1]
    assert all(w // d == span for w, d in ATTN_GROUPS)
    delta = (jnp.arange(blk, dtype=I32)[:, None] + blk) - jnp.arange(2 * blk, dtype=I32)[None, :]
    idx = jnp.stack([_t5_bucket(jnp.maximum(delta, 0) * d, n_buckets) for _, d in ATTN_GROUPS])
    ng = len(ATTN_GROUPS)
    return pl.pallas_call(
        functools.partial(_bias_kernel, n_buckets=n_buckets, nh=nh, span=span, blk=blk),
        grid=(ng, nh),
        in_specs=[pl.BlockSpec(memory_space=pltpu.SMEM),
                  pl.BlockSpec((None, blk, 2 * blk), lambda g, h: (g, 0, 0))],
        out_specs=pl.BlockSpec((None, None, blk, 2 * blk), lambda g, h: (g, h, 0, 0)),
        out_shape=jax.ShapeDtypeStruct((ng, nh, blk, 2 * blk), F32),
        name="rel_bias_tables",
    )(rel_bias, idx)


ATTN_HEAD_PAIR = 2


def _attn_kernel(q_ref, k_ref, v_ref, qg_ref, kg_ref, bias_ref, seg_ref, o_ref, lse_ref, kprev, vprev,
                 qn_s, kn_s, s_s, p_s, m_s, *, nh, hd, blk):
    has_prev = pl.program_id(2) > 0
    scale = hd ** -0.5
    nt = (((1,), (1,)), ((), ()))
    pw = ATTN_HEAD_PAIR * hd
    seg = seg_ref[...]
    ones = jnp.ones((blk, hd), BF16)

    @pl.when(jnp.logical_not(has_prev))
    def _no_previous_block():
        kprev[...] = jnp.zeros(kprev.shape, kprev.dtype)
        vprev[...] = jnp.zeros(vprev.shape, vprev.dtype)

    def qk_norm(ref, c0, gain):
        x = ref[:, c0:c0 + pw].astype(F32)
        sq = x * x
        hi = sq.astype(BF16)
        lo = (sq - hi.astype(F32)).astype(BF16)
        ss = jnp.dot(hi, seg, preferred_element_type=F32) + jnp.dot(lo, seg, preferred_element_type=F32)
        return (x * lax.rsqrt(ss * (1.0 / hd) + 1e-6) * gain).astype(BF16)

    for c0 in range(0, nh * hd, pw):
        qn_s[:, c0:c0 + pw] = qk_norm(q_ref, c0, qg_ref[...])
        kn_s[:, c0:c0 + pw] = qk_norm(k_ref, c0, kg_ref[...])
    for h in range(nh):
        hs = slice(h * hd, (h + 1) * hd)
        bias = bias_ref[h]
        q = qn_s[:, hs]
        s_s[h, :, :blk] = lax.dot_general(q, kprev[:, hs], nt, preferred_element_type=F32) * scale \
            + jnp.where(has_prev, bias[:, :blk], NEG_INF)
        s_s[h, :, blk:] = lax.dot_general(q, kn_s[:, hs], nt, preferred_element_type=F32) * scale + bias[:, blk:]
    for h in range(nh):
        s = s_s[h]
        m = jnp.max(s, axis=-1, keepdims=True)
        m_s[h] = jnp.broadcast_to(m, (blk, hd))
        p_s[h] = jnp.exp(s - m).astype(BF16)
    assert hd == V7X_LANES and nh <= V7X_LANES
    head_lane = lax.broadcasted_iota(I32, (blk, hd), 1)
    lse = jnp.zeros((blk, hd), F32)
    for h in range(nh):
        hs = slice(h * hd, (h + 1) * hd)
        res = jnp.dot(p_s[h, :, :blk], jnp.concatenate([vprev[:, hs], ones], axis=1),
                      preferred_element_type=F32) \
            + jnp.dot(p_s[h, :, blk:], jnp.concatenate([v_ref[:, hs], ones], axis=1),
                      preferred_element_type=F32)
        l = res[:, hd:]
        o_ref[:, hs] = (res[:, :hd] / l).astype(o_ref.dtype)
        lse = jnp.where(head_lane == h, m_s[h] + jnp.log(l), lse)
    lse_ref[...] = lse
    kprev[...] = kn_s[...]
    vprev[...] = v_ref[...]


def _attn_group(pa, g, dil, qb, kb, vb, nh, hd, q_gain, k_gain, bias_g):
    b, s, _ = pa.shape
    blk = ATTN_BLOCK
    seq = s // dil
    assert s % dil == 0 and seq % blk == 0
    nb = seq // blk
    aw = nh * hd
    assert nh % ATTN_HEAD_PAIR == 0
    pw = ATTN_HEAD_PAIR * hd
    own = lambda c: pl.BlockSpec((None, blk, aw), lambda bi, r, n, c=c: (bi, r * nb + n, c))
    gain = pl.BlockSpec((1, pw), lambda bi, r, n: (0, 0))
    out_spec = pl.BlockSpec((None, blk, aw), lambda bi, r, n: (bi, r * nb + n, 0))
    lane_head = np.arange(pw) // hd
    seg = jnp.asarray((lane_head[:, None] == lane_head[None, :]).astype(np.float32), BF16)
    tile_gain = lambda gn: jnp.tile(gn, ATTN_HEAD_PAIR).reshape(1, pw)
    blk_bytes = 3 * _nbytes((blk, aw), BF16) + _nbytes((nh, blk, 2 * blk), F32) \
        + _nbytes((blk, aw), BF16) + _nbytes((blk, aw), F32) + _nbytes((pw, pw), BF16)
    o, lse = pl.pallas_call(
        functools.partial(_attn_kernel, nh=nh, hd=hd, blk=blk),
        grid=(b, dil, nb),
        in_specs=[own(qb), own(kb), own(vb), gain, gain,
                  pl.BlockSpec((nh, blk, 2 * blk), lambda bi, r, n: (0, 0, 0)),
                  pl.BlockSpec((pw, pw), lambda bi, r, n: (0, 0))],
        out_specs=[out_spec, pl.BlockSpec((None, blk, V7X_LANES), lambda bi, r, n: (bi, r * nb + n, 0))],
        out_shape=[jax.ShapeDtypeStruct((b, s, aw), BF16), jax.ShapeDtypeStruct((b, s, V7X_LANES), F32)],
        scratch_shapes=[pltpu.VMEM((blk, aw), BF16), pltpu.VMEM((blk, aw), BF16),
                        pltpu.VMEM((blk, aw), BF16), pltpu.VMEM((blk, aw), BF16),
                        pltpu.VMEM((nh, blk, 2 * blk), F32), pltpu.VMEM((nh, blk, 2 * blk), BF16),
                        pltpu.VMEM((nh, blk, hd), F32)],
        compiler_params=_cparams(("arbitrary", "arbitrary", "arbitrary"), blk_bytes,
                                 4 * _nbytes((blk, aw), BF16) + 2 * _nbytes((nh, blk, 2 * blk), F32)),
        name=f"dilated_attn_g{g}",
    )(pa, pa, pa, tile_gain(q_gain), tile_gain(k_gain), bias_g, seg)
    return o.reshape(b, dil, seq, aw), lse.reshape(b, dil, seq, V7X_LANES)


def _group_merge_kernel(*refs, dils, ts):
    ng = len(dils)
    o_refs, l_refs, expand_ref, out_ref = refs[:ng], refs[ng:2 * ng], refs[2 * ng], refs[2 * ng + 1]
    scratch = list(refs[2 * ng + 2:])

    def token_order(ref, dil):
        if dil == 1:
            return ref[0].astype(F32)
        buf = scratch.pop(0)
        n_chunks, _, lanes = buf.shape
        rows = ts // dil
        for r in range(dil):
            for c in range(n_chunks):
                buf[c, pl.ds(r, rows, stride=dil), :] = ref[r, :, c * lanes:(c + 1) * lanes].astype(F32)
        return jnp.concatenate([buf[c] for c in range(n_chunks)], axis=1)

    def spread(w):
        hi = w.astype(BF16)
        lo = (w - hi.astype(F32)).astype(BF16)
        return jnp.dot(hi, expand_ref[...], preferred_element_type=F32) \
            + jnp.dot(lo, expand_ref[...], preferred_element_type=F32)

    ls = [token_order(r, dil) for r, dil in zip(l_refs, dils)]
    m = functools.reduce(jnp.maximum, ls)
    es = [jnp.exp(l - m) for l in ls]
    den = functools.reduce(lambda a, c: a + c, es)
    num = None
    for e, r, dil in zip(es, o_refs, dils):
        term = spread(e / den) * token_order(r, dil)
        num = term if num is None else num + term
    out_ref[...] = num.astype(out_ref.dtype)


def _group_merge(outs, lses, dils, hd):
    b, _, _, aw = outs[0].shape
    s = outs[0].shape[1] * outs[0].shape[2]
    nh = aw // hd
    ts = _fit(s, 256, 16 * max(dils))
    spec = lambda dil, w: pl.BlockSpec((None, dil, ts // dil, w), lambda i, j: (i, 0, j, 0))
    n_perm = sum(1 for dil in dils if dil > 1)
    lane = np.arange(V7X_LANES)[:, None]
    expand = jnp.asarray((lane == (np.arange(aw)[None, :] // hd)).astype(np.float32), BF16)
    blk = len(dils) * (_nbytes((ts, aw), BF16) + _nbytes((ts, V7X_LANES), F32)) + _nbytes((ts, aw), BF16) \
        + _nbytes((V7X_LANES, aw), BF16)
    assert nh <= V7X_LANES
    return pl.pallas_call(
        functools.partial(_group_merge_kernel, dils=tuple(dils), ts=ts),
        grid=(b, s // ts),
        in_specs=[spec(dil, aw) for dil in dils] + [spec(dil, V7X_LANES) for dil in dils]
        + [pl.BlockSpec((V7X_LANES, aw), lambda i, j: (0, 0))],
        out_specs=pl.BlockSpec((None, ts, aw), lambda i, j: (i, j, 0)),
        out_shape=jax.ShapeDtypeStruct((b, s, aw), BF16),
        scratch_shapes=[pltpu.VMEM((1, ts, V7X_LANES), F32)] * n_perm
        + [pltpu.VMEM((aw // V7X_LANES, ts, V7X_LANES), F32)] * n_perm,
        compiler_params=_cparams(("arbitrary", "arbitrary"), blk, (n_perm + 4) * _nbytes((ts, aw), F32)),
        name="attn_group_merge")(*outs, *lses, expand)


def _row_copy(src_hbm, row, dst, slot, sem):
    return pltpu.make_async_copy(src_hbm.at[pl.ds(row, 1), :], dst.at[pl.ds(slot, 1), :], sem)


ROW_DMA_UNROLL = 8
DMA_PRIORITIES = 2


def _gather_kernel(src_ref, h_hbm, o_ref, buf, sem, *, rows):
    t = pl.program_id(0)
    slot = t % 2

    def fetch(step, dst_slot):
        def start(i, carry):
            for p in range(DMA_PRIORITIES):
                row = i * DMA_PRIORITIES + p
                _row_copy(h_hbm, src_ref[step * rows + row], buf.at[dst_slot], row,
                          sem.at[dst_slot]).start(priority=p)
            return carry
        lax.fori_loop(0, rows // DMA_PRIORITIES, start, 0, unroll=ROW_DMA_UNROLL // DMA_PRIORITIES)

    @pl.when(t == 0)
    def _first():
        fetch(0, 0)

    @pl.when(t + 1 < pl.num_programs(0))
    def _next():
        fetch(t + 1, 1 - slot)

    def wait(i, carry):
        _row_copy(h_hbm, 0, buf.at[slot], i, sem.at[slot]).wait()
        return carry

    lax.fori_loop(0, rows, wait, 0, unroll=ROW_DMA_UNROLL)
    o_ref[...] = buf[slot].astype(o_ref.dtype)


def _moe_gather(h2, src, rows):
    r_total = src.shape[0]
    d = h2.shape[1]
    return pl.pallas_call(
        functools.partial(_gather_kernel, rows=rows),
        grid_spec=pltpu.PrefetchScalarGridSpec(
            num_scalar_prefetch=1, grid=(r_total // rows,),
            in_specs=[pl.BlockSpec(memory_space=pl.ANY)],
            out_specs=pl.BlockSpec((rows, d), lambda t, src: (t, 0)),
            scratch_shapes=[pltpu.VMEM((2, rows, d), F32), pltpu.SemaphoreType.DMA((2,))]),
        out_shape=jax.ShapeDtypeStruct((r_total, d), BF16),
        compiler_params=_cparams(("arbitrary",), _nbytes((rows, d), BF16), 3 * _nbytes((rows, d), F32)),
        name="moe_gather",
    )(src, h2)


def _combine_kernel(pos_ref, y_hbm, x_ref, tw_ref, gt_ref, o_ref, buf, sem, *, rows):
    t = pl.program_id(0)
    slot = t % 2

    def fetch(step, dst_slot):
        def start(i, carry):
            for k in range(TOP_K):
                _row_copy(y_hbm, pos_ref[(step * rows + i) * TOP_K + k], buf.at[dst_slot, k], i,
                          sem.at[dst_slot]).start(priority=k % DMA_PRIORITIES)
            return carry
        lax.fori_loop(0, rows, start, 0, unroll=ROW_DMA_UNROLL // TOP_K)

    @pl.when(t == 0)
    def _first():
        fetch(0, 0)

    @pl.when(t + 1 < pl.num_programs(0))
    def _next():
        fetch(t + 1, 1 - slot)

    def wait(i, carry):
        for k in range(TOP_K):
            _row_copy(y_hbm, 0, buf.at[slot, k], i, sem.at[slot]).wait()
        return carry

    lax.fori_loop(0, rows, wait, 0, unroll=ROW_DMA_UNROLL // TOP_K)
    tw = tw_ref[...]
    y = tw[:, 0:1] * buf[slot, 0]
    for k in range(1, TOP_K):
        y = y + tw[:, k:k + 1] * buf[slot, k]
    o_ref[...] = x_ref[...] + gt_ref[...] * y


def _moe_combine(y_sorted, pos, x2, topw2, gate, rows_per_batch):
    m, d = x2.shape
    rows = _fit(rows_per_batch, 128, 8)
    blk = 2 * _nbytes((rows, d), F32) + _nbytes((rows, V7X_LANES), F32) + _nbytes((1, d), F32)
    return pl.pallas_call(
        functools.partial(_combine_kernel, rows=rows),
        grid_spec=pltpu.PrefetchScalarGridSpec(
            num_scalar_prefetch=1, grid=(m // rows,),
            in_specs=[pl.BlockSpec(memory_space=pl.ANY),
                      pl.BlockSpec((rows, d), lambda t, pos: (t, 0)),
                      pl.BlockSpec((rows, V7X_LANES), lambda t, pos: (t, 0)),
                      pl.BlockSpec((None, 1, d), lambda t, pos: ((t * rows) // rows_per_batch, 0, 0))],
            out_specs=pl.BlockSpec((rows, d), lambda t, pos: (t, 0)),
            scratch_shapes=[pltpu.VMEM((2, TOP_K, rows, d), F32), pltpu.SemaphoreType.DMA((2,))]),
        out_shape=jax.ShapeDtypeStruct((m, d), F32),
        compiler_params=_cparams(("arbitrary",), blk, (2 * TOP_K + 2) * _nbytes((rows, d), F32)),
        name="moe_combine",
    )(pos, y_sorted, x2, topw2, gate)


def _moe_schedule(route_i, counts, n_exp, bm, expert_base):
    m = route_i.shape[0]
    n_assign = m * TOP_K
    n_tiles = n_assign // bm + n_exp
    e = route_i[:, :TOP_K].reshape(n_assign)
    rank = route_i[:, TOP_K:2 * TOP_K].reshape(n_assign)
    onehot = (e[:, None] == jnp.arange(n_exp, dtype=I32)[None, :]).astype(I32)
    tiles_per = (counts + bm - 1) // bm
    tile_end = jnp.cumsum(tiles_per)
    row_start = (tile_end - tiles_per) * bm
    pos = jnp.sum(onehot * row_start[None, :], axis=1) + rank
    used = tile_end[-1]
    t = jnp.arange(n_tiles, dtype=I32)
    tr = jnp.minimum(t, used - 1)
    te = jnp.minimum(jnp.sum((tr[:, None] >= tile_end[None, :]).astype(I32), axis=1), n_exp - 1)
    tv = (t < used).astype(I32)
    prev_e = jnp.concatenate([jnp.full((1,), -1, I32), te[:-1]])
    tf = tv * ((te != prev_e).astype(I32))
    later_start = (t[None, :] > t[:, None]) & (tf[None, :] == 1)
    tn = jnp.min(jnp.where(later_start, t[None, :], n_tiles), axis=1)
    tn = jnp.where(tn == n_tiles, -1, tn)
    src = (jnp.arange(n_tiles * bm, dtype=I32) % m).at[pos].set(jnp.arange(n_assign, dtype=I32) // TOP_K)
    sched = (te.astype(I32) + expert_base, tf.astype(I32), tv, tr.astype(I32), tn.astype(I32))
    return sched, pos.astype(I32), src


def kernel(x, c, w_ada, b_ada, g_mix, g_ffn, w_in, conv_w, conv_b, conv_ln_g, conv_ln_b, w_conv_out,
           q_gain, k_gain, rel_bias, w_attn_out, w_out, ffn_w1, ffn_w3, ffn_w2, moe_router, moe_w1,
           moe_w3, moe_w2):
    b, s, d = x.shape
    m = b * s
    depth = w_ada.shape[0]
    in_w = w_in.shape[2]
    ch = conv_w.shape[2]
    hd = q_gain.shape[1]
    att_out_w = w_attn_out.shape[1]
    nh = att_out_w // hd
    ng = len(ATTN_GROUPS)
    att_w = rel_bias.shape[1] * hd
    q_off = 2 * ch
    gate_off = q_off + 3 * att_w
    assert in_w == gate_off + 2 * d and rel_bias.shape[1] == ng * nh

    bm = _fit(s, 1024)
    stack = lambda w: w.reshape((w.shape[0] * w.shape[1],) + w.shape[2:])
    bn = _fit(d, 512)
    aw = nh * hd
    dils = [dil for _, dil in ATTN_GROUPS]
    assert dils[0] == 1 and q_off % bn == 0 and aw % bn == 0 and q_off % aw == 0

    cb = lambda start, width, n=bn: list(range(start // n, (start + width) // n))
    qkv_cols = lambda g, n=bn: sum([cb(q_off + i * att_w + g * aw, aw, n) for i in range(3)], [])
    bn_in = bn
    nat_cols = cb(0, q_off, bn_in) + qkv_cols(0, bn_in) + cb(gate_off, 2 * d, bn_in)
    nat_gate = q_off + 3 * aw

    mod = _adaln(c, w_ada, b_ada)
    bias = _bias_tables(rel_bias, nh)

    for l in range(depth):
        sh1, sc1, gt1, sh2, sc2, gt2 = [mod[l, :, i * d:(i + 1) * d] for i in range(6)]
        gt1 = gt1.reshape(b, 1, d)
        gt2 = gt2.reshape(b, 1, d)

        h = _rmsmod(x, g_mix[l], sh1, sc1, BF16).reshape(m, d)
        sched = _dense_sched(m // bm, l)
        p = _ws_matmul("in_proj", _plain_compute, [(h, d, 0)], [(w_in, d, 0)], [],
                       m, len(nat_cols) * bn_in, BF16, bm, bn_in, sched, w_cols=nat_cols)
        p3 = p.reshape(b, s, len(nat_cols) * bn_in)
        cu = _conv_branch(p3, conv_w[l], conv_b[l], conv_ln_g[l], conv_ln_b[l]).reshape(m, ch)
        outs, lses = [], []
        for g, dil in enumerate(dils):
            if dil == 1:
                pa, qb = p3, q_off // aw
            else:
                pa = _ws_matmul(f"in_proj_g{g}", _plain_compute, [(h, d, 0)], [(w_in, d, 0)], [],
                                m, 3 * aw, BF16, bm, bn, sched, w_cols=qkv_cols(g), dil=dil,
                                rows_per_batch=s).reshape(b, s, 3 * aw)
                qb = 0
            o_g, lse_g = _attn_group(pa, g, dil, qb, qb + 1, qb + 2, nh, hd, q_gain[l], k_gain[l], bias[g])
            outs.append(o_g)
            lses.append(lse_g)
        o = _group_merge(outs, lses, dils, hd).reshape(m, att_out_w)
        merged = _ws_matmul(
            "branch_merge", _merge_compute, [(cu, ch, 0), (o, att_out_w, 0)],
            [(w_conv_out, ch, 0), (w_attn_out, att_out_w, 0)],
            [(p, _tile_extra(bm, bn, nat_gate // bn)), (p, _tile_extra(bm, bn, (nat_gate + d) // bn))],
            m, d, BF16, bm, bn, sched)
        x2 = _ws_matmul(
            "out_proj", _residual_compute, [(merged, d, 0)], [(w_out, d, 0)],
            [(x.reshape(m, d), _tile_extra(bm, bn, 0)), (gt1, _batch_row_extra(bm, bn, s))],
            m, d, F32, bm, bn, sched)
        x = x2.reshape(b, s, d)

        j = l // 2
        if l % 2 == 0:
            h = _rmsmod(x, g_ffn[l], sh2, sc2, BF16).reshape(m, d)
            d_ff = ffn_w1.shape[2]
            hid = _ws_matmul("ffn_up", _swiglu_compute, [(h, d, 0)],
                             [(ffn_w1, d, 0), (ffn_w3, d, 0)], [],
                             m, d_ff, BF16, bm, _fit(d_ff, 256), _dense_sched(m // bm, j))
            ksplit = 2 if d_ff % (2 * V7X_LANES) == 0 else 1
            kc = d_ff // ksplit
            bm2 = _fit(s, 512)
            sched2 = _dense_sched(m // bm2, j)
            for kb in range(ksplit):
                x2 = _ws_matmul(
                    "ffn_down", _residual_compute, [(hid, kc, kb)], [(ffn_w2, kc, kb)],
                    [(x2, _tile_extra(bm2, bn, 0)), (gt2, _batch_row_extra(bm2, bn, s))],
                    m, d, F32, bm2, bn, sched2)
            x = x2.reshape(b, s, d)
        else:
            n_exp = moe_router.shape[2]
            d_fe = moe_w1.shape[3]
            hf, topw, topi, cnt = _rmsmod(x, g_ffn[l], sh2, sc2, F32, w_router=moe_router[j])
            bme = _fit(m, 512)
            msched, pos, src = _moe_schedule(topi.reshape(m, V7X_LANES), cnt[0, :n_exp].astype(I32), n_exp, bme,
                                             j * n_exp)
            n_rows = src.shape[0]
            xs = _moe_gather(hf.reshape(m, d), src, _fit(bme, 128, 16))
            hid = _ws_matmul("moe_up", _swiglu_compute, [(xs, d, 0)],
                             [(stack(moe_w1), d, 0), (stack(moe_w3), d, 0)], [],
                             n_rows, d_fe, BF16, bme, _fit(d_fe, 256), msched)
            ys = _ws_matmul("moe_down", _plain_compute, [(hid, d_fe, 0)], [(stack(moe_w2), d_fe, 0)], [],
                            n_rows, d, F32, bme, bn, msched)
            x = _moe_combine(ys, pos, x2, topw.reshape(m, V7X_LANES), gt2, s).reshape(b, s, d)
    return x
```

```python
import functools
import math

import numpy as np
import jax
import jax.numpy as jnp
from jax import lax
from jax.experimental import pallas as pl
from jax.experimental.pallas import tpu as pltpu

F32 = jnp.float32
BF16 = jnp.bfloat16
I32 = jnp.int32

ATTN_GROUPS = ((128, 1), (512, 4), (2048, 16))
ATTN_BLOCK = 128
MAX_DISTANCE = 2048
TOP_K = 2

V7X_VMEM_BYTES = 64 * 1024 * 1024
V7X_LANES = 128
V7X_SUBLANES = 8
VMEM_REQUEST_CAP = V7X_VMEM_BYTES - 6 * 1024 * 1024
VMEM_SLACK = 8 * 1024 * 1024
NEG_INF = float("-inf")


def _fit(n, pref, unit=V7X_LANES):
    best = None
    for d in range(unit, min(n, pref) + 1, unit):
        if n % d == 0:
            best = d
    return best if best is not None else n


def _nbytes(shape, dtype):
    return int(np.prod(shape)) * jnp.dtype(dtype).itemsize


def _cparams(semantics, block_bytes, scratch_bytes=0):
    limit = min(VMEM_REQUEST_CAP, 2 * block_bytes + scratch_bytes + VMEM_SLACK)
    return pltpu.CompilerParams(dimension_semantics=semantics, vmem_limit_bytes=int(limit))


def _ada_kernel(c_ref, w_ref, b_ref, o_ref):
    c = c_ref[...]
    ca = (c * jax.nn.sigmoid(c)).astype(BF16)
    o_ref[...] = jnp.dot(ca, w_ref[...].astype(BF16), preferred_element_type=F32) + b_ref[...]


def _adaln(c, w_ada, b_ada):
    depth, d, n = w_ada.shape
    b = c.shape[0]
    rows = 8
    c8 = jnp.pad(c, ((0, rows - b), (0, 0)))
    bn = _fit(n, 1024)
    blk = _nbytes((rows, d), F32) + _nbytes((d, bn), F32) + 2 * _nbytes((rows, bn), F32)
    out = pl.pallas_call(
        _ada_kernel,
        grid=(depth, n // bn),
        in_specs=[pl.BlockSpec((rows, d), lambda l, j: (0, 0)),
                  pl.BlockSpec((None, d, bn), lambda l, j: (l, 0, j)),
                  pl.BlockSpec((None, 1, bn), lambda l, j: (l, 0, j))],
        out_specs=pl.BlockSpec((None, rows, bn), lambda l, j: (l, 0, j)),
        out_shape=jax.ShapeDtypeStruct((depth, rows, n), F32),
        compiler_params=_cparams(("arbitrary", "arbitrary"), blk, _nbytes((d, bn), BF16)),
        name="adaln",
    )(c8, w_ada, b_ada.reshape(depth, 1, n))
    return out[:, :b]


NORM_ROWS = 16
NORM_COLS = 512


def _normed_into(x_ref, g_ref, sh_ref, sc_ref, o_ref):
    ts, d = x_ref.shape
    rows, cols = min(NORM_ROWS, ts), min(NORM_COLS, d)
    for r0 in range(0, ts, rows):
        sq = jnp.zeros((rows, cols), F32)
        for c0 in range(0, d, cols):
            x = x_ref[r0:r0 + rows, c0:c0 + cols]
            sq = sq + x * x
        inv = lax.rsqrt(jnp.sum(sq, axis=-1, keepdims=True) * (1.0 / d) + 1e-6)
        for c0 in range(0, d, cols):
            cs = slice(c0, c0 + cols)
            y = x_ref[r0:r0 + rows, cs] * inv * g_ref[:, cs]
            o_ref[r0:r0 + rows, cs] = (y * (1.0 + sc_ref[:, cs]) + sh_ref[:, cs]).astype(o_ref.dtype)


def _rmsmod_kernel(x_ref, g_ref, sh_ref, sc_ref, o_ref):
    _normed_into(x_ref, g_ref, sh_ref, sc_ref, o_ref)


def _rmsmod_route_kernel(x_ref, g_ref, sh_ref, sc_ref, whi_ref, wlo_ref, h_ref, tw_ref, ti_ref, cnt_ref,
                         *, n_exp):
    _normed_into(x_ref, g_ref, sh_ref, sc_ref, h_ref)
    h = h_ref[...]
    h_hi = h.astype(BF16)
    h_lo = (h - h_hi.astype(F32)).astype(BF16)
    logits = jnp.dot(h_hi, whi_ref[...], preferred_element_type=F32) \
        + (jnp.dot(h_lo, whi_ref[...], preferred_element_type=F32)
           + jnp.dot(h_hi, wlo_ref[...], preferred_element_type=F32))
    lane = lax.broadcasted_iota(I32, logits.shape, 1)
    lanef = lane.astype(F32)
    sentinel = float(V7X_LANES)
    lg = jnp.where(lane < n_exp, logits, NEG_INF)
    m1 = jnp.max(lg, axis=-1, keepdims=True)
    i1 = jnp.min(jnp.where(lg == m1, lanef, sentinel), axis=-1, keepdims=True)
    lg2 = jnp.where(lanef == i1, NEG_INF, lg)
    m2 = jnp.max(lg2, axis=-1, keepdims=True)
    i2 = jnp.min(jnp.where(lg2 == m2, lanef, sentinel), axis=-1, keepdims=True)
    e2 = jnp.exp(m2 - m1)
    den = 1.0 + e2
    tw_ref[...] = jnp.where(lane == 0, 1.0 / den, jnp.where(lane == 1, e2 / den, 0.0))

    @pl.when((pl.program_id(0) == 0) & (pl.program_id(1) == 0))
    def _init_counts():
        cnt_ref[...] = jnp.zeros(cnt_ref.shape, F32)

    oh1 = (lanef == i1).astype(F32)
    oh2 = (lanef == i2).astype(F32)
    oh = oh1 + oh2
    ts = oh.shape[0]
    earlier = (lax.broadcasted_iota(I32, (ts, ts), 1) < lax.broadcasted_iota(I32, (ts, ts), 0))
    before = cnt_ref[0:1, :] + jnp.dot(earlier.astype(BF16), oh.astype(BF16), preferred_element_type=F32)
    r1 = jnp.sum(oh1 * before, axis=-1, keepdims=True)
    r2 = jnp.sum(oh2 * before, axis=-1, keepdims=True)
    cnt_ref[...] = cnt_ref[...] + jnp.sum(oh, axis=0, keepdims=True)
    ti_ref[...] = jnp.where(lane == 0, i1, jnp.where(lane == 1, i2, jnp.where(
        lane == 2, r1, jnp.where(lane == 3, r2, 0.0)))).astype(I32)


def _rmsmod(x, g, shift, scale, out_dtype, w_router=None):
    b, s, d = x.shape
    ts = _fit(s, 256, 8)
    specs = [pl.BlockSpec((None, ts, d), lambda i, j: (i, j, 0)),
             pl.BlockSpec((1, d), lambda i, j: (0, 0)),
             pl.BlockSpec((None, 1, d), lambda i, j: (i, 0, 0)),
             pl.BlockSpec((None, 1, d), lambda i, j: (i, 0, 0))]
    args = [x, g.reshape(1, d), shift.reshape(b, 1, d), scale.reshape(b, 1, d)]
    tile = pl.BlockSpec((None, ts, d), lambda i, j: (i, j, 0))
    blk = _nbytes((ts, d), F32) * 2 + 3 * _nbytes((1, d), F32)
    if w_router is None:
        return pl.pallas_call(
            _rmsmod_kernel, grid=(b, s // ts), in_specs=specs, out_specs=tile,
            out_shape=jax.ShapeDtypeStruct((b, s, d), out_dtype),
            compiler_params=_cparams(("arbitrary", "arbitrary"), blk, 4 * _nbytes((ts, d), F32)),
            name="rmsmod")(*args)
    n_exp = w_router.shape[1]
    wr = jnp.pad(w_router, ((0, 0), (0, V7X_LANES - n_exp)))
    wr_hi = wr.astype(BF16)
    wr_lo = (wr - wr_hi.astype(F32)).astype(BF16)
    lanes = pl.BlockSpec((None, ts, V7X_LANES), lambda i, j: (i, j, 0))
    blk += _nbytes((d, V7X_LANES), F32) + 2 * _nbytes((ts, V7X_LANES), F32)
    return pl.pallas_call(
        functools.partial(_rmsmod_route_kernel, n_exp=n_exp),
        grid=(b, s // ts),
        in_specs=specs + [pl.BlockSpec((d, V7X_LANES), lambda i, j: (0, 0))] * 2,
        out_specs=[tile, lanes, lanes, pl.BlockSpec((8, V7X_LANES), lambda i, j: (0, 0))],
        out_shape=[jax.ShapeDtypeStruct((b, s, d), F32),
                   jax.ShapeDtypeStruct((b, s, V7X_LANES), F32),
                   jax.ShapeDtypeStruct((b, s, V7X_LANES), I32),
                   jax.ShapeDtypeStruct((8, V7X_LANES), F32)],
        compiler_params=_cparams(("arbitrary", "arbitrary"), blk, 8 * _nbytes((ts, d), F32)),
        name="rmsmod_route")(*args, wr_hi, wr_lo)


def _ws_kernel(te_ref, tf_ref, tv_ref, tr_ref, tn_ref, wc_ref, *refs, nx, nw, ne, compute, dil, w_blocks, bn):
    del tr_ref
    x_refs = refs[:nx]
    w_refs = refs[nx:nx + nw]
    e_refs = refs[nx + nw:nx + nw + ne]
    o_ref = refs[nx + nw + ne]
    rest = refs[nx + nw + ne + 1:]
    wb_refs, stage_refs, sem = rest[:nw], rest[nw:2 * nw], rest[2 * nw]
    j = pl.program_id(0)
    t = pl.program_id(1)

    def weight_copy(i, expert, col_tile):
        k, kb = w_blocks[i]
        col = pl.multiple_of(wc_ref[col_tile] * bn, bn)
        return pltpu.make_async_copy(w_refs[i].at[expert, pl.ds(kb * k, k), pl.ds(col, bn)],
                                     stage_refs[i], sem.at[i])

    @pl.when(tf_ref[t] == 1)
    def _new_weights():
        @pl.when((j == 0) & (t == 0))
        def _first_block():
            for i in range(nw):
                weight_copy(i, te_ref[0], 0).start()

        for i in range(nw):
            weight_copy(i, te_ref[t], j).wait()
            wb_refs[i][...] = stage_refs[i][...].astype(BF16)

        nxt = tn_ref[t]

        @pl.when(nxt >= 0)
        def _next_run():
            for i in range(nw):
                weight_copy(i, te_ref[jnp.maximum(nxt, 0)], j).start()

        @pl.when((nxt < 0) & (j + 1 < pl.num_programs(0)))
        def _next_column_tile():
            for i in range(nw):
                weight_copy(i, te_ref[0], j + 1).start()

    @pl.when(tv_ref[t] == 1)
    def _compute():
        res = compute(x_refs, wb_refs, e_refs)
        if dil == 1:
            o_ref[...] = res.astype(o_ref.dtype)
        else:
            perm_ref = rest[-1]
            n_chunks, bm, lanes = perm_ref.shape
            rows = bm // dil
            for c in range(n_chunks):
                perm_ref[c] = res[:, c * lanes:(c + 1) * lanes]
            for r in range(dil):
                for c in range(n_chunks):
                    o_ref[r, :, c * lanes:(c + 1) * lanes] = \
                        perm_ref[c, pl.ds(r, rows, stride=dil), :].astype(o_ref.dtype)

    @pl.when(tv_ref[t] == 0)
    def _unused_tile():
        o_ref[...] = jnp.zeros(o_ref.shape, o_ref.dtype)


def _dense_sched(n_tiles, expert=0):
    first = np.zeros((n_tiles,), np.int32)
    first[0] = 1
    return (jnp.full((n_tiles,), expert, I32), jnp.asarray(first), jnp.ones((n_tiles,), I32),
            jnp.arange(n_tiles, dtype=I32), jnp.full((n_tiles,), -1, I32))


def _tile_extra(bm, bn, col_off):
    return pl.BlockSpec((bm, bn), lambda j, t, te, tf, tv, tr, *_: (tr[t], col_off + j))


def _batch_row_extra(bm, bn, rows_per_batch):
    return pl.BlockSpec((None, 1, bn), lambda j, t, te, tf, tv, tr, *_: ((tr[t] * bm) // rows_per_batch, 0, j))


def _ws_matmul(name, compute, xs, ws, extras, out_rows, out_cols, out_dtype, bm, bn, sched,
               w_cols=None, dil=1, rows_per_batch=None):
    n_tiles = sched[0].shape[0]
    nj = out_cols // bn
    if w_cols is None:
        w_cols = np.arange(nj)
    assert len(w_cols) == nj
    w_cols = jnp.asarray(np.asarray(w_cols, np.int32))
    in_specs, args, blk = [], [], 0
    for arr, k, kb in xs:
        in_specs.append(pl.BlockSpec((bm, k), lambda j, t, te, tf, tv, tr, *_, kb=kb: (tr[t], kb)))
        args.append(arr)
        blk += _nbytes((bm, k), arr.dtype)
    wb, stage, scratch_bytes = [], [], 0
    for arr, k, kb in ws:
        in_specs.append(pl.BlockSpec(memory_space=pl.ANY))
        args.append(arr)
        wb.append(pltpu.VMEM((k, bn), BF16))
        stage.append(pltpu.VMEM((k, bn), arr.dtype))
        scratch_bytes += _nbytes((k, bn), BF16) + _nbytes((k, bn), arr.dtype)
    scratch = wb + stage + [pltpu.SemaphoreType.DMA((len(ws),))]
    for arr, spec in extras:
        in_specs.append(spec)
        args.append(arr)
        blk += _nbytes([d for d in spec.block_shape if d is not None], arr.dtype)
    blk += _nbytes((bm, bn), out_dtype)
    scratch_bytes += 2 * _nbytes((bm, bn), F32)
    if dil == 1:
        out_spec = pl.BlockSpec((bm, bn), lambda j, t, *_: (t, j))
        out_shape = jax.ShapeDtypeStruct((out_rows, out_cols), out_dtype)
    else:
        tpb = rows_per_batch // bm
        assert rows_per_batch % bm == 0 and bm % dil == 0
        out_spec = pl.BlockSpec((None, dil, bm // dil, bn), lambda j, t, *_: (t // tpb, 0, t % tpb, j))
        out_shape = jax.ShapeDtypeStruct((out_rows // rows_per_batch, dil, rows_per_batch // dil, out_cols),
                                         out_dtype)
        assert bn % V7X_LANES == 0
        scratch.append(pltpu.VMEM((bn // V7X_LANES, bm, V7X_LANES), F32))
        scratch_bytes += _nbytes((bm, bn), F32)
    kern = functools.partial(_ws_kernel, nx=len(xs), nw=len(ws), ne=len(extras), compute=compute, dil=dil,
                             w_blocks=tuple((k, kb) for _, k, kb in ws), bn=bn)
    return pl.pallas_call(
        kern,
        grid_spec=pltpu.PrefetchScalarGridSpec(
            num_scalar_prefetch=6, grid=(nj, n_tiles), in_specs=in_specs,
            out_specs=out_spec, scratch_shapes=scratch),
        out_shape=out_shape,
        compiler_params=_cparams(("arbitrary", "arbitrary"), blk, scratch_bytes),
        name=name,
    )(*sched, w_cols, *args)


def _mm(x, wb_ref):
    return jnp.dot(x, wb_ref[...], preferred_element_type=F32)


def _plain_compute(xr, wr, er):
    return _mm(xr[0][...], wr[0])


def _swiglu_compute(xr, wr, er):
    x = xr[0][...]
    a = _mm(x, wr[0])
    return a * jax.nn.sigmoid(a) * _mm(x, wr[1])


def _merge_compute(xr, wr, er):
    conv_out = _mm(xr[0][...], wr[0])
    attn_out = _mm(xr[1][...], wr[1])
    gc = er[0][...].astype(F32)
    ga = er[1][...].astype(F32)
    return jax.nn.sigmoid(gc) * conv_out + jax.nn.sigmoid(ga) * attn_out


def _residual_compute(xr, wr, er):
    return er[0][...] + er[1][...] * _mm(xr[0][...], wr[0])


CONV_HALO = 32


def _conv_kernel(a_ref, b_ref, ah_ref, bh_ref, w_ref, cb_ref, g_ref, beta_ref, o_ref, ubuf, sbuf, cbuf,
                 *, ts, width, ch):
    i = pl.program_id(1)
    a = a_ref[...].astype(F32)
    b = b_ref[...].astype(F32)
    ubuf[CONV_HALO:CONV_HALO + ts, :] = a * jax.nn.sigmoid(b)
    ah = ah_ref[...].astype(F32)
    bh = bh_ref[...].astype(F32)
    ubuf[0:CONV_HALO, :] = jnp.where(i > 0, ah * jax.nn.sigmoid(bh), 0.0)
    span = sbuf.shape[1]
    for s in range(1, V7X_SUBLANES):
        sbuf[s - 1] = ubuf[s:s + span, :]
    off = CONV_HALO - (width - 1)
    rch = V7X_SUBLANES
    for r0 in range(0, ts, rch):
        acc = jnp.zeros((rch, ch), F32)
        for j in range(width):
            shift, row = (off + j) % V7X_SUBLANES, r0 + (off + j) // V7X_SUBLANES * V7X_SUBLANES
            src = ubuf if shift == 0 else sbuf.at[shift - 1]
            acc = acc + w_ref[j] * src[row:row + rch, :]
        cbuf[r0:r0 + rch, :] = acc + cb_ref[...]
    y = cbuf[...]
    mu = jnp.mean(y, axis=-1, keepdims=True)
    yc = y - mu
    var = jnp.mean(yc * yc, axis=-1, keepdims=True)
    yn = yc * lax.rsqrt(var + 1e-5) * g_ref[...] + beta_ref[...]
    o_ref[...] = (yn * jax.nn.sigmoid(yn)).astype(o_ref.dtype)


def _conv_branch(p3, conv_w, conv_b, ln_g, ln_b):
    b, s, _ = p3.shape
    width, ch = conv_w.shape
    assert width - 1 <= CONV_HALO
    ts = _fit(s, 128, CONV_HALO)
    hb = ts // CONV_HALO
    tile = lambda col: pl.BlockSpec((None, ts, ch), lambda bi, i, col=col: (bi, i, col))
    halo = lambda col: pl.BlockSpec((None, CONV_HALO, ch),
                                    lambda bi, i, col=col: (bi, jnp.maximum(i * hb - 1, 0), col))
    row = pl.BlockSpec((1, ch), lambda bi, i: (0, 0))
    blk = 2 * _nbytes((ts + CONV_HALO, ch), BF16) + _nbytes((width * V7X_SUBLANES + 3, ch), F32) + _nbytes((ts, ch), BF16)
    shifted = (V7X_SUBLANES - 1, ts + CONV_HALO - V7X_SUBLANES, ch)
    scratch_bytes = _nbytes((2 * ts + CONV_HALO, ch), F32) + _nbytes(shifted, F32)
    return pl.pallas_call(
        functools.partial(_conv_kernel, ts=ts, width=width, ch=ch),
        grid=(b, s // ts),
        in_specs=[tile(0), tile(1), halo(0), halo(1),
                  pl.BlockSpec((width, V7X_SUBLANES, ch), lambda bi, i: (0, 0, 0)), row, row, row],
        out_specs=pl.BlockSpec((None, ts, ch), lambda bi, i: (bi, i, 0)),
        out_shape=jax.ShapeDtypeStruct((b, s, ch), BF16),
        scratch_shapes=[pltpu.VMEM((ts + CONV_HALO, ch), F32), pltpu.VMEM(shifted, F32),
                        pltpu.VMEM((ts, ch), F32)],
        compiler_params=_cparams(("arbitrary", "arbitrary"), blk, scratch_bytes + 4 * _nbytes((ts, ch), F32)),
        name="conv_module",
    )(p3, p3, p3, p3, jnp.broadcast_to(conv_w[:, None, :], (width, V7X_SUBLANES, ch)),
      conv_b.reshape(1, ch), ln_g.reshape(1, ch), ln_b.reshape(1, ch))


def _bias_kernel(rel_ref, idx_ref, o_ref, *, n_buckets, nh, span, blk):
    g = pl.program_id(0)
    h = pl.program_id(1)
    idx = idx_ref[...]
    acc = jnp.zeros(idx.shape, F32)
    for bkt in range(n_buckets):
        acc = jnp.where(idx == bkt, rel_ref[bkt, g * nh + h], acc)
    qi = lax.broadcasted_iota(I32, idx.shape, 0) + blk
    kj = lax.broadcasted_iota(I32, idx.shape, 1)
    delta = qi - kj
    o_ref[...] = jnp.where((delta >= 0) & (delta <= span), acc, NEG_INF)


def _t5_bucket(dist, n_buckets):
    max_exact = n_buckets // 2
    n = jnp.maximum(dist, 1).astype(F32)
    large = max_exact + (jnp.log(n / max_exact) / math.log(MAX_DISTANCE / max_exact)
                         * (n_buckets - max_exact)).astype(I32)
    large = jnp.minimum(large, n_buckets - 1)
    return jnp.where(dist < max_exact, dist, large)


def _bias_tables(rel_bias, nh):
    n_buckets = rel_bias.shape[0]
    blk = ATTN_BLOCK
    span = ATTN_GROUPS[0][0] // ATTN_GROUPS[0][1]
    assert all(w // d == span for w, d in ATTN_GROUPS)
    delta = (jnp.arange(blk, dtype=I32)[:, None] + blk) - jnp.arange(2 * blk, dtype=I32)[None, :]
    idx = jnp.stack([_t5_bucket(jnp.maximum(delta, 0) * d, n_buckets) for _, d in ATTN_GROUPS])
    ng = len(ATTN_GROUPS)
    return pl.pallas_call(
        functools.partial(_bias_kernel, n_buckets=n_buckets, nh=nh, span=span, blk=blk),
        grid=(ng, nh),
        in_specs=[pl.BlockSpec(memory_space=pltpu.SMEM),
                  pl.BlockSpec((None, blk, 2 * blk), lambda g, h: (g, 0, 0))],
        out_specs=pl.BlockSpec((None, None, blk, 2 * blk), lambda g, h: (g, h, 0, 0)),
        out_shape=jax.ShapeDtypeStruct((ng, nh, blk, 2 * blk), F32),
        name="rel_bias_tables",
    )(rel_bias, idx)


ATTN_HEAD_PAIR = 2


def _attn_kernel(q_ref, k_ref, v_ref, qg_ref, kg_ref, bias_ref, seg_ref, o_ref, lse_ref, kprev, vprev,
                 qn_s, kn_s, s_s, p_s, m_s, *, nh, hd, blk):
    has_prev = pl.program_id(2) > 0
    scale = hd ** -0.5
    nt = (((1,), (1,)), ((), ()))
    pw = ATTN_HEAD_PAIR * hd
    seg = seg_ref[...]
    ones = jnp.ones((blk, hd), BF16)

    @pl.when(jnp.logical_not(has_prev))
    def _no_previous_block():
        kprev[...] = jnp.zeros(kprev.shape, kprev.dtype)
        vprev[...] = jnp.zeros(vprev.shape, vprev.dtype)

    def qk_norm(ref, c0, gain):
        x = ref[:, c0:c0 + pw].astype(F32)
        sq = x * x
        hi = sq.astype(BF16)
        lo = (sq - hi.astype(F32)).astype(BF16)
        ss = jnp.dot(hi, seg, preferred_element_type=F32) + jnp.dot(lo, seg, preferred_element_type=F32)
        return (x * lax.rsqrt(ss * (1.0 / hd) + 1e-6) * gain).astype(BF16)

    for c0 in range(0, nh * hd, pw):
        qn_s[:, c0:c0 + pw] = qk_norm(q_ref, c0, qg_ref[...])
        kn_s[:, c0:c0 + pw] = qk_norm(k_ref, c0, kg_ref[...])
    for h in range(nh):
        hs = slice(h * hd, (h + 1) * hd)
        bias = bias_ref[h]
        q = qn_s[:, hs]
        s_s[h, :, :blk] = lax.dot_general(q, kprev[:, hs], nt, preferred_element_type=F32) * scale \
            + jnp.where(has_prev, bias[:, :blk], NEG_INF)
        s_s[h, :, blk:] = lax.dot_general(q, kn_s[:, hs], nt, preferred_element_type=F32) * scale + bias[:, blk:]
    for h in range(nh):
        s = s_s[h]
        m = jnp.max(s, axis=-1, keepdims=True)
        m_s[h] = jnp.broadcast_to(m, (blk, hd))
        p_s[h] = jnp.exp(s - m).astype(BF16)
    assert hd == V7X_LANES and nh <= V7X_LANES
    head_lane = lax.broadcasted_iota(I32, (blk, hd), 1)
    lse = jnp.zeros((blk, hd), F32)
    for h in range(nh):
        hs = slice(h * hd, (h + 1) * hd)
        res = jnp.dot(p_s[h, :, :blk], jnp.concatenate([vprev[:, hs], ones], axis=1),
                      preferred_element_type=F32) \
            + jnp.dot(p_s[h, :, blk:], jnp.concatenate([v_ref[:, hs], ones], axis=1),
                      preferred_element_type=F32)
        l = res[:, hd:]
        o_ref[:, hs] = (res[:, :hd] / l).astype(o_ref.dtype)
        lse = jnp.where(head_lane == h, m_s[h] + jnp.log(l), lse)
    lse_ref[...] = lse
    kprev[...] = kn_s[...]
    vprev[...] = v_ref[...]


def _attn_group(pa, g, dil, qb, kb, vb, nh, hd, q_gain, k_gain, bias_g):
    b, s, _ = pa.shape
    blk = ATTN_BLOCK
    seq = s // dil
    assert s % dil == 0 and seq % blk == 0
    nb = seq // blk
    aw = nh * hd
    assert nh % ATTN_HEAD_PAIR == 0
    pw = ATTN_HEAD_PAIR * hd
    own = lambda c: pl.BlockSpec((None, blk, aw), lambda bi, r, n, c=c: (bi, r * nb + n, c))
    gain = pl.BlockSpec((1, pw), lambda bi, r, n: (0, 0))
    out_spec = pl.BlockSpec((None, blk, aw), lambda bi, r, n: (bi, r * nb + n, 0))
    lane_head = np.arange(pw) // hd
    seg = jnp.asarray((lane_head[:, None] == lane_head[None, :]).astype(np.float32), BF16)
    tile_gain = lambda gn: jnp.tile(gn, ATTN_HEAD_PAIR).reshape(1, pw)
    blk_bytes = 3 * _nbytes((blk, aw), BF16) + _nbytes((nh, blk, 2 * blk), F32) \
        + _nbytes((blk, aw), BF16) + _nbytes((blk, aw), F32) + _nbytes((pw, pw), BF16)
    o, lse = pl.pallas_call(
        functools.partial(_attn_kernel, nh=nh, hd=hd, blk=blk),
        grid=(b, dil, nb),
        in_specs=[own(qb), own(kb), own(vb), gain, gain,
                  pl.BlockSpec((nh, blk, 2 * blk), lambda bi, r, n: (0, 0, 0)),
                  pl.BlockSpec((pw, pw), lambda bi, r, n: (0, 0))],
        out_specs=[out_spec, pl.BlockSpec((None, blk, V7X_LANES), lambda bi, r, n: (bi, r * nb + n, 0))],
        out_shape=[jax.ShapeDtypeStruct((b, s, aw), BF16), jax.ShapeDtypeStruct((b, s, V7X_LANES), F32)],
        scratch_shapes=[pltpu.VMEM((blk, aw), BF16), pltpu.VMEM((blk, aw), BF16),
                        pltpu.VMEM((blk, aw), BF16), pltpu.VMEM((blk, aw), BF16),
                        pltpu.VMEM((nh, blk, 2 * blk), F32), pltpu.VMEM((nh, blk, 2 * blk), BF16),
                        pltpu.VMEM((nh, blk, hd), F32)],
        compiler_params=_cparams(("arbitrary", "arbitrary", "arbitrary"), blk_bytes,
                                 4 * _nbytes((blk, aw), BF16) + 2 * _nbytes((nh, blk, 2 * blk), F32)),
        name=f"dilated_attn_g{g}",
    )(pa, pa, pa, tile_gain(q_gain), tile_gain(k_gain), bias_g, seg)
    return o.reshape(b, dil, seq, aw), lse.reshape(b, dil, seq, V7X_LANES)


def _group_merge_kernel(*refs, dils, ts):
    ng = len(dils)
    o_refs, l_refs, expand_ref, out_ref = refs[:ng], refs[ng:2 * ng], refs[2 * ng], refs[2 * ng + 1]
    scratch = list(refs[2 * ng + 2:])

    def token_order(ref, dil):
        if dil == 1:
            return ref[0].astype(F32)
        buf = scratch.pop(0)
        n_chunks, _, lanes = buf.shape
        rows = ts // dil
        for r in range(dil):
            for c in range(n_chunks):
                buf[c, pl.ds(r, rows, stride=dil), :] = ref[r, :, c * lanes:(c + 1) * lanes].astype(F32)
        return jnp.concatenate([buf[c] for c in range(n_chunks)], axis=1)

    def spread(w):
        hi = w.astype(BF16)
        lo = (w - hi.astype(F32)).astype(BF16)
        return jnp.dot(hi, expand_ref[...], preferred_element_type=F32) \
            + jnp.dot(lo, expand_ref[...], preferred_element_type=F32)

    ls = [token_order(r, dil) for r, dil in zip(l_refs, dils)]
    m = functools.reduce(jnp.maximum, ls)
    es = [jnp.exp(l - m) for l in ls]
    den = functools.reduce(lambda a, c: a + c, es)
    num = None
    for e, r, dil in zip(es, o_refs, dils):
        term = spread(e / den) * token_order(r, dil)
        num = term if num is None else num + term
    out_ref[...] = num.astype(out_ref.dtype)


def _group_merge(outs, lses, dils, hd):
    b, _, _, aw = outs[0].shape
    s = outs[0].shape[1] * outs[0].shape[2]
    nh = aw // hd
    ts = _fit(s, 256, 16 * max(dils))
    spec = lambda dil, w: pl.BlockSpec((None, dil, ts // dil, w), lambda i, j: (i, 0, j, 0))
    n_perm = sum(1 for dil in dils if dil > 1)
    lane = np.arange(V7X_LANES)[:, None]
    expand = jnp.asarray((lane == (np.arange(aw)[None, :] // hd)).astype(np.float32), BF16)
    blk = len(dils) * (_nbytes((ts, aw), BF16) + _nbytes((ts, V7X_LANES), F32)) + _nbytes((ts, aw), BF16) \
        + _nbytes((V7X_LANES, aw), BF16)
    assert nh <= V7X_LANES
    return pl.pallas_call(
        functools.partial(_group_merge_kernel, dils=tuple(dils), ts=ts),
        grid=(b, s // ts),
        in_specs=[spec(dil, aw) for dil in dils] + [spec(dil, V7X_LANES) for dil in dils]
        + [pl.BlockSpec((V7X_LANES, aw), lambda i, j: (0, 0))],
        out_specs=pl.BlockSpec((None, ts, aw), lambda i, j: (i, j, 0)),
        out_shape=jax.ShapeDtypeStruct((b, s, aw), BF16),
        scratch_shapes=[pltpu.VMEM((1, ts, V7X_LANES), F32)] * n_perm
        + [pltpu.VMEM((aw // V7X_LANES, ts, V7X_LANES), F32)] * n_perm,
        compiler_params=_cparams(("arbitrary", "arbitrary"), blk, (n_perm + 4) * _nbytes((ts, aw), F32)),
        name="attn_group_merge")(*outs, *lses, expand)


def _row_copy(src_hbm, row, dst, slot, sem):
    return pltpu.make_async_copy(src_hbm.at[pl.ds(row, 1), :], dst.at[pl.ds(slot, 1), :], sem)


ROW_DMA_UNROLL = 8
DMA_PRIORITIES = 2


def _gather_kernel(src_ref, h_hbm, o_ref, buf, sem, *, rows):
    t = pl.program_id(0)
    slot = t % 2

    def fetch(step, dst_slot):
        def start(i, carry):
            for p in range(DMA_PRIORITIES):
                row = i * DMA_PRIORITIES + p
                _row_copy(h_hbm, src_ref[step * rows + row], buf.at[dst_slot], row,
                          sem.at[dst_slot]).start(priority=p)
            return carry
        lax.fori_loop(0, rows // DMA_PRIORITIES, start, 0, unroll=ROW_DMA_UNROLL // DMA_PRIORITIES)

    @pl.when(t == 0)
    def _first():
        fetch(0, 0)

    @pl.when(t + 1 < pl.num_programs(0))
    def _next():
        fetch(t + 1, 1 - slot)

    def wait(i, carry):
        _row_copy(h_hbm, 0, buf.at[slot], i, sem.at[slot]).wait()
        return carry

    lax.fori_loop(0, rows, wait, 0, unroll=ROW_DMA_UNROLL)
    o_ref[...] = buf[slot].astype(o_ref.dtype)


def _moe_gather(h2, src, rows):
    r_total = src.shape[0]
    d = h2.shape[1]
    return pl.pallas_call(
        functools.partial(_gather_kernel, rows=rows),
        grid_spec=pltpu.PrefetchScalarGridSpec(
            num_scalar_prefetch=1, grid=(r_total // rows,),
            in_specs=[pl.BlockSpec(memory_space=pl.ANY)],
            out_specs=pl.BlockSpec((rows, d), lambda t, src: (t, 0)),
            scratch_shapes=[pltpu.VMEM((2, rows, d), F32), pltpu.SemaphoreType.DMA((2,))]),
        out_shape=jax.ShapeDtypeStruct((r_total, d), BF16),
        compiler_params=_cparams(("arbitrary",), _nbytes((rows, d), BF16), 3 * _nbytes((rows, d), F32)),
        name="moe_gather",
    )(src, h2)


def _combine_kernel(pos_ref, y_hbm, x_ref, tw_ref, gt_ref, o_ref, buf, sem, *, rows):
    t = pl.program_id(0)
    slot = t % 2

    def fetch(step, dst_slot):
        def start(i, carry):
            for k in range(TOP_K):
                _row_copy(y_hbm, pos_ref[(step * rows + i) * TOP_K + k], buf.at[dst_slot, k], i,
                          sem.at[dst_slot]).start(priority=k % DMA_PRIORITIES)
            return carry
        lax.fori_loop(0, rows, start, 0, unroll=ROW_DMA_UNROLL // TOP_K)

    @pl.when(t == 0)
    def _first():
        fetch(0, 0)

    @pl.when(t + 1 < pl.num_programs(0))
    def _next():
        fetch(t + 1, 1 - slot)

    def wait(i, carry):
        for k in range(TOP_K):
            _row_copy(y_hbm, 0, buf.at[slot, k], i, sem.at[slot]).wait()
        return carry

    lax.fori_loop(0, rows, wait, 0, unroll=ROW_DMA_UNROLL // TOP_K)
    tw = tw_ref[...]
    y = tw[:, 0:1] * buf[slot, 0]
    for k in range(1, TOP_K):
        y = y + tw[:, k:k + 1] * buf[slot, k]
    o_ref[...] = x_ref[...] + gt_ref[...] * y


def _moe_combine(y_sorted, pos, x2, topw2, gate, rows_per_batch):
    m, d = x2.shape
    rows = _fit(rows_per_batch, 128, 8)
    blk = 2 * _nbytes((rows, d), F32) + _nbytes((rows, V7X_LANES), F32) + _nbytes((1, d), F32)
    return pl.pallas_call(
        functools.partial(_combine_kernel, rows=rows),
        grid_spec=pltpu.PrefetchScalarGridSpec(
            num_scalar_prefetch=1, grid=(m // rows,),
            in_specs=[pl.BlockSpec(memory_space=pl.ANY),
                      pl.BlockSpec((rows, d), lambda t, pos: (t, 0)),
                      pl.BlockSpec((rows, V7X_LANES), lambda t, pos: (t, 0)),
                      pl.BlockSpec((None, 1, d), lambda t, pos: ((t * rows) // rows_per_batch, 0, 0))],
            out_specs=pl.BlockSpec((rows, d), lambda t, pos: (t, 0)),
            scratch_shapes=[pltpu.VMEM((2, TOP_K, rows, d), F32), pltpu.SemaphoreType.DMA((2,))]),
        out_shape=jax.ShapeDtypeStruct((m, d), F32),
        compiler_params=_cparams(("arbitrary",), blk, (2 * TOP_K + 2) * _nbytes((rows, d), F32)),
        name="moe_combine",
    )(pos, y_sorted, x2, topw2, gate)


def _moe_schedule(route_i, counts, n_exp, bm, expert_base):
    m = route_i.shape[0]
    n_assign = m * TOP_K
    n_tiles = n_assign // bm + n_exp
    e = route_i[:, :TOP_K].reshape(n_assign)
    rank = route_i[:, TOP_K:2 * TOP_K].reshape(n_assign)
    onehot = (e[:, None] == jnp.arange(n_exp, dtype=I32)[None, :]).astype(I32)
    tiles_per = (counts + bm - 1) // bm
    tile_end = jnp.cumsum(tiles_per)
    row_start = (tile_end - tiles_per) * bm
    pos = jnp.sum(onehot * row_start[None, :], axis=1) + rank
    used = tile_end[-1]
    t = jnp.arange(n_tiles, dtype=I32)
    tr = jnp.minimum(t, used - 1)
    te = jnp.minimum(jnp.sum((tr[:, None] >= tile_end[None, :]).astype(I32), axis=1), n_exp - 1)
    tv = (t < used).astype(I32)
    prev_e = jnp.concatenate([jnp.full((1,), -1, I32), te[:-1]])
    tf = tv * ((te != prev_e).astype(I32))
    later_start = (t[None, :] > t[:, None]) & (tf[None, :] == 1)
    tn = jnp.min(jnp.where(later_start, t[None, :], n_tiles), axis=1)
    tn = jnp.where(tn == n_tiles, -1, tn)
    src = (jnp.arange(n_tiles * bm, dtype=I32) % m).at[pos].set(jnp.arange(n_assign, dtype=I32) // TOP_K)
    sched = (te.astype(I32) + expert_base, tf.astype(I32), tv, tr.astype(I32), tn.astype(I32))
    return sched, pos.astype(I32), src


def kernel(x, c, w_ada, b_ada, g_mix, g_ffn, w_in, conv_w, conv_b, conv_ln_g, conv_ln_b, w_conv_out,
           q_gain, k_gain, rel_bias, w_attn_out, w_out, ffn_w1, ffn_w3, ffn_w2, moe_router, moe_w1,
           moe_w3, moe_w2):
    b, s, d = x.shape
    m = b * s
    depth = w_ada.shape[0]
    in_w = w_in.shape[2]
    ch = conv_w.shape[2]
    hd = q_gain.shape[1]
    att_out_w = w_attn_out.shape[1]
    nh = att_out_w // hd
    ng = len(ATTN_GROUPS)
    att_w = rel_bias.shape[1] * hd
    q_off = 2 * ch
    gate_off = q_off + 3 * att_w
    assert in_w == gate_off + 2 * d and rel_bias.shape[1] == ng * nh

    bm = _fit(s, 1024)
    stack = lambda w: w.reshape((w.shape[0] * w.shape[1],) + w.shape[2:])
    bn = _fit(d, 512)
    aw = nh * hd
    dils = [dil for _, dil in ATTN_GROUPS]
    assert dils[0] == 1 and q_off % bn == 0 and aw % bn == 0 and q_off % aw == 0

    cb = lambda start, width, n=bn: list(range(start // n, (start + width) // n))
    qkv_cols = lambda g, n=bn: sum([cb(q_off + i * att_w + g * aw, aw, n) for i in range(3)], [])
    bn_in = bn
    nat_cols = cb(0, q_off, bn_in) + qkv_cols(0, bn_in) + cb(gate_off, 2 * d, bn_in)
    nat_gate = q_off + 3 * aw

    mod = _adaln(c, w_ada, b_ada)
    bias = _bias_tables(rel_bias, nh)

    for l in range(depth):
        sh1, sc1, gt1, sh2, sc2, gt2 = [mod[l, :, i * d:(i + 1) * d] for i in range(6)]
        gt1 = gt1.reshape(b, 1, d)
        gt2 = gt2.reshape(b, 1, d)

        h = _rmsmod(x, g_mix[l], sh1, sc1, BF16).reshape(m, d)
        sched = _dense_sched(m // bm, l)
        p = _ws_matmul("in_proj", _plain_compute, [(h, d, 0)], [(w_in, d, 0)], [],
                       m, len(nat_cols) * bn_in, BF16, bm, bn_in, sched, w_cols=nat_cols)
        p3 = p.reshape(b, s, len(nat_cols) * bn_in)
        cu = _conv_branch(p3, conv_w[l], conv_b[l], conv_ln_g[l], conv_ln_b[l]).reshape(m, ch)
        outs, lses = [], []
        for g, dil in enumerate(dils):
            if dil == 1:
                pa, qb = p3, q_off // aw
            else:
                pa = _ws_matmul(f"in_proj_g{g}", _plain_compute, [(h, d, 0)], [(w_in, d, 0)], [],
                                m, 3 * aw, BF16, bm, bn, sched, w_cols=qkv_cols(g), dil=dil,
                                rows_per_batch=s).reshape(b, s, 3 * aw)
                qb = 0
            o_g, lse_g = _attn_group(pa, g, dil, qb, qb + 1, qb + 2, nh, hd, q_gain[l], k_gain[l], bias[g])
            outs.append(o_g)
            lses.append(lse_g)
        o = _group_merge(outs, lses, dils, hd).reshape(m, att_out_w)
        merged = _ws_matmul(
            "branch_merge", _merge_compute, [(cu, ch, 0), (o, att_out_w, 0)],
            [(w_conv_out, ch, 0), (w_attn_out, att_out_w, 0)],
            [(p, _tile_extra(bm, bn, nat_gate // bn)), (p, _tile_extra(bm, bn, (nat_gate + d) // bn))],
            m, d, BF16, bm, bn, sched)
        x2 = _ws_matmul(
            "out_proj", _residual_compute, [(merged, d, 0)], [(w_out, d, 0)],
            [(x.reshape(m, d), _tile_extra(bm, bn, 0)), (gt1, _batch_row_extra(bm, bn, s))],
            m, d, F32, bm, bn, sched)
        x = x2.reshape(b, s, d)

        j = l // 2
        if l % 2 == 0:
            h = _rmsmod(x, g_ffn[l], sh2, sc2, BF16).reshape(m, d)
            d_ff = ffn_w1.shape[2]
            hid = _ws_matmul("ffn_up", _swiglu_compute, [(h, d, 0)],
                             [(ffn_w1, d, 0), (ffn_w3, d, 0)], [],
                             m, d_ff, BF16, bm, _fit(d_ff, 256), _dense_sched(m // bm, j))
            ksplit = 2 if d_ff % (2 * V7X_LANES) == 0 else 1
            kc = d_ff // ksplit
            bm2 = _fit(s, 1024)
            sched2 = _dense_sched(m // bm2, j)
            for kb in range(ksplit):
                x2 = _ws_matmul(
                    "ffn_down", _residual_compute, [(hid, kc, kb)], [(ffn_w2, kc, kb)],
                    [(x2, _tile_extra(bm2, bn, 0)), (gt2, _batch_row_extra(bm2, bn, s))],
                    m, d, F32, bm2, bn, sched2)
            x = x2.reshape(b, s, d)
        else:
            n_exp = moe_router.shape[2]
            d_fe = moe_w1.shape[3]
            hf, topw, topi, cnt = _rmsmod(x, g_ffn[l], sh2, sc2, F32, w_router=moe_router[j])
            bme = _fit(m, 512)
            msched, pos, src = _moe_schedule(topi.reshape(m, V7X_LANES), cnt[0, :n_exp].astype(I32), n_exp, bme,
                                             j * n_exp)
            n_rows = src.shape[0]
            xs = _moe_gather(hf.reshape(m, d), src, _fit(bme, 128, 16))
            hid = _ws_matmul("moe_up", _swiglu_compute, [(xs, d, 0)],
                             [(stack(moe_w1), d, 0), (stack(moe_w3), d, 0)], [],
                             n_rows, d_fe, BF16, bme, _fit(d_fe, 512), msched)
            ys = _ws_matmul("moe_down", _plain_compute, [(hid, d_fe, 0)], [(stack(moe_w2), d_fe, 0)], [],
                            n_rows, d, F32, bme, bn, msched)
            x = _moe_combine(ys, pos, x2, topw.reshape(m, V7X_LANES), gt2, s).reshape(b, s, d)
    return x
```

```python
import functools
import math

import numpy as np
import jax
import jax.numpy as jnp
from jax import lax
from jax.experimental import pallas as pl
from jax.experimental.pallas import tpu as pltpu

F32 = jnp.float32
BF16 = jnp.bfloat16
I32 = jnp.int32

ATTN_GROUPS = ((128, 1), (512, 4), (2048, 16))
ATTN_BLOCK = 128
MAX_DISTANCE = 2048
TOP_K = 2

V7X_VMEM_BYTES = 64 * 1024 * 1024
V7X_LANES = 128
V7X_SUBLANES = 8
VMEM_REQUEST_CAP = V7X_VMEM_BYTES - 6 * 1024 * 1024
VMEM_SLACK = 8 * 1024 * 1024
NEG_INF = float("-inf")


def _fit(n, pref, unit=V7X_LANES):
    best = None
    for d in range(unit, min(n, pref) + 1, unit):
        if n % d == 0:
            best = d
    return best if best is not None else n


def _nbytes(shape, dtype):
    return int(np.prod(shape)) * jnp.dtype(dtype).itemsize


def _cparams(semantics, block_bytes, scratch_bytes=0):
    limit = min(VMEM_REQUEST_CAP, 2 * block_bytes + scratch_bytes + VMEM_SLACK)
    return pltpu.CompilerParams(dimension_semantics=semantics, vmem_limit_bytes=int(limit))


def _ada_kernel(c_ref, w_ref, b_ref, o_ref):
    c = c_ref[...]
    ca = (c * jax.nn.sigmoid(c)).astype(BF16)
    o_ref[...] = jnp.dot(ca, w_ref[...].astype(BF16), preferred_element_type=F32) + b_ref[...]


def _adaln(c, w_ada, b_ada):
    depth, d, n = w_ada.shape
    b = c.shape[0]
    rows = 8
    c8 = jnp.pad(c, ((0, rows - b), (0, 0)))
    bn = _fit(n, 1024)
    blk = _nbytes((rows, d), F32) + _nbytes((d, bn), F32) + 2 * _nbytes((rows, bn), F32)
    out = pl.pallas_call(
        _ada_kernel,
        grid=(depth, n // bn),
        in_specs=[pl.BlockSpec((rows, d), lambda l, j: (0, 0)),
                  pl.BlockSpec((None, d, bn), lambda l, j: (l, 0, j)),
                  pl.BlockSpec((None, 1, bn), lambda l, j: (l, 0, j))],
        out_specs=pl.BlockSpec((None, rows, bn), lambda l, j: (l, 0, j)),
        out_shape=jax.ShapeDtypeStruct((depth, rows, n), F32),
        compiler_params=_cparams(("arbitrary", "arbitrary"), blk, _nbytes((d, bn), BF16)),
        name="adaln",
    )(c8, w_ada, b_ada.reshape(depth, 1, n))
    return out[:, :b]


NORM_ROWS = 16
NORM_COLS = 512


def _normed_into(x_ref, g_ref, sh_ref, sc_ref, o_ref):
    ts, d = x_ref.shape
    rows, cols = min(NORM_ROWS, ts), min(NORM_COLS, d)
    for r0 in range(0, ts, rows):
        sq = jnp.zeros((rows, cols), F32)
        for c0 in range(0, d, cols):
            x = x_ref[r0:r0 + rows, c0:c0 + cols]
            sq = sq + x * x
        inv = lax.rsqrt(jnp.sum(sq, axis=-1, keepdims=True) * (1.0 / d) + 1e-6)
        for c0 in range(0, d, cols):
            cs = slice(c0, c0 + cols)
            y = x_ref[r0:r0 + rows, cs] * inv * g_ref[:, cs]
            o_ref[r0:r0 + rows, cs] = (y * (1.0 + sc_ref[:, cs]) + sh_ref[:, cs]).astype(o_ref.dtype)


def _rmsmod_kernel(x_ref, g_ref, sh_ref, sc_ref, o_ref):
    _normed_into(x_ref, g_ref, sh_ref, sc_ref, o_ref)


def _rmsmod_route_kernel(x_ref, g_ref, sh_ref, sc_ref, whi_ref, wlo_ref, h_ref, tw_ref, ti_ref, cnt_ref,
                         *, n_exp):
    _normed_into(x_ref, g_ref, sh_ref, sc_ref, h_ref)
    h = h_ref[...]
    h_hi = h.astype(BF16)
    h_lo = (h - h_hi.astype(F32)).astype(BF16)
    logits = jnp.dot(h_hi, whi_ref[...], preferred_element_type=F32) \
        + (jnp.dot(h_lo, whi_ref[...], preferred_element_type=F32)
           + jnp.dot(h_hi, wlo_ref[...], preferred_element_type=F32))
    lane = lax.broadcasted_iota(I32, logits.shape, 1)
    lanef = lane.astype(F32)
    sentinel = float(V7X_LANES)
    lg = jnp.where(lane < n_exp, logits, NEG_INF)
    m1 = jnp.max(lg, axis=-1, keepdims=True)
    i1 = jnp.min(jnp.where(lg == m1, lanef, sentinel), axis=-1, keepdims=True)
    lg2 = jnp.where(lanef == i1, NEG_INF, lg)
    m2 = jnp.max(lg2, axis=-1, keepdims=True)
    i2 = jnp.min(jnp.where(lg2 == m2, lanef, sentinel), axis=-1, keepdims=True)
    e2 = jnp.exp(m2 - m1)
    den = 1.0 + e2
    tw_ref[...] = jnp.where(lane == 0, 1.0 / den, jnp.where(lane == 1, e2 / den, 0.0))

    @pl.when((pl.program_id(0) == 0) & (pl.program_id(1) == 0))
    def _init_counts():
        cnt_ref[...] = jnp.zeros(cnt_ref.shape, F32)

    oh1 = (lanef == i1).astype(F32)
    oh2 = (lanef == i2).astype(F32)
    oh = oh1 + oh2
    ts = oh.shape[0]
    earlier = (lax.broadcasted_iota(I32, (ts, ts), 1) < lax.broadcasted_iota(I32, (ts, ts), 0))
    before = cnt_ref[0:1, :] + jnp.dot(earlier.astype(BF16), oh.astype(BF16), preferred_element_type=F32)
    r1 = jnp.sum(oh1 * before, axis=-1, keepdims=True)
    r2 = jnp.sum(oh2 * before, axis=-1, keepdims=True)
    cnt_ref[...] = cnt_ref[...] + jnp.sum(oh, axis=0, keepdims=True)
    ti_ref[...] = jnp.where(lane == 0, i1, jnp.where(lane == 1, i2, jnp.where(
        lane == 2, r1, jnp.where(lane == 3, r2, 0.0)))).astype(I32)


def _rmsmod(x, g, shift, scale, out_dtype, w_router=None):
    b, s, d = x.shape
    ts = _fit(s, 256, 8)
    specs = [pl.BlockSpec((None, ts, d), lambda i, j: (i, j, 0)),
             pl.BlockSpec((1, d), lambda i, j: (0, 0)),
             pl.BlockSpec((None, 1, d), lambda i, j: (i, 0, 0)),
             pl.BlockSpec((None, 1, d), lambda i, j: (i, 0, 0))]
    args = [x, g.reshape(1, d), shift.reshape(b, 1, d), scale.reshape(b, 1, d)]
    tile = pl.BlockSpec((None, ts, d), lambda i, j: (i, j, 0))
    blk = _nbytes((ts, d), F32) * 2 + 3 * _nbytes((1, d), F32)
    if w_router is None:
        return pl.pallas_call(
            _rmsmod_kernel, grid=(b, s // ts), in_specs=specs, out_specs=tile,
            out_shape=jax.ShapeDtypeStruct((b, s, d), out_dtype),
            compiler_params=_cparams(("arbitrary", "arbitrary"), blk, 4 * _nbytes((ts, d), F32)),
            name="rmsmod")(*args)
    n_exp = w_router.shape[1]
    wr = jnp.pad(w_router, ((0, 0), (0, V7X_LANES - n_exp)))
    wr_hi = wr.astype(BF16)
    wr_lo = (wr - wr_hi.astype(F32)).astype(BF16)
    lanes = pl.BlockSpec((None, ts, V7X_LANES), lambda i, j: (i, j, 0))
    blk += _nbytes((d, V7X_LANES), F32) + 2 * _nbytes((ts, V7X_LANES), F32)
    return pl.pallas_call(
        functools.partial(_rmsmod_route_kernel, n_exp=n_exp),
        grid=(b, s // ts),
        in_specs=specs + [pl.BlockSpec((d, V7X_LANES), lambda i, j: (0, 0))] * 2,
        out_specs=[tile, lanes, lanes, pl.BlockSpec((8, V7X_LANES), lambda i, j: (0, 0))],
        out_shape=[jax.ShapeDtypeStruct((b, s, d), F32),
                   jax.ShapeDtypeStruct((b, s, V7X_LANES), F32),
                   jax.ShapeDtypeStruct((b, s, V7X_LANES), I32),
                   jax.ShapeDtypeStruct((8, V7X_LANES), F32)],
        compiler_params=_cparams(("arbitrary", "arbitrary"), blk, 8 * _nbytes((ts, d), F32)),
        name="rmsmod_route")(*args, wr_hi, wr_lo)


def _ws_kernel(te_ref, tf_ref, tv_ref, tr_ref, tn_ref, wc_ref, *refs, nx, nw, ne, compute, dil, w_blocks, bn):
    del tr_ref
    x_refs = refs[:nx]
    w_refs = refs[nx:nx + nw]
    e_refs = refs[nx + nw:nx + nw + ne]
    o_ref = refs[nx + nw + ne]
    rest = refs[nx + nw + ne + 1:]
    wb_refs, stage_refs, sem = rest[:nw], rest[nw:2 * nw], rest[2 * nw]
    j = pl.program_id(0)
    t = pl.program_id(1)

    def weight_copy(i, expert, col_tile):
        k, kb = w_blocks[i]
        col = pl.multiple_of(wc_ref[col_tile] * bn, bn)
        return pltpu.make_async_copy(w_refs[i].at[expert, pl.ds(kb * k, k), pl.ds(col, bn)],
                                     stage_refs[i], sem.at[i])

    @pl.when(tf_ref[t] == 1)
    def _new_weights():
        @pl.when((j == 0) & (t == 0))
        def _first_block():
            for i in range(nw):
                weight_copy(i, te_ref[0], 0).start()

        for i in range(nw):
            weight_copy(i, te_ref[t], j).wait()
            wb_refs[i][...] = stage_refs[i][...].astype(BF16)

        nxt = tn_ref[t]

        @pl.when(nxt >= 0)
        def _next_run():
            for i in range(nw):
                weight_copy(i, te_ref[jnp.maximum(nxt, 0)], j).start()

        @pl.when((nxt < 0) & (j + 1 < pl.num_programs(0)))
        def _next_column_tile():
            for i in range(nw):
                weight_copy(i, te_ref[0], j + 1).start()

    @pl.when(tv_ref[t] == 1)
    def _compute():
        res = compute(x_refs, wb_refs, e_refs)
        if dil == 1:
            o_ref[...] = res.astype(o_ref.dtype)
        else:
            perm_ref = rest[-1]
            n_chunks, bm, lanes = perm_ref.shape
            rows = bm // dil
            for c in range(n_chunks):
                perm_ref[c] = res[:, c * lanes:(c + 1) * lanes]
            for r in range(dil):
                for c in range(n_chunks):
                    o_ref[r, :, c * lanes:(c + 1) * lanes] = \
                        perm_ref[c, pl.ds(r, rows, stride=dil), :].astype(o_ref.dtype)

    @pl.when(tv_ref[t] == 0)
    def _unused_tile():
        o_ref[...] = jnp.zeros(o_ref.shape, o_ref.dtype)


def _dense_sched(n_tiles, expert=0):
    first = np.zeros((n_tiles,), np.int32)
    first[0] = 1
    return (jnp.full((n_tiles,), expert, I32), jnp.asarray(first), jnp.ones((n_tiles,), I32),
            jnp.arange(n_tiles, dtype=I32), jnp.full((n_tiles,), -1, I32))


def _tile_extra(bm, bn, col_off):
    return pl.BlockSpec((bm, bn), lambda j, t, te, tf, tv, tr, *_: (tr[t], col_off + j))


def _batch_row_extra(bm, bn, rows_per_batch):
    return pl.BlockSpec((None, 1, bn), lambda j, t, te, tf, tv, tr, *_: ((tr[t] * bm) // rows_per_batch, 0, j))


def _ws_matmul(name, compute, xs, ws, extras, out_rows, out_cols, out_dtype, bm, bn, sched,
               w_cols=None, dil=1, rows_per_batch=None):
    n_tiles = sched[0].shape[0]
    nj = out_cols // bn
    if w_cols is None:
        w_cols = np.arange(nj)
    assert len(w_cols) == nj
    w_cols = jnp.asarray(np.asarray(w_cols, np.int32))
    in_specs, args, blk = [], [], 0
    for arr, k, kb in xs:
        in_specs.append(pl.BlockSpec((bm, k), lambda j, t, te, tf, tv, tr, *_, kb=kb: (tr[t], kb)))
        args.append(arr)
        blk += _nbytes((bm, k), arr.dtype)
    wb, stage, scratch_bytes = [], [], 0
    for arr, k, kb in ws:
        in_specs.append(pl.BlockSpec(memory_space=pl.ANY))
        args.append(arr)
        wb.append(pltpu.VMEM((k, bn), BF16))
        stage.append(pltpu.VMEM((k, bn), arr.dtype))
        scratch_bytes += _nbytes((k, bn), BF16) + _nbytes((k, bn), arr.dtype)
    scratch = wb + stage + [pltpu.SemaphoreType.DMA((len(ws),))]
    for arr, spec in extras:
        in_specs.append(spec)
        args.append(arr)
        blk += _nbytes([d for d in spec.block_shape if d is not None], arr.dtype)
    blk += _nbytes((bm, bn), out_dtype)
    scratch_bytes += 2 * _nbytes((bm, bn), F32)
    if dil == 1:
        out_spec = pl.BlockSpec((bm, bn), lambda j, t, *_: (t, j))
        out_shape = jax.ShapeDtypeStruct((out_rows, out_cols), out_dtype)
    else:
        tpb = rows_per_batch // bm
        assert rows_per_batch % bm == 0 and bm % dil == 0
        out_spec = pl.BlockSpec((None, dil, bm // dil, bn), lambda j, t, *_: (t // tpb, 0, t % tpb, j))
        out_shape = jax.ShapeDtypeStruct((out_rows // rows_per_batch, dil, rows_per_batch // dil, out_cols),
                                         out_dtype)
        assert bn % V7X_LANES == 0
        scratch.append(pltpu.VMEM((bn // V7X_LANES, bm, V7X_LANES), F32))
        scratch_bytes += _nbytes((bm, bn), F32)
    kern = functools.partial(_ws_kernel, nx=len(xs), nw=len(ws), ne=len(extras), compute=compute, dil=dil,
                             w_blocks=tuple((k, kb) for _, k, kb in ws), bn=bn)
    return pl.pallas_call(
        kern,
        grid_spec=pltpu.PrefetchScalarGridSpec(
            num_scalar_prefetch=6, grid=(nj, n_tiles), in_specs=in_specs,
            out_specs=out_spec, scratch_shapes=scratch),
        out_shape=out_shape,
        compiler_params=_cparams(("arbitrary", "arbitrary"), blk, scratch_bytes),
        name=name,
    )(*sched, w_cols, *args)


def _mm(x, wb_ref):
    return jnp.dot(x, wb_ref[...], preferred_element_type=F32)


def _plain_compute(xr, wr, er):
    return _mm(xr[0][...], wr[0])


def _swiglu_compute(xr, wr, er):
    x = xr[0][...]
    a = _mm(x, wr[0])
    return a * jax.nn.sigmoid(a) * _mm(x, wr[1])


def _merge_compute(xr, wr, er):
    conv_out = _mm(xr[0][...], wr[0])
    attn_out = _mm(xr[1][...], wr[1])
    gc = er[0][...].astype(F32)
    ga = er[1][...].astype(F32)
    return jax.nn.sigmoid(gc) * conv_out + jax.nn.sigmoid(ga) * attn_out


def _residual_compute(xr, wr, er):
    return er[0][...] + er[1][...] * _mm(xr[0][...], wr[0])


CONV_HALO = 32


def _conv_kernel(a_ref, b_ref, ah_ref, bh_ref, w_ref, cb_ref, g_ref, beta_ref, o_ref, ubuf, sbuf, cbuf,
                 *, ts, width, ch):
    i = pl.program_id(1)
    a = a_ref[...].astype(F32)
    b = b_ref[...].astype(F32)
    ubuf[CONV_HALO:CONV_HALO + ts, :] = a * jax.nn.sigmoid(b)
    ah = ah_ref[...].astype(F32)
    bh = bh_ref[...].astype(F32)
    ubuf[0:CONV_HALO, :] = jnp.where(i > 0, ah * jax.nn.sigmoid(bh), 0.0)
    span = sbuf.shape[1]
    for s in range(1, V7X_SUBLANES):
        sbuf[s - 1] = ubuf[s:s + span, :]
    off = CONV_HALO - (width - 1)
    rch = V7X_SUBLANES
    for r0 in range(0, ts, rch):
        acc = jnp.zeros((rch, ch), F32)
        for j in range(width):
            shift, row = (off + j) % V7X_SUBLANES, r0 + (off + j) // V7X_SUBLANES * V7X_SUBLANES
            src = ubuf if shift == 0 else sbuf.at[shift - 1]
            acc = acc + w_ref[j] * src[row:row + rch, :]
        cbuf[r0:r0 + rch, :] = acc + cb_ref[...]
    y = cbuf[...]
    mu = jnp.mean(y, axis=-1, keepdims=True)
    yc = y - mu
    var = jnp.mean(yc * yc, axis=-1, keepdims=True)
    yn = yc * lax.rsqrt(var + 1e-5) * g_ref[...] + beta_ref[...]
    o_ref[...] = (yn * jax.nn.sigmoid(yn)).astype(o_ref.dtype)


def _conv_branch(p3, conv_w, conv_b, ln_g, ln_b):
    b, s, _ = p3.shape
    width, ch = conv_w.shape
    assert width - 1 <= CONV_HALO
    ts = _fit(s, 128, CONV_HALO)
    hb = ts // CONV_HALO
    tile = lambda col: pl.BlockSpec((None, ts, ch), lambda bi, i, col=col: (bi, i, col))
    halo = lambda col: pl.BlockSpec((None, CONV_HALO, ch),
                                    lambda bi, i, col=col: (bi, jnp.maximum(i * hb - 1, 0), col))
    row = pl.BlockSpec((1, ch), lambda bi, i: (0, 0))
    blk = 2 * _nbytes((ts + CONV_HALO, ch), BF16) + _nbytes((width * V7X_SUBLANES + 3, ch), F32) + _nbytes((ts, ch), BF16)
    shifted = (V7X_SUBLANES - 1, ts + CONV_HALO - V7X_SUBLANES, ch)
    scratch_bytes = _nbytes((2 * ts + CONV_HALO, ch), F32) + _nbytes(shifted, F32)
    return pl.pallas_call(
        functools.partial(_conv_kernel, ts=ts, width=width, ch=ch),
        grid=(b, s // ts),
        in_specs=[tile(0), tile(1), halo(0), halo(1),
                  pl.BlockSpec((width, V7X_SUBLANES, ch), lambda bi, i: (0, 0, 0)), row, row, row],
        out_specs=pl.BlockSpec((None, ts, ch), lambda bi, i: (bi, i, 0)),
        out_shape=jax.ShapeDtypeStruct((b, s, ch), BF16),
        scratch_shapes=[pltpu.VMEM((ts + CONV_HALO, ch), F32), pltpu.VMEM(shifted, F32),
                        pltpu.VMEM((ts, ch), F32)],
        compiler_params=_cparams(("arbitrary", "arbitrary"), blk, scratch_bytes + 4 * _nbytes((ts, ch), F32)),
        name="conv_module",
    )(p3, p3, p3, p3, jnp.broadcast_to(conv_w[:, None, :], (width, V7X_SUBLANES, ch)),
      conv_b.reshape(1, ch), ln_g.reshape(1, ch), ln_b.reshape(1, ch))


def _bias_kernel(rel_ref, idx_ref, o_ref, *, n_buckets, nh, span, blk):
    g = pl.program_id(0)
    h = pl.program_id(1)
    idx = idx_ref[...]
    acc = jnp.zeros(idx.shape, F32)
    for bkt in range(n_buckets):
        acc = jnp.where(idx == bkt, rel_ref[bkt, g * nh + h], acc)
    qi = lax.broadcasted_iota(I32, idx.shape, 0) + blk
    kj = lax.broadcasted_iota(I32, idx.shape, 1)
    delta = qi - kj
    o_ref[...] = jnp.where((delta >= 0) & (delta <= span), acc, NEG_INF)


def _t5_bucket(dist, n_buckets):
    max_exact = n_buckets // 2
    n = jnp.maximum(dist, 1).astype(F32)
    large = max_exact + (jnp.log(n / max_exact) / math.log(MAX_DISTANCE / max_exact)
                         * (n_buckets - max_exact)).astype(I32)
    large = jnp.minimum(large, n_buckets - 1)
    return jnp.where(dist < max_exact, dist, large)


def _bias_tables(rel_bias, nh):
    n_buckets = rel_bias.shape[0]
    blk = ATTN_BLOCK
    span = ATTN_GROUPS[0][0] // ATTN_GROUPS[0][1]
    assert all(w // d == span for w, d in ATTN_GROUPS)
    delta = (jnp.arange(blk, dtype=I32)[:, None] + blk) - jnp.arange(2 * blk, dtype=I32)[None, :]
    idx = jnp.stack([_t5_bucket(jnp.maximum(delta, 0) * d, n_buckets) for _, d in ATTN_GROUPS])
    ng = len(ATTN_GROUPS)
    return pl.pallas_call(
        functools.partial(_bias_kernel, n_buckets=n_buckets, nh=nh, span=span, blk=blk),
        grid=(ng, nh),
        in_specs=[pl.BlockSpec(memory_space=pltpu.SMEM),
                  pl.BlockSpec((None, blk, 2 * blk), lambda g, h: (g, 0, 0))],
        out_specs=pl.BlockSpec((None, None, blk, 2 * blk), lambda g, h: (g, h, 0, 0)),
        out_shape=jax.ShapeDtypeStruct((ng, nh, blk, 2 * blk), F32),
        name="rel_bias_tables",
    )(rel_bias, idx)


ATTN_HEAD_PAIR = 2


def _attn_kernel(q_ref, k_ref, v_ref, qg_ref, kg_ref, bias_ref, seg_ref, o_ref, lse_ref, kprev, vprev,
                 qn_s, kn_s, s_s, p_s, m_s, *, nh, hd, blk):
    has_prev = pl.program_id(2) > 0
    scale = hd ** -0.5
    nt = (((1,), (1,)), ((), ()))
    pw = ATTN_HEAD_PAIR * hd
    seg = seg_ref[...]
    ones = jnp.ones((blk, hd), BF16)

    @pl.when(jnp.logical_not(has_prev))
    def _no_previous_block():
        kprev[...] = jnp.zeros(kprev.shape, kprev.dtype)
        vprev[...] = jnp.zeros(vprev.shape, vprev.dtype)

    def qk_norm(ref, c0, gain):
        x = ref[:, c0:c0 + pw].astype(F32)
        sq = x * x
        hi = sq.astype(BF16)
        lo = (sq - hi.astype(F32)).astype(BF16)
        ss = jnp.dot(hi, seg, preferred_element_type=F32) + jnp.dot(lo, seg, preferred_element_type=F32)
        return (x * lax.rsqrt(ss * (1.0 / hd) + 1e-6) * gain).astype(BF16)

    for c0 in range(0, nh * hd, pw):
        qn_s[:, c0:c0 + pw] = qk_norm(q_ref, c0, qg_ref[...])
        kn_s[:, c0:c0 + pw] = qk_norm(k_ref, c0, kg_ref[...])
    for h in range(nh):
        hs = slice(h * hd, (h + 1) * hd)
        bias = bias_ref[h]
        q = qn_s[:, hs]
        s_s[h, :, :blk] = lax.dot_general(q, kprev[:, hs], nt, preferred_element_type=F32) * scale \
            + jnp.where(has_prev, bias[:, :blk], NEG_INF)
        s_s[h, :, blk:] = lax.dot_general(q, kn_s[:, hs], nt, preferred_element_type=F32) * scale + bias[:, blk:]
    for h in range(nh):
        s = s_s[h]
        m = jnp.max(s, axis=-1, keepdims=True)
        m_s[h] = jnp.broadcast_to(m, (blk, hd))
        p_s[h] = jnp.exp(s - m).astype(BF16)
    assert hd == V7X_LANES and nh <= V7X_LANES
    head_lane = lax.broadcasted_iota(I32, (blk, hd), 1)
    lse = jnp.zeros((blk, hd), F32)
    for h in range(nh):
        hs = slice(h * hd, (h + 1) * hd)
        res = jnp.dot(p_s[h, :, :blk], jnp.concatenate([vprev[:, hs], ones], axis=1),
                      preferred_element_type=F32) \
            + jnp.dot(p_s[h, :, blk:], jnp.concatenate([v_ref[:, hs], ones], axis=1),
                      preferred_element_type=F32)
        l = res[:, hd:]
        o_ref[:, hs] = (res[:, :hd] / l).astype(o_ref.dtype)
        lse = jnp.where(head_lane == h, m_s[h] + jnp.log(l), lse)
    lse_ref[...] = lse
    kprev[...] = kn_s[...]
    vprev[...] = v_ref[...]


def _attn_group(pa, g, dil, qb, kb, vb, nh, hd, q_gain, k_gain, bias_g):
    b, s, _ = pa.shape
    blk = ATTN_BLOCK
    seq = s // dil
    assert s % dil == 0 and seq % blk == 0
    nb = seq // blk
    aw = nh * hd
    assert nh % ATTN_HEAD_PAIR == 0
    pw = ATTN_HEAD_PAIR * hd
    own = lambda c: pl.BlockSpec((None, blk, aw), lambda bi, r, n, c=c: (bi, r * nb + n, c))
    gain = pl.BlockSpec((1, pw), lambda bi, r, n: (0, 0))
    out_spec = pl.BlockSpec((None, blk, aw), lambda bi, r, n: (bi, r * nb + n, 0))
    lane_head = np.arange(pw) // hd
    seg = jnp.asarray((lane_head[:, None] == lane_head[None, :]).astype(np.float32), BF16)
    tile_gain = lambda gn: jnp.tile(gn, ATTN_HEAD_PAIR).reshape(1, pw)
    blk_bytes = 3 * _nbytes((blk, aw), BF16) + _nbytes((nh, blk, 2 * blk), F32) \
        + _nbytes((blk, aw), BF16) + _nbytes((blk, aw), F32) + _nbytes((pw, pw), BF16)
    o, lse = pl.pallas_call(
        functools.partial(_attn_kernel, nh=nh, hd=hd, blk=blk),
        grid=(b, dil, nb),
        in_specs=[own(qb), own(kb), own(vb), gain, gain,
                  pl.BlockSpec((nh, blk, 2 * blk), lambda bi, r, n: (0, 0, 0)),
                  pl.BlockSpec((pw, pw), lambda bi, r, n: (0, 0))],
        out_specs=[out_spec, pl.BlockSpec((None, blk, V7X_LANES), lambda bi, r, n: (bi, r * nb + n, 0))],
        out_shape=[jax.ShapeDtypeStruct((b, s, aw), BF16), jax.ShapeDtypeStruct((b, s, V7X_LANES), F32)],
        scratch_shapes=[pltpu.VMEM((blk, aw), BF16), pltpu.VMEM((blk, aw), BF16),
                        pltpu.VMEM((blk, aw), BF16), pltpu.VMEM((blk, aw), BF16),
                        pltpu.VMEM((nh, blk, 2 * blk), F32), pltpu.VMEM((nh, blk, 2 * blk), BF16),
                        pltpu.VMEM((nh, blk, hd), F32)],
        compiler_params=_cparams(("arbitrary", "arbitrary", "arbitrary"), blk_bytes,
                                 4 * _nbytes((blk, aw), BF16) + 2 * _nbytes((nh, blk, 2 * blk), F32)),
        name=f"dilated_attn_g{g}",
    )(pa, pa, pa, tile_gain(q_gain), tile_gain(k_gain), bias_g, seg)
    return o.reshape(b, dil, seq, aw), lse.reshape(b, dil, seq, V7X_LANES)


def _group_merge_kernel(*refs, dils, ts):
    ng = len(dils)
    o_refs, l_refs, expand_ref, out_ref = refs[:ng], refs[ng:2 * ng], refs[2 * ng], refs[2 * ng + 1]
    scratch = list(refs[2 * ng + 2:])

    def token_order(ref, dil):
        if dil == 1:
            return ref[0].astype(F32)
        buf = scratch.pop(0)
        n_chunks, _, lanes = buf.shape
        rows = ts // dil
        for r in range(dil):
            for c in range(n_chunks):
                buf[c, pl.ds(r, rows, stride=dil), :] = ref[r, :, c * lanes:(c + 1) * lanes].astype(F32)
        return jnp.concatenate([buf[c] for c in range(n_chunks)], axis=1)

    def spread(w):
        hi = w.astype(BF16)
        lo = (w - hi.astype(F32)).astype(BF16)
        return jnp.dot(hi, expand_ref[...], preferred_element_type=F32) \
            + jnp.dot(lo, expand_ref[...], preferred_element_type=F32)

    ls = [token_order(r, dil) for r, dil in zip(l_refs, dils)]
    m = functools.reduce(jnp.maximum, ls)
    es = [jnp.exp(l - m) for l in ls]
    den = functools.reduce(lambda a, c: a + c, es)
    num = None
    for e, r, dil in zip(es, o_refs, dils):
        term = spread(e / den) * token_order(r, dil)
        num = term if num is None else num + term
    out_ref[...] = num.astype(out_ref.dtype)


def _group_merge(outs, lses, dils, hd):
    b, _, _, aw = outs[0].shape
    s = outs[0].shape[1] * outs[0].shape[2]
    nh = aw // hd
    ts = _fit(s, 256, 16 * max(dils))
    spec = lambda dil, w: pl.BlockSpec((None, dil, ts // dil, w), lambda i, j: (i, 0, j, 0))
    n_perm = sum(1 for dil in dils if dil > 1)
    lane = np.arange(V7X_LANES)[:, None]
    expand = jnp.asarray((lane == (np.arange(aw)[None, :] // hd)).astype(np.float32), BF16)
    blk = len(dils) * (_nbytes((ts, aw), BF16) + _nbytes((ts, V7X_LANES), F32)) + _nbytes((ts, aw), BF16) \
        + _nbytes((V7X_LANES, aw), BF16)
    assert nh <= V7X_LANES
    return pl.pallas_call(
        functools.partial(_group_merge_kernel, dils=tuple(dils), ts=ts),
        grid=(b, s // ts),
        in_specs=[spec(dil, aw) for dil in dils] + [spec(dil, V7X_LANES) for dil in dils]
        + [pl.BlockSpec((V7X_LANES, aw), lambda i, j: (0, 0))],
        out_specs=pl.BlockSpec((None, ts, aw), lambda i, j: (i, j, 0)),
        out_shape=jax.ShapeDtypeStruct((b, s, aw), BF16),
        scratch_shapes=[pltpu.VMEM((1, ts, V7X_LANES), F32)] * n_perm
        + [pltpu.VMEM((aw // V7X_LANES, ts, V7X_LANES), F32)] * n_perm,
        compiler_params=_cparams(("arbitrary", "arbitrary"), blk, (n_perm + 4) * _nbytes((ts, aw), F32)),
        name="attn_group_merge")(*outs, *lses, expand)


def _row_copy(src_hbm, row, dst, slot, sem):
    return pltpu.make_async_copy(src_hbm.at[pl.ds(row, 1), :], dst.at[pl.ds(slot, 1), :], sem)


ROW_DMA_UNROLL = 8
DMA_PRIORITIES = 2


def _gather_kernel(src_ref, h_hbm, o_ref, buf, sem, *, rows):
    t = pl.program_id(0)
    slot = t % 2

    def fetch(step, dst_slot):
        def start(i, carry):
            for p in range(DMA_PRIORITIES):
                row = i * DMA_PRIORITIES + p
                _row_copy(h_hbm, src_ref[step * rows + row], buf.at[dst_slot], row,
                          sem.at[dst_slot]).start(priority=p)
            return carry
        lax.fori_loop(0, rows // DMA_PRIORITIES, start, 0, unroll=ROW_DMA_UNROLL // DMA_PRIORITIES)

    @pl.when(t == 0)
    def _first():
        fetch(0, 0)

    @pl.when(t + 1 < pl.num_programs(0))
    def _next():
        fetch(t + 1, 1 - slot)

    def wait(i, carry):
        _row_copy(h_hbm, 0, buf.at[slot], i, sem.at[slot]).wait()
        return carry

    lax.fori_loop(0, rows, wait, 0, unroll=ROW_DMA_UNROLL)
    o_ref[...] = buf[slot].astype(o_ref.dtype)


def _moe_gather(h2, src, rows):
    r_total = src.shape[0]
    d = h2.shape[1]
    return pl.pallas_call(
        functools.partial(_gather_kernel, rows=rows),
        grid_spec=pltpu.PrefetchScalarGridSpec(
            num_scalar_prefetch=1, grid=(r_total // rows,),
            in_specs=[pl.BlockSpec(memory_space=pl.ANY)],
            out_specs=pl.BlockSpec((rows, d), lambda t, src: (t, 0)),
            scratch_shapes=[pltpu.VMEM((2, rows, d), F32), pltpu.SemaphoreType.DMA((2,))]),
        out_shape=jax.ShapeDtypeStruct((r_total, d), BF16),
        compiler_params=_cparams(("arbitrary",), _nbytes((rows, d), BF16), 3 * _nbytes((rows, d), F32)),
        name="moe_gather",
    )(src, h2)


def _combine_kernel(pos_ref, y_hbm, x_ref, tw_ref, gt_ref, o_ref, buf, sem, *, rows):
    t = pl.program_id(0)
    slot = t % 2

    def fetch(step, dst_slot):
        def start(i, carry):
            for k in range(TOP_K):
                _row_copy(y_hbm, pos_ref[(step * rows + i) * TOP_K + k], buf.at[dst_slot, k], i,
                          sem.at[dst_slot]).start(priority=k % DMA_PRIORITIES)
            return carry
        lax.fori_loop(0, rows, start, 0, unroll=ROW_DMA_UNROLL // TOP_K)

    @pl.when(t == 0)
    def _first():
        fetch(0, 0)

    @pl.when(t + 1 < pl.num_programs(0))
    def _next():
        fetch(t + 1, 1 - slot)

    def wait(i, carry):
        for k in range(TOP_K):
            _row_copy(y_hbm, 0, buf.at[slot, k], i, sem.at[slot]).wait()
        return carry

    lax.fori_loop(0, rows, wait, 0, unroll=ROW_DMA_UNROLL // TOP_K)
    tw = tw_ref[...]
    y = tw[:, 0:1] * buf[slot, 0]
    for k in range(1, TOP_K):
        y = y + tw[:, k:k + 1] * buf[slot, k]
    o_ref[...] = x_ref[...] + gt_ref[...] * y


def _moe_combine(y_sorted, pos, x2, topw2, gate, rows_per_batch):
    m, d = x2.shape
    rows = _fit(rows_per_batch, 128, 8)
    blk = 2 * _nbytes((rows, d), F32) + _nbytes((rows, V7X_LANES), F32) + _nbytes((1, d), F32)
    return pl.pallas_call(
        functools.partial(_combine_kernel, rows=rows),
        grid_spec=pltpu.PrefetchScalarGridSpec(
            num_scalar_prefetch=1, grid=(m // rows,),
            in_specs=[pl.BlockSpec(memory_space=pl.ANY),
                      pl.BlockSpec((rows, d), lambda t, pos: (t, 0)),
                      pl.BlockSpec((rows, V7X_LANES), lambda t, pos: (t, 0)),
                      pl.BlockSpec((None, 1, d), lambda t, pos: ((t * rows) // rows_per_batch, 0, 0))],
            out_specs=pl.BlockSpec((rows, d), lambda t, pos: (t, 0)),
            scratch_shapes=[pltpu.VMEM((2, TOP_K, rows, d), F32), pltpu.SemaphoreType.DMA((2,))]),
        out_shape=jax.ShapeDtypeStruct((m, d), F32),
        compiler_params=_cparams(("arbitrary",), blk, (2 * TOP_K + 2) * _nbytes((rows, d), F32)),
        name="moe_combine",
    )(pos, y_sorted, x2, topw2, gate)


def _moe_schedule(route_i, counts, n_exp, bm, expert_base):
    m = route_i.shape[0]
    n_assign = m * TOP_K
    n_tiles = n_assign // bm + n_exp
    e = route_i[:, :TOP_K].reshape(n_assign)
    rank = route_i[:, TOP_K:2 * TOP_K].reshape(n_assign)
    onehot = (e[:, None] == jnp.arange(n_exp, dtype=I32)[None, :]).astype(I32)
    tiles_per = (counts + bm - 1) // bm
    tile_end = jnp.cumsum(tiles_per)
    row_start = (tile_end - tiles_per) * bm
    pos = jnp.sum(onehot * row_start[None, :], axis=1) + rank
    used = tile_end[-1]
    t = jnp.arange(n_tiles, dtype=I32)
    tr = jnp.minimum(t, used - 1)
    te = jnp.minimum(jnp.sum((tr[:, None] >= tile_end[None, :]).astype(I32), axis=1), n_exp - 1)
    tv = (t < used).astype(I32)
    prev_e = jnp.concatenate([jnp.full((1,), -1, I32), te[:-1]])
    tf = tv * ((te != prev_e).astype(I32))
    later_start = (t[None, :] > t[:, None]) & (tf[None, :] == 1)
    tn = jnp.min(jnp.where(later_start, t[None, :], n_tiles), axis=1)
    tn = jnp.where(tn == n_tiles, -1, tn)
    src = (jnp.arange(n_tiles * bm, dtype=I32) % m).at[pos].set(jnp.arange(n_assign, dtype=I32) // TOP_K)
    sched = (te.astype(I32) + expert_base, tf.astype(I32), tv, tr.astype(I32), tn.astype(I32))
    return sched, pos.astype(I32), src


def kernel(x, c, w_ada, b_ada, g_mix, g_ffn, w_in, conv_w, conv_b, conv_ln_g, conv_ln_b, w_conv_out,
           q_gain, k_gain, rel_bias, w_attn_out, w_out, ffn_w1, ffn_w3, ffn_w2, moe_router, moe_w1,
           moe_w3, moe_w2):
    b, s, d = x.shape
    m = b * s
    depth = w_ada.shape[0]
    in_w = w_in.shape[2]
    ch = conv_w.shape[2]
    hd = q_gain.shape[1]
    att_out_w = w_attn_out.shape[1]
    nh = att_out_w // hd
    ng = len(ATTN_GROUPS)
    att_w = rel_bias.shape[1] * hd
    q_off = 2 * ch
    gate_off = q_off + 3 * att_w
    assert in_w == gate_off + 2 * d and rel_bias.shape[1] == ng * nh

    bm = _fit(s, 1024)
    stack = lambda w: w.reshape((w.shape[0] * w.shape[1],) + w.shape[2:])
    bn = _fit(d, 512)
    aw = nh * hd
    dils = [dil for _, dil in ATTN_GROUPS]
    assert dils[0] == 1 and q_off % bn == 0 and aw % bn == 0 and q_off % aw == 0

    cb = lambda start, width, n=bn: list(range(start // n, (start + width) // n))
    qkv_cols = lambda g, n=bn: sum([cb(q_off + i * att_w + g * aw, aw, n) for i in range(3)], [])
    bn_in = _fit(math.gcd(q_off, aw, d), 1024)
    nat_cols = cb(0, q_off, bn_in) + qkv_cols(0, bn_in) + cb(gate_off, 2 * d, bn_in)
    nat_gate = q_off + 3 * aw

    mod = _adaln(c, w_ada, b_ada)
    bias = _bias_tables(rel_bias, nh)

    for l in range(depth):
        sh1, sc1, gt1, sh2, sc2, gt2 = [mod[l, :, i * d:(i + 1) * d] for i in range(6)]
        gt1 = gt1.reshape(b, 1, d)
        gt2 = gt2.reshape(b, 1, d)

        h = _rmsmod(x, g_mix[l], sh1, sc1, BF16).reshape(m, d)
        sched = _dense_sched(m // bm, l)
        p = _ws_matmul("in_proj", _plain_compute, [(h, d, 0)], [(w_in, d, 0)], [],
                       m, len(nat_cols) * bn_in, BF16, bm, bn_in, sched, w_cols=nat_cols)
        p3 = p.reshape(b, s, len(nat_cols) * bn_in)
        cu = _conv_branch(p3, conv_w[l], conv_b[l], conv_ln_g[l], conv_ln_b[l]).reshape(m, ch)
        outs, lses = [], []
        for g, dil in enumerate(dils):
            if dil == 1:
                pa, qb = p3, q_off // aw
            else:
                pa = _ws_matmul(f"in_proj_g{g}", _plain_compute, [(h, d, 0)], [(w_in, d, 0)], [],
                                m, 3 * aw, BF16, bm, bn, sched, w_cols=qkv_cols(g), dil=dil,
                                rows_per_batch=s).reshape(b, s, 3 * aw)
                qb = 0
            o_g, lse_g = _attn_group(pa, g, dil, qb, qb + 1, qb + 2, nh, hd, q_gain[l], k_gain[l], bias[g])
            outs.append(o_g)
            lses.append(lse_g)
        o = _group_merge(outs, lses, dils, hd).reshape(m, att_out_w)
        merged = _ws_matmul(
            "branch_merge", _merge_compute, [(cu, ch, 0), (o, att_out_w, 0)],
            [(w_conv_out, ch, 0), (w_attn_out, att_out_w, 0)],
            [(p, _tile_extra(bm, bn, nat_gate // bn)), (p, _tile_extra(bm, bn, (nat_gate + d) // bn))],
            m, d, BF16, bm, bn, sched)
        x2 = _ws_matmul(
            "out_proj", _residual_compute, [(merged, d, 0)], [(w_out, d, 0)],
            [(x.reshape(m, d), _tile_extra(bm, bn, 0)), (gt1, _batch_row_extra(bm, bn, s))],
            m, d, F32, bm, bn, sched)
        x = x2.reshape(b, s, d)

        j = l // 2
        if l % 2 == 0:
            h = _rmsmod(x, g_ffn[l], sh2, sc2, BF16).reshape(m, d)
            d_ff = ffn_w1.shape[2]
            hid = _ws_matmul("ffn_up", _swiglu_compute, [(h, d, 0)],
                             [(ffn_w1, d, 0), (ffn_w3, d, 0)], [],
                             m, d_ff, BF16, bm, _fit(d_ff, 256), _dense_sched(m // bm, j))
            ksplit = 2 if d_ff % (2 * V7X_LANES) == 0 else 1
            kc = d_ff // ksplit
            bm2 = _fit(s, 1024)
            sched2 = _dense_sched(m // bm2, j)
            for kb in range(ksplit):
                x2 = _ws_matmul(
                    "ffn_down", _residual_compute, [(hid, kc, kb)], [(ffn_w2, kc, kb)],
                    [(x2, _tile_extra(bm2, bn, 0)), (gt2, _batch_row_extra(bm2, bn, s))],
                    m, d, F32, bm2, bn, sched2)
            x = x2.reshape(b, s, d)
        else:
            n_exp = moe_router.shape[2]
            d_fe = moe_w1.shape[3]
            hf, topw, topi, cnt = _rmsmod(x, g_ffn[l], sh2, sc2, F32, w_router=moe_router[j])
            bme = _fit(m, 512)
            msched, pos, src = _moe_schedule(topi.reshape(m, V7X_LANES), cnt[0, :n_exp].astype(I32), n_exp, bme,
                                             j * n_exp)
            n_rows = src.shape[0]
            xs = _moe_gather(hf.reshape(m, d), src, _fit(bme, 128, 16))
            hid = _ws_matmul("moe_up", _swiglu_compute, [(xs, d, 0)],
                             [(stack(moe_w1), d, 0), (stack(moe_w3), d, 0)], [],
                             n_rows, d_fe, BF16, bme, _fit(d_fe, 512), msched)
            ys = _ws_matmul("moe_down", _plain_compute, [(hid, d_fe, 0)], [(stack(moe_w2), d_fe, 0)], [],
                            n_rows, d, F32, bme, _fit(d, 1024), msched)
            x = _moe_combine(ys, pos, x2, topw.reshape(m, V7X_LANES), gt2, s).reshape(b, s, d)
    return x
```

```python
import functools
import math

import numpy as np
import jax
import jax.numpy as jnp
from jax import lax
from jax.experimental import pallas as pl
from jax.experimental.pallas import tpu as pltpu

F32 = jnp.float32
BF16 = jnp.bfloat16
I32 = jnp.int32

ATTN_GROUPS = ((128, 1), (512, 4), (2048, 16))
ATTN_BLOCK = 128
MAX_DISTANCE = 2048
TOP_K = 2

V7X_VMEM_BYTES = 64 * 1024 * 1024
V7X_LANES = 128
V7X_SUBLANES = 8
VMEM_REQUEST_CAP = V7X_VMEM_BYTES - 6 * 1024 * 1024
VMEM_SLACK = 8 * 1024 * 1024
NEG_INF = float("-inf")


def _fit(n, pref, unit=V7X_LANES):
    best = None
    for d in range(unit, min(n, pref) + 1, unit):
        if n % d == 0:
            best = d
    return best if best is not None else n


def _nbytes(shape, dtype):
    return int(np.prod(shape)) * jnp.dtype(dtype).itemsize


def _cparams(semantics, block_bytes, scratch_bytes=0):
    limit = min(VMEM_REQUEST_CAP, 2 * block_bytes + scratch_bytes + VMEM_SLACK)
    return pltpu.CompilerParams(dimension_semantics=semantics, vmem_limit_bytes=int(limit))


def _ada_kernel(c_ref, w_ref, b_ref, o_ref):
    c = c_ref[...]
    ca = (c * jax.nn.sigmoid(c)).astype(BF16)
    o_ref[...] = jnp.dot(ca, w_ref[...].astype(BF16), preferred_element_type=F32) + b_ref[...]


def _adaln(c, w_ada, b_ada):
    depth, d, n = w_ada.shape
    b = c.shape[0]
    rows = 8
    c8 = jnp.pad(c, ((0, rows - b), (0, 0)))
    bn = _fit(n, 1024)
    blk = _nbytes((rows, d), F32) + _nbytes((d, bn), F32) + 2 * _nbytes((rows, bn), F32)
    out = pl.pallas_call(
        _ada_kernel,
        grid=(depth, n // bn),
        in_specs=[pl.BlockSpec((rows, d), lambda l, j: (0, 0)),
                  pl.BlockSpec((None, d, bn), lambda l, j: (l, 0, j)),
                  pl.BlockSpec((None, 1, bn), lambda l, j: (l, 0, j))],
        out_specs=pl.BlockSpec((None, rows, bn), lambda l, j: (l, 0, j)),
        out_shape=jax.ShapeDtypeStruct((depth, rows, n), F32),
        compiler_params=_cparams(("arbitrary", "arbitrary"), blk, _nbytes((d, bn), BF16)),
        name="adaln",
    )(c8, w_ada, b_ada.reshape(depth, 1, n))
    return out[:, :b]


NORM_ROWS = 16
NORM_COLS = 512


def _normed_into(x_ref, g_ref, sh_ref, sc_ref, o_ref):
    ts, d = x_ref.shape
    rows, cols = min(NORM_ROWS, ts), min(NORM_COLS, d)
    for r0 in range(0, ts, rows):
        sq = jnp.zeros((rows, cols), F32)
        for c0 in range(0, d, cols):
            x = x_ref[r0:r0 + rows, c0:c0 + cols]
            sq = sq + x * x
        inv = lax.rsqrt(jnp.sum(sq, axis=-1, keepdims=True) * (1.0 / d) + 1e-6)
        for c0 in range(0, d, cols):
            cs = slice(c0, c0 + cols)
            y = x_ref[r0:r0 + rows, cs] * inv * g_ref[:, cs]
            o_ref[r0:r0 + rows, cs] = (y * (1.0 + sc_ref[:, cs]) + sh_ref[:, cs]).astype(o_ref.dtype)


def _rmsmod_kernel(x_ref, g_ref, sh_ref, sc_ref, o_ref):
    _normed_into(x_ref, g_ref, sh_ref, sc_ref, o_ref)


def _rmsmod_route_kernel(x_ref, g_ref, sh_ref, sc_ref, whi_ref, wlo_ref, h_ref, tw_ref, ti_ref, cnt_ref,
                         *, n_exp):
    _normed_into(x_ref, g_ref, sh_ref, sc_ref, h_ref)
    h = h_ref[...]
    h_hi = h.astype(BF16)
    h_lo = (h - h_hi.astype(F32)).astype(BF16)
    logits = jnp.dot(h_hi, whi_ref[...], preferred_element_type=F32) \
        + (jnp.dot(h_lo, whi_ref[...], preferred_element_type=F32)
           + jnp.dot(h_hi, wlo_ref[...], preferred_element_type=F32))
    lane = lax.broadcasted_iota(I32, logits.shape, 1)
    lanef = lane.astype(F32)
    sentinel = float(V7X_LANES)
    lg = jnp.where(lane < n_exp, logits, NEG_INF)
    m1 = jnp.max(lg, axis=-1, keepdims=True)
    i1 = jnp.min(jnp.where(lg == m1, lanef, sentinel), axis=-1, keepdims=True)
    lg2 = jnp.where(lanef == i1, NEG_INF, lg)
    m2 = jnp.max(lg2, axis=-1, keepdims=True)
    i2 = jnp.min(jnp.where(lg2 == m2, lanef, sentinel), axis=-1, keepdims=True)
    e2 = jnp.exp(m2 - m1)
    den = 1.0 + e2
    tw_ref[...] = jnp.where(lane == 0, 1.0 / den, jnp.where(lane == 1, e2 / den, 0.0))

    @pl.when((pl.program_id(0) == 0) & (pl.program_id(1) == 0))
    def _init_counts():
        cnt_ref[...] = jnp.zeros(cnt_ref.shape, F32)

    oh1 = (lanef == i1).astype(F32)
    oh2 = (lanef == i2).astype(F32)
    oh = oh1 + oh2
    ts = oh.shape[0]
    earlier = (lax.broadcasted_iota(I32, (ts, ts), 1) < lax.broadcasted_iota(I32, (ts, ts), 0))
    before = cnt_ref[0:1, :] + jnp.dot(earlier.astype(BF16), oh.astype(BF16), preferred_element_type=F32)
    r1 = jnp.sum(oh1 * before, axis=-1, keepdims=True)
    r2 = jnp.sum(oh2 * before, axis=-1, keepdims=True)
    cnt_ref[...] = cnt_ref[...] + jnp.sum(oh, axis=0, keepdims=True)
    ti_ref[...] = jnp.where(lane == 0, i1, jnp.where(lane == 1, i2, jnp.where(
        lane == 2, r1, jnp.where(lane == 3, r2, 0.0)))).astype(I32)


def _rmsmod(x, g, shift, scale, out_dtype, w_router=None):
    b, s, d = x.shape
    ts = _fit(s, 256, 8)
    specs = [pl.BlockSpec((None, ts, d), lambda i, j: (i, j, 0)),
             pl.BlockSpec((1, d), lambda i, j: (0, 0)),
             pl.BlockSpec((None, 1, d), lambda i, j: (i, 0, 0)),
             pl.BlockSpec((None, 1, d), lambda i, j: (i, 0, 0))]
    args = [x, g.reshape(1, d), shift.reshape(b, 1, d), scale.reshape(b, 1, d)]
    tile = pl.BlockSpec((None, ts, d), lambda i, j: (i, j, 0))
    blk = _nbytes((ts, d), F32) * 2 + 3 * _nbytes((1, d), F32)
    if w_router is None:
        return pl.pallas_call(
            _rmsmod_kernel, grid=(b, s // ts), in_specs=specs, out_specs=tile,
            out_shape=jax.ShapeDtypeStruct((b, s, d), out_dtype),
            compiler_params=_cparams(("arbitrary", "arbitrary"), blk, 4 * _nbytes((ts, d), F32)),
            name="rmsmod")(*args)
    n_exp = w_router.shape[1]
    wr = jnp.pad(w_router, ((0, 0), (0, V7X_LANES - n_exp)))
    wr_hi = wr.astype(BF16)
    wr_lo = (wr - wr_hi.astype(F32)).astype(BF16)
    lanes = pl.BlockSpec((None, ts, V7X_LANES), lambda i, j: (i, j, 0))
    blk += _nbytes((d, V7X_LANES), F32) + 2 * _nbytes((ts, V7X_LANES), F32)
    return pl.pallas_call(
        functools.partial(_rmsmod_route_kernel, n_exp=n_exp),
        grid=(b, s // ts),
        in_specs=specs + [pl.BlockSpec((d, V7X_LANES), lambda i, j: (0, 0))] * 2,
        out_specs=[tile, lanes, lanes, pl.BlockSpec((8, V7X_LANES), lambda i, j: (0, 0))],
        out_shape=[jax.ShapeDtypeStruct((b, s, d), F32),
                   jax.ShapeDtypeStruct((b, s, V7X_LANES), F32),
                   jax.ShapeDtypeStruct((b, s, V7X_LANES), I32),
                   jax.ShapeDtypeStruct((8, V7X_LANES), F32)],
        compiler_params=_cparams(("arbitrary", "arbitrary"), blk, 8 * _nbytes((ts, d), F32)),
        name="rmsmod_route")(*args, wr_hi, wr_lo)


def _ws_kernel(te_ref, tf_ref, tv_ref, tr_ref, tn_ref, wc_ref, *refs, nx, nw, ne, compute, dil, w_blocks, bn):
    del tr_ref
    x_refs = refs[:nx]
    w_refs = refs[nx:nx + nw]
    e_refs = refs[nx + nw:nx + nw + ne]
    o_ref = refs[nx + nw + ne]
    rest = refs[nx + nw + ne + 1:]
    wb_refs, stage_refs, sem = rest[:nw], rest[nw:2 * nw], rest[2 * nw]
    j = pl.program_id(0)
    t = pl.program_id(1)

    def weight_copy(i, expert, col_tile):
        k, kb = w_blocks[i]
        col = pl.multiple_of(wc_ref[col_tile] * bn, bn)
        return pltpu.make_async_copy(w_refs[i].at[expert, pl.ds(kb * k, k), pl.ds(col, bn)],
                                     stage_refs[i], sem.at[i])

    @pl.when(tf_ref[t] == 1)
    def _new_weights():
        @pl.when((j == 0) & (t == 0))
        def _first_block():
            for i in range(nw):
                weight_copy(i, te_ref[0], 0).start()

        for i in range(nw):
            weight_copy(i, te_ref[t], j).wait()
            wb_refs[i][...] = stage_refs[i][...].astype(BF16)

        nxt = tn_ref[t]

        @pl.when(nxt >= 0)
        def _next_run():
            for i in range(nw):
                weight_copy(i, te_ref[jnp.maximum(nxt, 0)], j).start()

        @pl.when((nxt < 0) & (j + 1 < pl.num_programs(0)))
        def _next_column_tile():
            for i in range(nw):
                weight_copy(i, te_ref[0], j + 1).start()

    @pl.when(tv_ref[t] == 1)
    def _compute():
        res = compute(x_refs, wb_refs, e_refs)
        if dil == 1:
            o_ref[...] = res.astype(o_ref.dtype)
        else:
            perm_ref = rest[-1]
            n_chunks, bm, lanes = perm_ref.shape
            rows = bm // dil
            for c in range(n_chunks):
                perm_ref[c] = res[:, c * lanes:(c + 1) * lanes]
            for r in range(dil):
                for c in range(n_chunks):
                    o_ref[r, :, c * lanes:(c + 1) * lanes] = \
                        perm_ref[c, pl.ds(r, rows, stride=dil), :].astype(o_ref.dtype)

    @pl.when(tv_ref[t] == 0)
    def _unused_tile():
        o_ref[...] = jnp.zeros(o_ref.shape, o_ref.dtype)


def _dense_sched(n_tiles, expert=0):
    first = np.zeros((n_tiles,), np.int32)
    first[0] = 1
    return (jnp.full((n_tiles,), expert, I32), jnp.asarray(first), jnp.ones((n_tiles,), I32),
            jnp.arange(n_tiles, dtype=I32), jnp.full((n_tiles,), -1, I32))


def _tile_extra(bm, bn, col_off):
    return pl.BlockSpec((bm, bn), lambda j, t, te, tf, tv, tr, *_: (tr[t], col_off + j))


def _batch_row_extra(bm, bn, rows_per_batch):
    return pl.BlockSpec((None, 1, bn), lambda j, t, te, tf, tv, tr, *_: ((tr[t] * bm) // rows_per_batch, 0, j))


def _ws_matmul(name, compute, xs, ws, extras, out_rows, out_cols, out_dtype, bm, bn, sched,
               w_cols=None, dil=1, rows_per_batch=None):
    n_tiles = sched[0].shape[0]
    nj = out_cols // bn
    if w_cols is None:
        w_cols = np.arange(nj)
    assert len(w_cols) == nj
    w_cols = jnp.asarray(np.asarray(w_cols, np.int32))
    in_specs, args, blk = [], [], 0
    for arr, k, kb in xs:
        in_specs.append(pl.BlockSpec((bm, k), lambda j, t, te, tf, tv, tr, *_, kb=kb: (tr[t], kb)))
        args.append(arr)
        blk += _nbytes((bm, k), arr.dtype)
    wb, stage, scratch_bytes = [], [], 0
    for arr, k, kb in ws:
        in_specs.append(pl.BlockSpec(memory_space=pl.ANY))
        args.append(arr)
        wb.append(pltpu.VMEM((k, bn), BF16))
        stage.append(pltpu.VMEM((k, bn), arr.dtype))
        scratch_bytes += _nbytes((k, bn), BF16) + _nbytes((k, bn), arr.dtype)
    scratch = wb + stage + [pltpu.SemaphoreType.DMA((len(ws),))]
    for arr, spec in extras:
        in_specs.append(spec)
        args.append(arr)
        blk += _nbytes([d for d in spec.block_shape if d is not None], arr.dtype)
    blk += _nbytes((bm, bn), out_dtype)
    scratch_bytes += 2 * _nbytes((bm, bn), F32)
    if dil == 1:
        out_spec = pl.BlockSpec((bm, bn), lambda j, t, *_: (t, j))
        out_shape = jax.ShapeDtypeStruct((out_rows, out_cols), out_dtype)
    else:
        tpb = rows_per_batch // bm
        assert rows_per_batch % bm == 0 and bm % dil == 0
        out_spec = pl.BlockSpec((None, dil, bm // dil, bn), lambda j, t, *_: (t // tpb, 0, t % tpb, j))
        out_shape = jax.ShapeDtypeStruct((out_rows // rows_per_batch, dil, rows_per_batch // dil, out_cols),
                                         out_dtype)
        assert bn % V7X_LANES == 0
        scratch.append(pltpu.VMEM((bn // V7X_LANES, bm, V7X_LANES), F32))
        scratch_bytes += _nbytes((bm, bn), F32)
    kern = functools.partial(_ws_kernel, nx=len(xs), nw=len(ws), ne=len(extras), compute=compute, dil=dil,
                             w_blocks=tuple((k, kb) for _, k, kb in ws), bn=bn)
    return pl.pallas_call(
        kern,
        grid_spec=pltpu.PrefetchScalarGridSpec(
            num_scalar_prefetch=6, grid=(nj, n_tiles), in_specs=in_specs,
            out_specs=out_spec, scratch_shapes=scratch),
        out_shape=out_shape,
        compiler_params=_cparams(("arbitrary", "arbitrary"), blk, scratch_bytes),
        name=name,
    )(*sched, w_cols, *args)


def _mm(x, wb_ref):
    return jnp.dot(x, wb_ref[...], preferred_element_type=F32)


def _plain_compute(xr, wr, er):
    return _mm(xr[0][...], wr[0])


def _swiglu_compute(xr, wr, er):
    x = xr[0][...]
    a = _mm(x, wr[0])
    return a * jax.nn.sigmoid(a) * _mm(x, wr[1])


def _merge_compute(xr, wr, er):
    conv_out = _mm(xr[0][...], wr[0])
    attn_out = _mm(xr[1][...], wr[1])
    gc = er[0][...].astype(F32)
    ga = er[1][...].astype(F32)
    return jax.nn.sigmoid(gc) * conv_out + jax.nn.sigmoid(ga) * attn_out


def _residual_compute(xr, wr, er):
    return er[0][...] + er[1][...] * _mm(xr[0][...], wr[0])


CONV_HALO = 32


def _conv_kernel(a_ref, b_ref, ah_ref, bh_ref, w_ref, cb_ref, g_ref, beta_ref, o_ref, ubuf, sbuf, cbuf,
                 *, ts, width, ch):
    i = pl.program_id(1)
    a = a_ref[...].astype(F32)
    b = b_ref[...].astype(F32)
    ubuf[CONV_HALO:CONV_HALO + ts, :] = a * jax.nn.sigmoid(b)
    ah = ah_ref[...].astype(F32)
    bh = bh_ref[...].astype(F32)
    ubuf[0:CONV_HALO, :] = jnp.where(i > 0, ah * jax.nn.sigmoid(bh), 0.0)
    span = sbuf.shape[1]
    for s in range(1, V7X_SUBLANES):
        sbuf[s - 1] = ubuf[s:s + span, :]
    off = CONV_HALO - (width - 1)
    rch = V7X_SUBLANES
    for r0 in range(0, ts, rch):
        acc = jnp.zeros((rch, ch), F32)
        for j in range(width):
            shift, row = (off + j) % V7X_SUBLANES, r0 + (off + j) // V7X_SUBLANES * V7X_SUBLANES
            src = ubuf if shift == 0 else sbuf.at[shift - 1]
            acc = acc + w_ref[j] * src[row:row + rch, :]
        cbuf[r0:r0 + rch, :] = acc + cb_ref[...]
    y = cbuf[...]
    mu = jnp.mean(y, axis=-1, keepdims=True)
    yc = y - mu
    var = jnp.mean(yc * yc, axis=-1, keepdims=True)
    yn = yc * lax.rsqrt(var + 1e-5) * g_ref[...] + beta_ref[...]
    o_ref[...] = (yn * jax.nn.sigmoid(yn)).astype(o_ref.dtype)


def _conv_branch(p3, conv_w, conv_b, ln_g, ln_b):
    b, s, _ = p3.shape
    width, ch = conv_w.shape
    assert width - 1 <= CONV_HALO
    ts = _fit(s, 128, CONV_HALO)
    hb = ts // CONV_HALO
    tile = lambda col: pl.BlockSpec((None, ts, ch), lambda bi, i, col=col: (bi, i, col))
    halo = lambda col: pl.BlockSpec((None, CONV_HALO, ch),
                                    lambda bi, i, col=col: (bi, jnp.maximum(i * hb - 1, 0), col))
    row = pl.BlockSpec((1, ch), lambda bi, i: (0, 0))
    blk = 2 * _nbytes((ts + CONV_HALO, ch), BF16) + _nbytes((width * V7X_SUBLANES + 3, ch), F32) + _nbytes((ts, ch), BF16)
    shifted = (V7X_SUBLANES - 1, ts + CONV_HALO - V7X_SUBLANES, ch)
    scratch_bytes = _nbytes((2 * ts + CONV_HALO, ch), F32) + _nbytes(shifted, F32)
    return pl.pallas_call(
        functools.partial(_conv_kernel, ts=ts, width=width, ch=ch),
        grid=(b, s // ts),
        in_specs=[tile(0), tile(1), halo(0), halo(1),
                  pl.BlockSpec((width, V7X_SUBLANES, ch), lambda bi, i: (0, 0, 0)), row, row, row],
        out_specs=pl.BlockSpec((None, ts, ch), lambda bi, i: (bi, i, 0)),
        out_shape=jax.ShapeDtypeStruct((b, s, ch), BF16),
        scratch_shapes=[pltpu.VMEM((ts + CONV_HALO, ch), F32), pltpu.VMEM(shifted, F32),
                        pltpu.VMEM((ts, ch), F32)],
        compiler_params=_cparams(("arbitrary", "arbitrary"), blk, scratch_bytes + 4 * _nbytes((ts, ch), F32)),
        name="conv_module",
    )(p3, p3, p3, p3, jnp.broadcast_to(conv_w[:, None, :], (width, V7X_SUBLANES, ch)),
      conv_b.reshape(1, ch), ln_g.reshape(1, ch), ln_b.reshape(1, ch))


def _bias_kernel(rel_ref, idx_ref, o_ref, *, n_buckets, nh, span, blk):
    g = pl.program_id(0)
    h = pl.program_id(1)
    idx = idx_ref[...]
    acc = jnp.zeros(idx.shape, F32)
    for bkt in range(n_buckets):
        acc = jnp.where(idx == bkt, rel_ref[bkt, g * nh + h], acc)
    qi = lax.broadcasted_iota(I32, idx.shape, 0) + blk
    kj = lax.broadcasted_iota(I32, idx.shape, 1)
    delta = qi - kj
    o_ref[...] = jnp.where((delta >= 0) & (delta <= span), acc, NEG_INF)


def _t5_bucket(dist, n_buckets):
    max_exact = n_buckets // 2
    n = jnp.maximum(dist, 1).astype(F32)
    large = max_exact + (jnp.log(n / max_exact) / math.log(MAX_DISTANCE / max_exact)
                         * (n_buckets - max_exact)).astype(I32)
    large = jnp.minimum(large, n_buckets - 1)
    return jnp.where(dist < max_exact, dist, large)


def _bias_tables(rel_bias, nh):
    n_buckets = rel_bias.shape[0]
    blk = ATTN_BLOCK
    span = ATTN_GROUPS[0][0] // ATTN_GROUPS[0][1]
    assert all(w // d == span for w, d in ATTN_GROUPS)
    delta = (jnp.arange(blk, dtype=I32)[:, None] + blk) - jnp.arange(2 * blk, dtype=I32)[None, :]
    idx = jnp.stack([_t5_bucket(jnp.maximum(delta, 0) * d, n_buckets) for _, d in ATTN_GROUPS])
    ng = len(ATTN_GROUPS)
    return pl.pallas_call(
        functools.partial(_bias_kernel, n_buckets=n_buckets, nh=nh, span=span, blk=blk),
        grid=(ng, nh),
        in_specs=[pl.BlockSpec(memory_space=pltpu.SMEM),
                  pl.BlockSpec((None, blk, 2 * blk), lambda g, h: (g, 0, 0))],
        out_specs=pl.BlockSpec((None, None, blk, 2 * blk), lambda g, h: (g, h, 0, 0)),
        out_shape=jax.ShapeDtypeStruct((ng, nh, blk, 2 * blk), F32),
        name="rel_bias_tables",
    )(rel_bias, idx)


ATTN_HEAD_PAIR = 2


def _attn_kernel(q_ref, k_ref, v_ref, qg_ref, kg_ref, bias_ref, seg_ref, o_ref, lse_ref, kprev, vprev,
                 qn_s, kn_s, s_s, p_s, m_s, *, nh, hd, blk):
    has_prev = pl.program_id(2) > 0
    scale = hd ** -0.5
    nt = (((1,), (1,)), ((), ()))
    pw = ATTN_HEAD_PAIR * hd
    seg = seg_ref[...]
    ones = jnp.ones((blk, hd), BF16)

    @pl.when(jnp.logical_not(has_prev))
    def _no_previous_block():
        kprev[...] = jnp.zeros(kprev.shape, kprev.dtype)
        vprev[...] = jnp.zeros(vprev.shape, vprev.dtype)

    def qk_norm(ref, c0, gain):
        x = ref[:, c0:c0 + pw].astype(F32)
        ss = jnp.dot((x * x).astype(BF16), seg, preferred_element_type=F32)
        return (x * lax.rsqrt(ss * (1.0 / hd) + 1e-6) * gain).astype(BF16)

    for c0 in range(0, nh * hd, pw):
        qn_s[:, c0:c0 + pw] = qk_norm(q_ref, c0, qg_ref[...])
        kn_s[:, c0:c0 + pw] = qk_norm(k_ref, c0, kg_ref[...])
    for h in range(nh):
        hs = slice(h * hd, (h + 1) * hd)
        bias = bias_ref[h]
        q = qn_s[:, hs]
        s_s[h, :, :blk] = lax.dot_general(q, kprev[:, hs], nt, preferred_element_type=F32) * scale \
            + jnp.where(has_prev, bias[:, :blk], NEG_INF)
        s_s[h, :, blk:] = lax.dot_general(q, kn_s[:, hs], nt, preferred_element_type=F32) * scale + bias[:, blk:]
    for h in range(nh):
        s = s_s[h]
        m = jnp.max(s, axis=-1, keepdims=True)
        m_s[h] = jnp.broadcast_to(m, (blk, hd))
        p_s[h] = jnp.exp(s - m).astype(BF16)
    assert hd == V7X_LANES and nh <= V7X_LANES
    head_lane = lax.broadcasted_iota(I32, (blk, hd), 1)
    lse = jnp.zeros((blk, hd), F32)
    for h in range(nh):
        hs = slice(h * hd, (h + 1) * hd)
        res = jnp.dot(p_s[h, :, :blk], jnp.concatenate([vprev[:, hs], ones], axis=1),
                      preferred_element_type=F32) \
            + jnp.dot(p_s[h, :, blk:], jnp.concatenate([v_ref[:, hs], ones], axis=1),
                      preferred_element_type=F32)
        l = res[:, hd:]
        o_ref[:, hs] = (res[:, :hd] / l).astype(o_ref.dtype)
        lse = jnp.where(head_lane == h, m_s[h] + jnp.log(l), lse)
    lse_ref[...] = lse
    kprev[...] = kn_s[...]
    vprev[...] = v_ref[...]


def _attn_group(pa, g, dil, qb, kb, vb, nh, hd, q_gain, k_gain, bias_g):
    b, s, _ = pa.shape
    blk = ATTN_BLOCK
    seq = s // dil
    assert s % dil == 0 and seq % blk == 0
    nb = seq // blk
    aw = nh * hd
    assert nh % ATTN_HEAD_PAIR == 0
    pw = ATTN_HEAD_PAIR * hd
    own = lambda c: pl.BlockSpec((None, blk, aw), lambda bi, r, n, c=c: (bi, r * nb + n, c))
    gain = pl.BlockSpec((1, pw), lambda bi, r, n: (0, 0))
    out_spec = pl.BlockSpec((None, blk, aw), lambda bi, r, n: (bi, r * nb + n, 0))
    lane_head = np.arange(pw) // hd
    seg = jnp.asarray((lane_head[:, None] == lane_head[None, :]).astype(np.float32), BF16)
    tile_gain = lambda gn: jnp.tile(gn, ATTN_HEAD_PAIR).reshape(1, pw)
    blk_bytes = 3 * _nbytes((blk, aw), BF16) + _nbytes((nh, blk, 2 * blk), F32) \
        + _nbytes((blk, aw), BF16) + _nbytes((blk, aw), F32) + _nbytes((pw, pw), BF16)
    o, lse = pl.pallas_call(
        functools.partial(_attn_kernel, nh=nh, hd=hd, blk=blk),
        grid=(b, dil, nb),
        in_specs=[own(qb), own(kb), own(vb), gain, gain,
                  pl.BlockSpec((nh, blk, 2 * blk), lambda bi, r, n: (0, 0, 0)),
                  pl.BlockSpec((pw, pw), lambda bi, r, n: (0, 0))],
        out_specs=[out_spec, pl.BlockSpec((None, blk, V7X_LANES), lambda bi, r, n: (bi, r * nb + n, 0))],
        out_shape=[jax.ShapeDtypeStruct((b, s, aw), BF16), jax.ShapeDtypeStruct((b, s, V7X_LANES), F32)],
        scratch_shapes=[pltpu.VMEM((blk, aw), BF16), pltpu.VMEM((blk, aw), BF16),
                        pltpu.VMEM((blk, aw), BF16), pltpu.VMEM((blk, aw), BF16),
                        pltpu.VMEM((nh, blk, 2 * blk), F32), pltpu.VMEM((nh, blk, 2 * blk), BF16),
                        pltpu.VMEM((nh, blk, hd), F32)],
        compiler_params=_cparams(("arbitrary", "arbitrary", "arbitrary"), blk_bytes,
                                 4 * _nbytes((blk, aw), BF16) + 2 * _nbytes((nh, blk, 2 * blk), F32)),
        name=f"dilated_attn_g{g}",
    )(pa, pa, pa, tile_gain(q_gain), tile_gain(k_gain), bias_g, seg)
    return o.reshape(b, dil, seq, aw), lse.reshape(b, dil, seq, V7X_LANES)


def _group_merge_kernel(*refs, dils, ts):
    ng = len(dils)
    o_refs, l_refs, expand_ref, out_ref = refs[:ng], refs[ng:2 * ng], refs[2 * ng], refs[2 * ng + 1]
    scratch = list(refs[2 * ng + 2:])

    def token_order(ref, dil):
        if dil == 1:
            return ref[0].astype(F32)
        buf = scratch.pop(0)
        n_chunks, _, lanes = buf.shape
        rows = ts // dil
        for r in range(dil):
            for c in range(n_chunks):
                buf[c, pl.ds(r, rows, stride=dil), :] = ref[r, :, c * lanes:(c + 1) * lanes].astype(F32)
        return jnp.concatenate([buf[c] for c in range(n_chunks)], axis=1)

    def spread(w):
        hi = w.astype(BF16)
        lo = (w - hi.astype(F32)).astype(BF16)
        return jnp.dot(hi, expand_ref[...], preferred_element_type=F32) \
            + jnp.dot(lo, expand_ref[...], preferred_element_type=F32)

    ls = [token_order(r, dil) for r, dil in zip(l_refs, dils)]
    m = functools.reduce(jnp.maximum, ls)
    es = [jnp.exp(l - m) for l in ls]
    den = functools.reduce(lambda a, c: a + c, es)
    num = None
    for e, r, dil in zip(es, o_refs, dils):
        term = spread(e / den) * token_order(r, dil)
        num = term if num is None else num + term
    out_ref[...] = num.astype(out_ref.dtype)


def _group_merge(outs, lses, dils, hd):
    b, _, _, aw = outs[0].shape
    s = outs[0].shape[1] * outs[0].shape[2]
    nh = aw // hd
    ts = _fit(s, 256, 16 * max(dils))
    spec = lambda dil, w: pl.BlockSpec((None, dil, ts // dil, w), lambda i, j: (i, 0, j, 0))
    n_perm = sum(1 for dil in dils if dil > 1)
    lane = np.arange(V7X_LANES)[:, None]
    expand = jnp.asarray((lane == (np.arange(aw)[None, :] // hd)).astype(np.float32), BF16)
    blk = len(dils) * (_nbytes((ts, aw), BF16) + _nbytes((ts, V7X_LANES), F32)) + _nbytes((ts, aw), BF16) \
        + _nbytes((V7X_LANES, aw), BF16)
    assert nh <= V7X_LANES
    return pl.pallas_call(
        functools.partial(_group_merge_kernel, dils=tuple(dils), ts=ts),
        grid=(b, s // ts),
        in_specs=[spec(dil, aw) for dil in dils] + [spec(dil, V7X_LANES) for dil in dils]
        + [pl.BlockSpec((V7X_LANES, aw), lambda i, j: (0, 0))],
        out_specs=pl.BlockSpec((None, ts, aw), lambda i, j: (i, j, 0)),
        out_shape=jax.ShapeDtypeStruct((b, s, aw), BF16),
        scratch_shapes=[pltpu.VMEM((1, ts, V7X_LANES), F32)] * n_perm
        + [pltpu.VMEM((aw // V7X_LANES, ts, V7X_LANES), F32)] * n_perm,
        compiler_params=_cparams(("arbitrary", "arbitrary"), blk, (n_perm + 4) * _nbytes((ts, aw), F32)),
        name="attn_group_merge")(*outs, *lses, expand)


def _row_copy(src_hbm, row, dst, slot, sem):
    return pltpu.make_async_copy(src_hbm.at[pl.ds(row, 1), :], dst.at[pl.ds(slot, 1), :], sem)


ROW_DMA_UNROLL = 8
DMA_PRIORITIES = 2


def _gather_kernel(src_ref, h_hbm, o_ref, buf, sem, *, rows):
    t = pl.program_id(0)
    slot = t % 2

    def fetch(step, dst_slot):
        def start(i, carry):
            for p in range(DMA_PRIORITIES):
                row = i * DMA_PRIORITIES + p
                _row_copy(h_hbm, src_ref[step * rows + row], buf.at[dst_slot], row,
                          sem.at[dst_slot]).start(priority=p)
            return carry
        lax.fori_loop(0, rows // DMA_PRIORITIES, start, 0, unroll=ROW_DMA_UNROLL // DMA_PRIORITIES)

    @pl.when(t == 0)
    def _first():
        fetch(0, 0)

    @pl.when(t + 1 < pl.num_programs(0))
    def _next():
        fetch(t + 1, 1 - slot)

    def wait(i, carry):
        _row_copy(h_hbm, 0, buf.at[slot], i, sem.at[slot]).wait()
        return carry

    lax.fori_loop(0, rows, wait, 0, unroll=ROW_DMA_UNROLL)
    o_ref[...] = buf[slot].astype(o_ref.dtype)


def _moe_gather(h2, src, rows):
    r_total = src.shape[0]
    d = h2.shape[1]
    return pl.pallas_call(
        functools.partial(_gather_kernel, rows=rows),
        grid_spec=pltpu.PrefetchScalarGridSpec(
            num_scalar_prefetch=1, grid=(r_total // rows,),
            in_specs=[pl.BlockSpec(memory_space=pl.ANY)],
            out_specs=pl.BlockSpec((rows, d), lambda t, src: (t, 0)),
            scratch_shapes=[pltpu.VMEM((2, rows, d), F32), pltpu.SemaphoreType.DMA((2,))]),
        out_shape=jax.ShapeDtypeStruct((r_total, d), BF16),
        compiler_params=_cparams(("arbitrary",), _nbytes((rows, d), BF16), 3 * _nbytes((rows, d), F32)),
        name="moe_gather",
    )(src, h2)


def _combine_kernel(pos_ref, y_hbm, x_ref, tw_ref, gt_ref, o_ref, buf, sem, *, rows):
    t = pl.program_id(0)
    slot = t % 2

    def fetch(step, dst_slot):
        def start(i, carry):
            for k in range(TOP_K):
                _row_copy(y_hbm, pos_ref[(step * rows + i) * TOP_K + k], buf.at[dst_slot, k], i,
                          sem.at[dst_slot]).start(priority=k % DMA_PRIORITIES)
            return carry
        lax.fori_loop(0, rows, start, 0, unroll=ROW_DMA_UNROLL // TOP_K)

    @pl.when(t == 0)
    def _first():
        fetch(0, 0)

    @pl.when(t + 1 < pl.num_programs(0))
    def _next():
        fetch(t + 1, 1 - slot)

    def wait(i, carry):
        for k in range(TOP_K):
            _row_copy(y_hbm, 0, buf.at[slot, k], i, sem.at[slot]).wait()
        return carry

    lax.fori_loop(0, rows, wait, 0, unroll=ROW_DMA_UNROLL // TOP_K)
    tw = tw_ref[...]
    y = tw[:, 0:1] * buf[slot, 0]
    for k in range(1, TOP_K):
        y = y + tw[:, k:k + 1] * buf[slot, k]
    o_ref[...] = x_ref[...] + gt_ref[...] * y


def _moe_combine(y_sorted, pos, x2, topw2, gate, rows_per_batch):
    m, d = x2.shape
    rows = _fit(rows_per_batch, 128, 8)
    blk = 2 * _nbytes((rows, d), F32) + _nbytes((rows, V7X_LANES), F32) + _nbytes((1, d), F32)
    return pl.pallas_call(
        functools.partial(_combine_kernel, rows=rows),
        grid_spec=pltpu.PrefetchScalarGridSpec(
            num_scalar_prefetch=1, grid=(m // rows,),
            in_specs=[pl.BlockSpec(memory_space=pl.ANY),
                      pl.BlockSpec((rows, d), lambda t, pos: (t, 0)),
                      pl.BlockSpec((rows, V7X_LANES), lambda t, pos: (t, 0)),
                      pl.BlockSpec((None, 1, d), lambda t, pos: ((t * rows) // rows_per_batch, 0, 0))],
            out_specs=pl.BlockSpec((rows, d), lambda t, pos: (t, 0)),
            scratch_shapes=[pltpu.VMEM((2, TOP_K, rows, d), F32), pltpu.SemaphoreType.DMA((2,))]),
        out_shape=jax.ShapeDtypeStruct((m, d), F32),
        compiler_params=_cparams(("arbitrary",), blk, (2 * TOP_K + 2) * _nbytes((rows, d), F32)),
        name="moe_combine",
    )(pos, y_sorted, x2, topw2, gate)


def _moe_schedule(route_i, counts, n_exp, bm, expert_base):
    m = route_i.shape[0]
    n_assign = m * TOP_K
    n_tiles = n_assign // bm + n_exp
    e = route_i[:, :TOP_K].reshape(n_assign)
    rank = route_i[:, TOP_K:2 * TOP_K].reshape(n_assign)
    onehot = (e[:, None] == jnp.arange(n_exp, dtype=I32)[None, :]).astype(I32)
    tiles_per = (counts + bm - 1) // bm
    tile_end = jnp.cumsum(tiles_per)
    row_start = (tile_end - tiles_per) * bm
    pos = jnp.sum(onehot * row_start[None, :], axis=1) + rank
    used = tile_end[-1]
    t = jnp.arange(n_tiles, dtype=I32)
    tr = jnp.minimum(t, used - 1)
    te = jnp.minimum(jnp.sum((tr[:, None] >= tile_end[None, :]).astype(I32), axis=1), n_exp - 1)
    tv = (t < used).astype(I32)
    prev_e = jnp.concatenate([jnp.full((1,), -1, I32), te[:-1]])
    tf = tv * ((te != prev_e).astype(I32))
    later_start = (t[None, :] > t[:, None]) & (tf[None, :] == 1)
    tn = jnp.min(jnp.where(later_start, t[None, :], n_tiles), axis=1)
    tn = jnp.where(tn == n_tiles, -1, tn)
    src = (jnp.arange(n_tiles * bm, dtype=I32) % m).at[pos].set(jnp.arange(n_assign, dtype=I32) // TOP_K)
    sched = (te.astype(I32) + expert_base, tf.astype(I32), tv, tr.astype(I32), tn.astype(I32))
    return sched, pos.astype(I32), src


def kernel(x, c, w_ada, b_ada, g_mix, g_ffn, w_in, conv_w, conv_b, conv_ln_g, conv_ln_b, w_conv_out,
           q_gain, k_gain, rel_bias, w_attn_out, w_out, ffn_w1, ffn_w3, ffn_w2, moe_router, moe_w1,
           moe_w3, moe_w2):
    b, s, d = x.shape
    m = b * s
    depth = w_ada.shape[0]
    in_w = w_in.shape[2]
    ch = conv_w.shape[2]
    hd = q_gain.shape[1]
    att_out_w = w_attn_out.shape[1]
    nh = att_out_w // hd
    ng = len(ATTN_GROUPS)
    att_w = rel_bias.shape[1] * hd
    q_off = 2 * ch
    gate_off = q_off + 3 * att_w
    assert in_w == gate_off + 2 * d and rel_bias.shape[1] == ng * nh

    bm = _fit(s, 1024)
    stack = lambda w: w.reshape((w.shape[0] * w.shape[1],) + w.shape[2:])
    bn = _fit(d, 512)
    aw = nh * hd
    dils = [dil for _, dil in ATTN_GROUPS]
    assert dils[0] == 1 and q_off % bn == 0 and aw % bn == 0 and q_off % aw == 0

    cb = lambda start, width, n=bn: list(range(start // n, (start + width) // n))
    qkv_cols = lambda g, n=bn: sum([cb(q_off + i * att_w + g * aw, aw, n) for i in range(3)], [])
    bn_in = _fit(math.gcd(q_off, aw, d), 1024)
    nat_cols = cb(0, q_off, bn_in) + qkv_cols(0, bn_in) + cb(gate_off, 2 * d, bn_in)
    nat_gate = q_off + 3 * aw
    bmh, bnw = _fit(s, 512), _fit(math.gcd(d, nat_gate), 1024)

    mod = _adaln(c, w_ada, b_ada)
    bias = _bias_tables(rel_bias, nh)

    for l in range(depth):
        sh1, sc1, gt1, sh2, sc2, gt2 = [mod[l, :, i * d:(i + 1) * d] for i in range(6)]
        gt1 = gt1.reshape(b, 1, d)
        gt2 = gt2.reshape(b, 1, d)

        h = _rmsmod(x, g_mix[l], sh1, sc1, BF16).reshape(m, d)
        sched = _dense_sched(m // bm, l)
        schedh = _dense_sched(m // bmh, l)
        p = _ws_matmul("in_proj", _plain_compute, [(h, d, 0)], [(w_in, d, 0)], [],
                       m, len(nat_cols) * bn_in, BF16, bm, bn_in, sched, w_cols=nat_cols)
        p3 = p.reshape(b, s, len(nat_cols) * bn_in)
        cu = _conv_branch(p3, conv_w[l], conv_b[l], conv_ln_g[l], conv_ln_b[l]).reshape(m, ch)
        outs, lses = [], []
        for g, dil in enumerate(dils):
            if dil == 1:
                pa, qb = p3, q_off // aw
            else:
                pa = _ws_matmul(f"in_proj_g{g}", _plain_compute, [(h, d, 0)], [(w_in, d, 0)], [],
                                m, 3 * aw, BF16, bmh, bn_in, schedh, w_cols=qkv_cols(g, bn_in), dil=dil,
                                rows_per_batch=s).reshape(b, s, 3 * aw)
                qb = 0
            o_g, lse_g = _attn_group(pa, g, dil, qb, qb + 1, qb + 2, nh, hd, q_gain[l], k_gain[l], bias[g])
            outs.append(o_g)
            lses.append(lse_g)
        o = _group_merge(outs, lses, dils, hd).reshape(m, att_out_w)
        merged = _ws_matmul(
            "branch_merge", _merge_compute, [(cu, ch, 0), (o, att_out_w, 0)],
            [(w_conv_out, ch, 0), (w_attn_out, att_out_w, 0)],
            [(p, _tile_extra(bmh, bnw, nat_gate // bnw)), (p, _tile_extra(bmh, bnw, (nat_gate + d) // bnw))],
            m, d, BF16, bmh, bnw, schedh)
        x2 = _ws_matmul(
            "out_proj", _residual_compute, [(merged, d, 0)], [(w_out, d, 0)],
            [(x.reshape(m, d), _tile_extra(bmh, bnw, 0)), (gt1, _batch_row_extra(bmh, bnw, s))],
            m, d, F32, bmh, bnw, schedh)
        x = x2.reshape(b, s, d)

        j = l // 2
        if l % 2 == 0:
            h = _rmsmod(x, g_ffn[l], sh2, sc2, BF16).reshape(m, d)
            d_ff = ffn_w1.shape[2]
            hid = _ws_matmul("ffn_up", _swiglu_compute, [(h, d, 0)],
                             [(ffn_w1, d, 0), (ffn_w3, d, 0)], [],
                             m, d_ff, BF16, bm, _fit(d_ff, 256), _dense_sched(m // bm, j))
            ksplit = 2 if d_ff % (2 * V7X_LANES) == 0 else 1
            kc = d_ff // ksplit
            bm2 = _fit(s, 1024)
            sched2 = _dense_sched(m // bm2, j)
            for kb in range(ksplit):
                x2 = _ws_matmul(
                    "ffn_down", _residual_compute, [(hid, kc, kb)], [(ffn_w2, kc, kb)],
                    [(x2, _tile_extra(bm2, bn, 0)), (gt2, _batch_row_extra(bm2, bn, s))],
                    m, d, F32, bm2, bn, sched2)
            x = x2.reshape(b, s, d)
        else:
            n_exp = moe_router.shape[2]
            d_fe = moe_w1.shape[3]
            hf, topw, topi, cnt = _rmsmod(x, g_ffn[l], sh2, sc2, F32, w_router=moe_router[j])
            bme = _fit(m, 512)
            msched, pos, src = _moe_schedule(topi.reshape(m, V7X_LANES), cnt[0, :n_exp].astype(I32), n_exp, bme,
                                             j * n_exp)
            n_rows = src.shape[0]
            xs = _moe_gather(hf.reshape(m, d), src, _fit(bme, 128, 16))
            hid = _ws_matmul("moe_up", _swiglu_compute, [(xs, d, 0)],
                             [(stack(moe_w1), d, 0), (stack(moe_w3), d, 0)], [],
                             n_rows, d_fe, BF16, bme, _fit(d_fe, 512), msched)
            ys = _ws_matmul("moe_down", _plain_compute, [(hid, d_fe, 0)], [(stack(moe_w2), d_fe, 0)], [],
                            n_rows, d, F32, bme, _fit(d, 1024), msched)
            x = _moe_combine(ys, pos, x2, topw.reshape(m, V7X_LANES), gt2, s).reshape(b, s, d)
    return x
```

```python
import functools
import math

import numpy as np
import jax
import jax.numpy as jnp
from jax import lax
from jax.experimental import pallas as pl
from jax.experimental.pallas import tpu as pltpu

F32 = jnp.float32
BF16 = jnp.bfloat16
I32 = jnp.int32

ATTN_GROUPS = ((128, 1), (512, 4), (2048, 16))
ATTN_BLOCK = 128
MAX_DISTANCE = 2048
TOP_K = 2

V7X_VMEM_BYTES = 64 * 1024 * 1024
V7X_LANES = 128
V7X_SUBLANES = 8
VMEM_REQUEST_CAP = V7X_VMEM_BYTES - 6 * 1024 * 1024
VMEM_SLACK = 8 * 1024 * 1024
NEG_INF = float("-inf")


def _fit(n, pref, unit=V7X_LANES):
    best = None
    for d in range(unit, min(n, pref) + 1, unit):
        if n % d == 0:
            best = d
    return best if best is not None else n


def _nbytes(shape, dtype):
    return int(np.prod(shape)) * jnp.dtype(dtype).itemsize


def _cparams(semantics, block_bytes, scratch_bytes=0):
    limit = min(VMEM_REQUEST_CAP, 2 * block_bytes + scratch_bytes + VMEM_SLACK)
    return pltpu.CompilerParams(dimension_semantics=semantics, vmem_limit_bytes=int(limit))


def _ada_kernel(c_ref, w_ref, b_ref, o_ref):
    c = c_ref[...]
    ca = (c * jax.nn.sigmoid(c)).astype(BF16)
    o_ref[...] = jnp.dot(ca, w_ref[...].astype(BF16), preferred_element_type=F32) + b_ref[...]


def _adaln(c, w_ada, b_ada):
    depth, d, n = w_ada.shape
    b = c.shape[0]
    rows = 8
    c8 = jnp.pad(c, ((0, rows - b), (0, 0)))
    bn = _fit(n, 1024)
    blk = _nbytes((rows, d), F32) + _nbytes((d, bn), F32) + 2 * _nbytes((rows, bn), F32)
    out = pl.pallas_call(
        _ada_kernel,
        grid=(depth, n // bn),
        in_specs=[pl.BlockSpec((rows, d), lambda l, j: (0, 0)),
                  pl.BlockSpec((None, d, bn), lambda l, j: (l, 0, j)),
                  pl.BlockSpec((None, 1, bn), lambda l, j: (l, 0, j))],
        out_specs=pl.BlockSpec((None, rows, bn), lambda l, j: (l, 0, j)),
        out_shape=jax.ShapeDtypeStruct((depth, rows, n), F32),
        compiler_params=_cparams(("arbitrary", "arbitrary"), blk, _nbytes((d, bn), BF16)),
        name="adaln",
    )(c8, w_ada, b_ada.reshape(depth, 1, n))
    return out[:, :b]


NORM_ROWS = 16
NORM_COLS = 512


def _normed_into(x_ref, g_ref, sh_ref, sc_ref, o_ref):
    ts, d = x_ref.shape
    rows, cols = min(NORM_ROWS, ts), min(NORM_COLS, d)
    for r0 in range(0, ts, rows):
        sq = jnp.zeros((rows, cols), F32)
        for c0 in range(0, d, cols):
            x = x_ref[r0:r0 + rows, c0:c0 + cols]
            sq = sq + x * x
        inv = lax.rsqrt(jnp.sum(sq, axis=-1, keepdims=True) * (1.0 / d) + 1e-6)
        for c0 in range(0, d, cols):
            cs = slice(c0, c0 + cols)
            y = x_ref[r0:r0 + rows, cs] * inv * g_ref[:, cs]
            o_ref[r0:r0 + rows, cs] = (y * (1.0 + sc_ref[:, cs]) + sh_ref[:, cs]).astype(o_ref.dtype)


def _rmsmod_kernel(x_ref, g_ref, sh_ref, sc_ref, o_ref):
    _normed_into(x_ref, g_ref, sh_ref, sc_ref, o_ref)


def _rmsmod_route_kernel(x_ref, g_ref, sh_ref, sc_ref, whi_ref, wlo_ref, h_ref, tw_ref, ti_ref, cnt_ref,
                         *, n_exp):
    _normed_into(x_ref, g_ref, sh_ref, sc_ref, h_ref)
    h = h_ref[...]
    h_hi = h.astype(BF16)
    h_lo = (h - h_hi.astype(F32)).astype(BF16)
    logits = jnp.dot(h_hi, whi_ref[...], preferred_element_type=F32) \
        + (jnp.dot(h_lo, whi_ref[...], preferred_element_type=F32)
           + jnp.dot(h_hi, wlo_ref[...], preferred_element_type=F32))
    lane = lax.broadcasted_iota(I32, logits.shape, 1)
    lanef = lane.astype(F32)
    sentinel = float(V7X_LANES)
    lg = jnp.where(lane < n_exp, logits, NEG_INF)
    m1 = jnp.max(lg, axis=-1, keepdims=True)
    i1 = jnp.min(jnp.where(lg == m1, lanef, sentinel), axis=-1, keepdims=True)
    lg2 = jnp.where(lanef == i1, NEG_INF, lg)
    m2 = jnp.max(lg2, axis=-1, keepdims=True)
    i2 = jnp.min(jnp.where(lg2 == m2, lanef, sentinel), axis=-1, keepdims=True)
    e2 = jnp.exp(m2 - m1)
    den = 1.0 + e2
    tw_ref[...] = jnp.where(lane == 0, 1.0 / den, jnp.where(lane == 1, e2 / den, 0.0))

    @pl.when((pl.program_id(0) == 0) & (pl.program_id(1) == 0))
    def _init_counts():
        cnt_ref[...] = jnp.zeros(cnt_ref.shape, F32)

    oh1 = (lanef == i1).astype(F32)
    oh2 = (lanef == i2).astype(F32)
    oh = oh1 + oh2
    ts = oh.shape[0]
    earlier = (lax.broadcasted_iota(I32, (ts, ts), 1) < lax.broadcasted_iota(I32, (ts, ts), 0))
    before = cnt_ref[0:1, :] + jnp.dot(earlier.astype(BF16), oh.astype(BF16), preferred_element_type=F32)
    r1 = jnp.sum(oh1 * before, axis=-1, keepdims=True)
    r2 = jnp.sum(oh2 * before, axis=-1, keepdims=True)
    cnt_ref[...] = cnt_ref[...] + jnp.sum(oh, axis=0, keepdims=True)
    ti_ref[...] = jnp.where(lane == 0, i1, jnp.where(lane == 1, i2, jnp.where(
        lane == 2, r1, jnp.where(lane == 3, r2, 0.0)))).astype(I32)


def _rmsmod(x, g, shift, scale, out_dtype, w_router=None):
    b, s, d = x.shape
    ts = _fit(s, 256, 8)
    specs = [pl.BlockSpec((None, ts, d), lambda i, j: (i, j, 0)),
             pl.BlockSpec((1, d), lambda i, j: (0, 0)),
             pl.BlockSpec((None, 1, d), lambda i, j: (i, 0, 0)),
             pl.BlockSpec((None, 1, d), lambda i, j: (i, 0, 0))]
    args = [x, g.reshape(1, d), shift.reshape(b, 1, d), scale.reshape(b, 1, d)]
    tile = pl.BlockSpec((None, ts, d), lambda i, j: (i, j, 0))
    blk = _nbytes((ts, d), F32) * 2 + 3 * _nbytes((1, d), F32)
    if w_router is None:
        return pl.pallas_call(
            _rmsmod_kernel, grid=(b, s // ts), in_specs=specs, out_specs=tile,
            out_shape=jax.ShapeDtypeStruct((b, s, d), out_dtype),
            compiler_params=_cparams(("arbitrary", "arbitrary"), blk, 4 * _nbytes((ts, d), F32)),
            name="rmsmod")(*args)
    n_exp = w_router.shape[1]
    wr = jnp.pad(w_router, ((0, 0), (0, V7X_LANES - n_exp)))
    wr_hi = wr.astype(BF16)
    wr_lo = (wr - wr_hi.astype(F32)).astype(BF16)
    lanes = pl.BlockSpec((None, ts, V7X_LANES), lambda i, j: (i, j, 0))
    blk += _nbytes((d, V7X_LANES), F32) + 2 * _nbytes((ts, V7X_LANES), F32)
    return pl.pallas_call(
        functools.partial(_rmsmod_route_kernel, n_exp=n_exp),
        grid=(b, s // ts),
        in_specs=specs + [pl.BlockSpec((d, V7X_LANES), lambda i, j: (0, 0))] * 2,
        out_specs=[tile, lanes, lanes, pl.BlockSpec((8, V7X_LANES), lambda i, j: (0, 0))],
        out_shape=[jax.ShapeDtypeStruct((b, s, d), F32),
                   jax.ShapeDtypeStruct((b, s, V7X_LANES), F32),
                   jax.ShapeDtypeStruct((b, s, V7X_LANES), I32),
                   jax.ShapeDtypeStruct((8, V7X_LANES), F32)],
        compiler_params=_cparams(("arbitrary", "arbitrary"), blk, 8 * _nbytes((ts, d), F32)),
        name="rmsmod_route")(*args, wr_hi, wr_lo)


def _ws_kernel(te_ref, tf_ref, tv_ref, tr_ref, tn_ref, wc_ref, *refs, nx, nw, ne, compute, dil, w_blocks, bn):
    del tr_ref
    x_refs = refs[:nx]
    w_refs = refs[nx:nx + nw]
    e_refs = refs[nx + nw:nx + nw + ne]
    o_ref = refs[nx + nw + ne]
    rest = refs[nx + nw + ne + 1:]
    wb_refs, stage_refs, sem = rest[:nw], rest[nw:2 * nw], rest[2 * nw]
    j = pl.program_id(0)
    t = pl.program_id(1)

    def weight_copy(i, expert, col_tile):
        k, kb = w_blocks[i]
        col = pl.multiple_of(wc_ref[col_tile] * bn, bn)
        return pltpu.make_async_copy(w_refs[i].at[expert, pl.ds(kb * k, k), pl.ds(col, bn)],
                                     stage_refs[i], sem.at[i])

    @pl.when(tf_ref[t] == 1)
    def _new_weights():
        @pl.when((j == 0) & (t == 0))
        def _first_block():
            for i in range(nw):
                weight_copy(i, te_ref[0], 0).start()

        for i in range(nw):
            weight_copy(i, te_ref[t], j).wait()
            wb_refs[i][...] = stage_refs[i][...].astype(BF16)

        nxt = tn_ref[t]

        @pl.when(nxt >= 0)
        def _next_run():
            for i in range(nw):
                weight_copy(i, te_ref[jnp.maximum(nxt, 0)], j).start()

        @pl.when((nxt < 0) & (j + 1 < pl.num_programs(0)))
        def _next_column_tile():
            for i in range(nw):
                weight_copy(i, te_ref[0], j + 1).start()

    @pl.when(tv_ref[t] == 1)
    def _compute():
        res = compute(x_refs, wb_refs, e_refs)
        if dil == 1:
            o_ref[...] = res.astype(o_ref.dtype)
        else:
            perm_ref = rest[-1]
            n_chunks, bm, lanes = perm_ref.shape
            rows = bm // dil
            for c in range(n_chunks):
                perm_ref[c] = res[:, c * lanes:(c + 1) * lanes]
            for r in range(dil):
                for c in range(n_chunks):
                    o_ref[r, :, c * lanes:(c + 1) * lanes] = \
                        perm_ref[c, pl.ds(r, rows, stride=dil), :].astype(o_ref.dtype)

    @pl.when(tv_ref[t] == 0)
    def _unused_tile():
        o_ref[...] = jnp.zeros(o_ref.shape, o_ref.dtype)


def _dense_sched(n_tiles, expert=0):
    first = np.zeros((n_tiles,), np.int32)
    first[0] = 1
    return (jnp.full((n_tiles,), expert, I32), jnp.asarray(first), jnp.ones((n_tiles,), I32),
            jnp.arange(n_tiles, dtype=I32), jnp.full((n_tiles,), -1, I32))


def _tile_extra(bm, bn, col_off):
    return pl.BlockSpec((bm, bn), lambda j, t, te, tf, tv, tr, *_: (tr[t], col_off + j))


def _batch_row_extra(bm, bn, rows_per_batch):
    return pl.BlockSpec((None, 1, bn), lambda j, t, te, tf, tv, tr, *_: ((tr[t] * bm) // rows_per_batch, 0, j))


def _ws_matmul(name, compute, xs, ws, extras, out_rows, out_cols, out_dtype, bm, bn, sched,
               w_cols=None, dil=1, rows_per_batch=None):
    n_tiles = sched[0].shape[0]
    nj = out_cols // bn
    if w_cols is None:
        w_cols = np.arange(nj)
    assert len(w_cols) == nj
    w_cols = jnp.asarray(np.asarray(w_cols, np.int32))
    in_specs, args, blk = [], [], 0
    for arr, k, kb in xs:
        in_specs.append(pl.BlockSpec((bm, k), lambda j, t, te, tf, tv, tr, *_, kb=kb: (tr[t], kb)))
        args.append(arr)
        blk += _nbytes((bm, k), arr.dtype)
    wb, stage, scratch_bytes = [], [], 0
    for arr, k, kb in ws:
        in_specs.append(pl.BlockSpec(memory_space=pl.ANY))
        args.append(arr)
        wb.append(pltpu.VMEM((k, bn), BF16))
        stage.append(pltpu.VMEM((k, bn), arr.dtype))
        scratch_bytes += _nbytes((k, bn), BF16) + _nbytes((k, bn), arr.dtype)
    scratch = wb + stage + [pltpu.SemaphoreType.DMA((len(ws),))]
    for arr, spec in extras:
        in_specs.append(spec)
        args.append(arr)
        blk += _nbytes([d for d in spec.block_shape if d is not None], arr.dtype)
    blk += _nbytes((bm, bn), out_dtype)
    scratch_bytes += 2 * _nbytes((bm, bn), F32)
    if dil == 1:
        out_spec = pl.BlockSpec((bm, bn), lambda j, t, *_: (t, j))
        out_shape = jax.ShapeDtypeStruct((out_rows, out_cols), out_dtype)
    else:
        tpb = rows_per_batch // bm
        assert rows_per_batch % bm == 0 and bm % dil == 0
        out_spec = pl.BlockSpec((None, dil, bm // dil, bn), lambda j, t, *_: (t // tpb, 0, t % tpb, j))
        out_shape = jax.ShapeDtypeStruct((out_rows // rows_per_batch, dil, rows_per_batch // dil, out_cols),
                                         out_dtype)
        assert bn % V7X_LANES == 0
        scratch.append(pltpu.VMEM((bn // V7X_LANES, bm, V7X_LANES), F32))
        scratch_bytes += _nbytes((bm, bn), F32)
    kern = functools.partial(_ws_kernel, nx=len(xs), nw=len(ws), ne=len(extras), compute=compute, dil=dil,
                             w_blocks=tuple((k, kb) for _, k, kb in ws), bn=bn)
    return pl.pallas_call(
        kern,
        grid_spec=pltpu.PrefetchScalarGridSpec(
            num_scalar_prefetch=6, grid=(nj, n_tiles), in_specs=in_specs,
            out_specs=out_spec, scratch_shapes=scratch),
        out_shape=out_shape,
        compiler_params=_cparams(("arbitrary", "arbitrary"), blk, scratch_bytes),
        name=name,
    )(*sched, w_cols, *args)


def _mm(x, wb_ref):
    return jnp.dot(x, wb_ref[...], preferred_element_type=F32)


def _plain_compute(xr, wr, er):
    return _mm(xr[0][...], wr[0])


def _swiglu_compute(xr, wr, er):
    x = xr[0][...]
    a = _mm(x, wr[0])
    return a * jax.nn.sigmoid(a) * _mm(x, wr[1])


def _merge_compute(xr, wr, er):
    conv_out = _mm(xr[0][...], wr[0])
    attn_out = _mm(xr[1][...], wr[1])
    gc = er[0][...].astype(F32)
    ga = er[1][...].astype(F32)
    return jax.nn.sigmoid(gc) * conv_out + jax.nn.sigmoid(ga) * attn_out


def _residual_compute(xr, wr, er):
    return er[0][...] + er[1][...] * _mm(xr[0][...], wr[0])


CONV_HALO = 32


def _conv_kernel(a_ref, b_ref, ah_ref, bh_ref, w_ref, cb_ref, g_ref, beta_ref, o_ref, ubuf, sbuf, cbuf,
                 *, ts, width, ch):
    i = pl.program_id(1)
    a = a_ref[...].astype(F32)
    b = b_ref[...].astype(F32)
    ubuf[CONV_HALO:CONV_HALO + ts, :] = a * jax.nn.sigmoid(b)
    ah = ah_ref[...].astype(F32)
    bh = bh_ref[...].astype(F32)
    ubuf[0:CONV_HALO, :] = jnp.where(i > 0, ah * jax.nn.sigmoid(bh), 0.0)
    span = sbuf.shape[1]
    for s in range(1, V7X_SUBLANES):
        sbuf[s - 1] = ubuf[s:s + span, :]
    off = CONV_HALO - (width - 1)
    rch = V7X_SUBLANES
    for r0 in range(0, ts, rch):
        acc = jnp.zeros((rch, ch), F32)
        for j in range(width):
            shift, row = (off + j) % V7X_SUBLANES, r0 + (off + j) // V7X_SUBLANES * V7X_SUBLANES
            src = ubuf if shift == 0 else sbuf.at[shift - 1]
            acc = acc + w_ref[j] * src[row:row + rch, :]
        cbuf[r0:r0 + rch, :] = acc + cb_ref[...]
    y = cbuf[...]
    mu = jnp.mean(y, axis=-1, keepdims=True)
    yc = y - mu
    var = jnp.mean(yc * yc, axis=-1, keepdims=True)
    yn = yc * lax.rsqrt(var + 1e-5) * g_ref[...] + beta_ref[...]
    o_ref[...] = (yn * jax.nn.sigmoid(yn)).astype(o_ref.dtype)


def _conv_branch(p3, conv_w, conv_b, ln_g, ln_b):
    b, s, _ = p3.shape
    width, ch = conv_w.shape
    assert width - 1 <= CONV_HALO
    ts = _fit(s, 128, CONV_HALO)
    hb = ts // CONV_HALO
    tile = lambda col: pl.BlockSpec((None, ts, ch), lambda bi, i, col=col: (bi, i, col))
    halo = lambda col: pl.BlockSpec((None, CONV_HALO, ch),
                                    lambda bi, i, col=col: (bi, jnp.maximum(i * hb - 1, 0), col))
    row = pl.BlockSpec((1, ch), lambda bi, i: (0, 0))
    blk = 2 * _nbytes((ts + CONV_HALO, ch), BF16) + _nbytes((width * V7X_SUBLANES + 3, ch), F32) + _nbytes((ts, ch), BF16)
    shifted = (V7X_SUBLANES - 1, ts + CONV_HALO - V7X_SUBLANES, ch)
    scratch_bytes = _nbytes((2 * ts + CONV_HALO, ch), F32) + _nbytes(shifted, F32)
    return pl.pallas_call(
        functools.partial(_conv_kernel, ts=ts, width=width, ch=ch),
        grid=(b, s // ts),
        in_specs=[tile(0), tile(1), halo(0), halo(1),
                  pl.BlockSpec((width, V7X_SUBLANES, ch), lambda bi, i: (0, 0, 0)), row, row, row],
        out_specs=pl.BlockSpec((None, ts, ch), lambda bi, i: (bi, i, 0)),
        out_shape=jax.ShapeDtypeStruct((b, s, ch), BF16),
        scratch_shapes=[pltpu.VMEM((ts + CONV_HALO, ch), F32), pltpu.VMEM(shifted, F32),
                        pltpu.VMEM((ts, ch), F32)],
        compiler_params=_cparams(("arbitrary", "arbitrary"), blk, scratch_bytes + 4 * _nbytes((ts, ch), F32)),
        name="conv_module",
    )(p3, p3, p3, p3, jnp.broadcast_to(conv_w[:, None, :], (width, V7X_SUBLANES, ch)),
      conv_b.reshape(1, ch), ln_g.reshape(1, ch), ln_b.reshape(1, ch))


def _bias_kernel(rel_ref, idx_ref, o_ref, *, n_buckets, nh, span, blk):
    g = pl.program_id(0)
    h = pl.program_id(1)
    idx = idx_ref[...]
    acc = jnp.zeros(idx.shape, F32)
    for bkt in range(n_buckets):
        acc = jnp.where(idx == bkt, rel_ref[bkt, g * nh + h], acc)
    qi = lax.broadcasted_iota(I32, idx.shape, 0) + blk
    kj = lax.broadcasted_iota(I32, idx.shape, 1)
    delta = qi - kj
    o_ref[...] = jnp.where((delta >= 0) & (delta <= span), acc, NEG_INF)


def _t5_bucket(dist, n_buckets):
    max_exact = n_buckets // 2
    n = jnp.maximum(dist, 1).astype(F32)
    large = max_exact + (jnp.log(n / max_exact) / math.log(MAX_DISTANCE / max_exact)
                         * (n_buckets - max_exact)).astype(I32)
    large = jnp.minimum(large, n_buckets - 1)
    return jnp.where(dist < max_exact, dist, large)


def _bias_tables(rel_bias, nh):
    n_buckets = rel_bias.shape[0]
    blk = ATTN_BLOCK
    span = ATTN_GROUPS[0][0] // ATTN_GROUPS[0][1]
    assert all(w // d == span for w, d in ATTN_GROUPS)
    delta = (jnp.arange(blk, dtype=I32)[:, None] + blk) - jnp.arange(2 * blk, dtype=I32)[None, :]
    idx = jnp.stack([_t5_bucket(jnp.maximum(delta, 0) * d, n_buckets) for _, d in ATTN_GROUPS])
    ng = len(ATTN_GROUPS)
    return pl.pallas_call(
        functools.partial(_bias_kernel, n_buckets=n_buckets, nh=nh, span=span, blk=blk),
        grid=(ng, nh),
        in_specs=[pl.BlockSpec(memory_space=pltpu.SMEM),
                  pl.BlockSpec((None, blk, 2 * blk), lambda g, h: (g, 0, 0))],
        out_specs=pl.BlockSpec((None, None, blk, 2 * blk), lambda g, h: (g, h, 0, 0)),
        out_shape=jax.ShapeDtypeStruct((ng, nh, blk, 2 * blk), F32),
        name="rel_bias_tables",
    )(rel_bias, idx)


ATTN_HEAD_PAIR = 2


def _attn_kernel(q_ref, k_ref, v_ref, qg_ref, kg_ref, bias_ref, seg_ref, o_ref, lse_ref, kprev, vprev,
                 qn_s, kn_s, s_s, p_s, m_s, *, nh, hd, blk):
    has_prev = pl.program_id(2) > 0
    scale = hd ** -0.5
    nt = (((1,), (1,)), ((), ()))
    pw = ATTN_HEAD_PAIR * hd
    seg = seg_ref[...]
    ones = jnp.ones((blk, hd), BF16)

    @pl.when(jnp.logical_not(has_prev))
    def _no_previous_block():
        kprev[...] = jnp.zeros(kprev.shape, kprev.dtype)
        vprev[...] = jnp.zeros(vprev.shape, vprev.dtype)

    def qk_norm(ref, c0, gain):
        x = ref[:, c0:c0 + pw].astype(F32)
        ss = jnp.dot((x * x).astype(BF16), seg, preferred_element_type=F32)
        return (x * lax.rsqrt(ss * (1.0 / hd) + 1e-6) * gain).astype(BF16)

    for c0 in range(0, nh * hd, pw):
        qn_s[:, c0:c0 + pw] = qk_norm(q_ref, c0, qg_ref[...])
        kn_s[:, c0:c0 + pw] = qk_norm(k_ref, c0, kg_ref[...])
    for h in range(nh):
        hs = slice(h * hd, (h + 1) * hd)
        bias = bias_ref[h]
        q = qn_s[:, hs]
        s_s[h, :, :blk] = lax.dot_general(q, kprev[:, hs], nt, preferred_element_type=F32) * scale \
            + jnp.where(has_prev, bias[:, :blk], NEG_INF)
        s_s[h, :, blk:] = lax.dot_general(q, kn_s[:, hs], nt, preferred_element_type=F32) * scale + bias[:, blk:]
    for h in range(nh):
        s = s_s[h]
        m = jnp.max(s, axis=-1, keepdims=True)
        m_s[h] = jnp.broadcast_to(m, (blk, hd))
        p_s[h] = jnp.exp(s - m).astype(BF16)
    assert hd == V7X_LANES and nh <= V7X_LANES
    head_lane = lax.broadcasted_iota(I32, (blk, hd), 1)
    lse = jnp.zeros((blk, hd), F32)
    for h in range(nh):
        hs = slice(h * hd, (h + 1) * hd)
        res = jnp.dot(p_s[h, :, :blk], jnp.concatenate([vprev[:, hs], ones], axis=1),
                      preferred_element_type=F32) \
            + jnp.dot(p_s[h, :, blk:], jnp.concatenate([v_ref[:, hs], ones], axis=1),
                      preferred_element_type=F32)
        l = res[:, hd:]
        o_ref[:, hs] = (res[:, :hd] / l).astype(o_ref.dtype)
        lse = jnp.where(head_lane == h, m_s[h] + jnp.log(l), lse)
    lse_ref[...] = lse
    kprev[...] = kn_s[...]
    vprev[...] = v_ref[...]


def _attn_group(pa, g, dil, qb, kb, vb, nh, hd, q_gain, k_gain, bias_g):
    b, s, _ = pa.shape
    blk = ATTN_BLOCK
    seq = s // dil
    assert s % dil == 0 and seq % blk == 0
    nb = seq // blk
    aw = nh * hd
    assert nh % ATTN_HEAD_PAIR == 0
    pw = ATTN_HEAD_PAIR * hd
    own = lambda c: pl.BlockSpec((None, blk, aw), lambda bi, r, n, c=c: (bi, r * nb + n, c))
    gain = pl.BlockSpec((1, pw), lambda bi, r, n: (0, 0))
    out_spec = pl.BlockSpec((None, blk, aw), lambda bi, r, n: (bi, r * nb + n, 0))
    lane_head = np.arange(pw) // hd
    seg = jnp.asarray((lane_head[:, None] == lane_head[None, :]).astype(np.float32), BF16)
    tile_gain = lambda gn: jnp.tile(gn, ATTN_HEAD_PAIR).reshape(1, pw)
    blk_bytes = 3 * _nbytes((blk, aw), BF16) + _nbytes((nh, blk, 2 * blk), F32) \
        + _nbytes((blk, aw), BF16) + _nbytes((blk, aw), F32) + _nbytes((pw, pw), BF16)
    o, lse = pl.pallas_call(
        functools.partial(_attn_kernel, nh=nh, hd=hd, blk=blk),
        grid=(b, dil, nb),
        in_specs=[own(qb), own(kb), own(vb), gain, gain,
                  pl.BlockSpec((nh, blk, 2 * blk), lambda bi, r, n: (0, 0, 0)),
                  pl.BlockSpec((pw, pw), lambda bi, r, n: (0, 0))],
        out_specs=[out_spec, pl.BlockSpec((None, blk, V7X_LANES), lambda bi, r, n: (bi, r * nb + n, 0))],
        out_shape=[jax.ShapeDtypeStruct((b, s, aw), BF16), jax.ShapeDtypeStruct((b, s, V7X_LANES), F32)],
        scratch_shapes=[pltpu.VMEM((blk, aw), BF16), pltpu.VMEM((blk, aw), BF16),
                        pltpu.VMEM((blk, aw), BF16), pltpu.VMEM((blk, aw), BF16),
                        pltpu.VMEM((nh, blk, 2 * blk), F32), pltpu.VMEM((nh, blk, 2 * blk), BF16),
                        pltpu.VMEM((nh, blk, hd), F32)],
        compiler_params=_cparams(("arbitrary", "arbitrary", "arbitrary"), blk_bytes,
                                 4 * _nbytes((blk, aw), BF16) + 2 * _nbytes((nh, blk, 2 * blk), F32)),
        name=f"dilated_attn_g{g}",
    )(pa, pa, pa, tile_gain(q_gain), tile_gain(k_gain), bias_g, seg)
    return o.reshape(b, dil, seq, aw), lse.reshape(b, dil, seq, V7X_LANES)


def _group_merge_kernel(*refs, dils, ts):
    ng = len(dils)
    o_refs, l_refs, expand_ref, out_ref = refs[:ng], refs[ng:2 * ng], refs[2 * ng], refs[2 * ng + 1]
    scratch = list(refs[2 * ng + 2:])

    def token_order(ref, dil):
        if dil == 1:
            return ref[0].astype(F32)
        buf = scratch.pop(0)
        n_chunks, _, lanes = buf.shape
        rows = ts // dil
        for r in range(dil):
            for c in range(n_chunks):
                buf[c, pl.ds(r, rows, stride=dil), :] = ref[r, :, c * lanes:(c + 1) * lanes].astype(F32)
        return jnp.concatenate([buf[c] for c in range(n_chunks)], axis=1)

    def spread(w):
        hi = w.astype(BF16)
        lo = (w - hi.astype(F32)).astype(BF16)
        return jnp.dot(hi, expand_ref[...], preferred_element_type=F32) \
            + jnp.dot(lo, expand_ref[...], preferred_element_type=F32)

    ls = [token_order(r, dil) for r, dil in zip(l_refs, dils)]
    m = functools.reduce(jnp.maximum, ls)
    es = [jnp.exp(l - m) for l in ls]
    den = functools.reduce(lambda a, c: a + c, es)
    num = None
    for e, r, dil in zip(es, o_refs, dils):
        term = spread(e / den) * token_order(r, dil)
        num = term if num is None else num + term
    out_ref[...] = num.astype(out_ref.dtype)


def _group_merge(outs, lses, dils, hd):
    b, _, _, aw = outs[0].shape
    s = outs[0].shape[1] * outs[0].shape[2]
    nh = aw // hd
    ts = _fit(s, 256, 16 * max(dils))
    spec = lambda dil, w: pl.BlockSpec((None, dil, ts // dil, w), lambda i, j: (i, 0, j, 0))
    n_perm = sum(1 for dil in dils if dil > 1)
    lane = np.arange(V7X_LANES)[:, None]
    expand = jnp.asarray((lane == (np.arange(aw)[None, :] // hd)).astype(np.float32), BF16)
    blk = len(dils) * (_nbytes((ts, aw), BF16) + _nbytes((ts, V7X_LANES), F32)) + _nbytes((ts, aw), BF16) \
        + _nbytes((V7X_LANES, aw), BF16)
    assert nh <= V7X_LANES
    return pl.pallas_call(
        functools.partial(_group_merge_kernel, dils=tuple(dils), ts=ts),
        grid=(b, s // ts),
        in_specs=[spec(dil, aw) for dil in dils] + [spec(dil, V7X_LANES) for dil in dils]
        + [pl.BlockSpec((V7X_LANES, aw), lambda i, j: (0, 0))],
        out_specs=pl.BlockSpec((None, ts, aw), lambda i, j: (i, j, 0)),
        out_shape=jax.ShapeDtypeStruct((b, s, aw), BF16),
        scratch_shapes=[pltpu.VMEM((1, ts, V7X_LANES), F32)] * n_perm
        + [pltpu.VMEM((aw // V7X_LANES, ts, V7X_LANES), F32)] * n_perm,
        compiler_params=_cparams(("arbitrary", "arbitrary"), blk, (n_perm + 4) * _nbytes((ts, aw), F32)),
        name="attn_group_merge")(*outs, *lses, expand)


def _row_copy(src_hbm, row, dst, slot, sem):
    return pltpu.make_async_copy(src_hbm.at[pl.ds(row, 1), :], dst.at[pl.ds(slot, 1), :], sem)


ROW_DMA_UNROLL = 8
DMA_PRIORITIES = 2


def _gather_kernel(src_ref, h_hbm, o_ref, buf, sem, *, rows):
    t = pl.program_id(0)
    slot = t % 2

    def fetch(step, dst_slot):
        def start(i, carry):
            for p in range(DMA_PRIORITIES):
                row = i * DMA_PRIORITIES + p
                _row_copy(h_hbm, src_ref[step * rows + row], buf.at[dst_slot], row,
                          sem.at[dst_slot]).start(priority=p)
            return carry
        lax.fori_loop(0, rows // DMA_PRIORITIES, start, 0, unroll=ROW_DMA_UNROLL // DMA_PRIORITIES)

    @pl.when(t == 0)
    def _first():
        fetch(0, 0)

    @pl.when(t + 1 < pl.num_programs(0))
    def _next():
        fetch(t + 1, 1 - slot)

    def wait(i, carry):
        _row_copy(h_hbm, 0, buf.at[slot], i, sem.at[slot]).wait()
        return carry

    lax.fori_loop(0, rows, wait, 0, unroll=ROW_DMA_UNROLL)
    o_ref[...] = buf[slot].astype(o_ref.dtype)


def _moe_gather(h2, src, rows):
    r_total = src.shape[0]
    d = h2.shape[1]
    return pl.pallas_call(
        functools.partial(_gather_kernel, rows=rows),
        grid_spec=pltpu.PrefetchScalarGridSpec(
            num_scalar_prefetch=1, grid=(r_total // rows,),
            in_specs=[pl.BlockSpec(memory_space=pl.ANY)],
            out_specs=pl.BlockSpec((rows, d), lambda t, src: (t, 0)),
            scratch_shapes=[pltpu.VMEM((2, rows, d), F32), pltpu.SemaphoreType.DMA((2,))]),
        out_shape=jax.ShapeDtypeStruct((r_total, d), BF16),
        compiler_params=_cparams(("arbitrary",), _nbytes((rows, d), BF16), 3 * _nbytes((rows, d), F32)),
        name="moe_gather",
    )(src, h2)


def _combine_kernel(pos_ref, y_hbm, x_ref, tw_ref, gt_ref, o_ref, buf, sem, *, rows):
    t = pl.program_id(0)
    slot = t % 2

    def fetch(step, dst_slot):
        def start(i, carry):
            for k in range(TOP_K):
                _row_copy(y_hbm, pos_ref[(step * rows + i) * TOP_K + k], buf.at[dst_slot, k], i,
                          sem.at[dst_slot]).start(priority=k % DMA_PRIORITIES)
            return carry
        lax.fori_loop(0, rows, start, 0, unroll=ROW_DMA_UNROLL // TOP_K)

    @pl.when(t == 0)
    def _first():
        fetch(0, 0)

    @pl.when(t + 1 < pl.num_programs(0))
    def _next():
        fetch(t + 1, 1 - slot)

    def wait(i, carry):
        for k in range(TOP_K):
            _row_copy(y_hbm, 0, buf.at[slot, k], i, sem.at[slot]).wait()
        return carry

    lax.fori_loop(0, rows, wait, 0, unroll=ROW_DMA_UNROLL // TOP_K)
    tw = tw_ref[...]
    y = tw[:, 0:1] * buf[slot, 0]
    for k in range(1, TOP_K):
        y = y + tw[:, k:k + 1] * buf[slot, k]
    o_ref[...] = x_ref[...] + gt_ref[...] * y


def _moe_combine(y_sorted, pos, x2, topw2, gate, rows_per_batch):
    m, d = x2.shape
    rows = _fit(rows_per_batch, 128, 8)
    blk = 2 * _nbytes((rows, d), F32) + _nbytes((rows, V7X_LANES), F32) + _nbytes((1, d), F32)
    return pl.pallas_call(
        functools.partial(_combine_kernel, rows=rows),
        grid_spec=pltpu.PrefetchScalarGridSpec(
            num_scalar_prefetch=1, grid=(m // rows,),
            in_specs=[pl.BlockSpec(memory_space=pl.ANY),
                      pl.BlockSpec((rows, d), lambda t, pos: (t, 0)),
                      pl.BlockSpec((rows, V7X_LANES), lambda t, pos: (t, 0)),
                      pl.BlockSpec((None, 1, d), lambda t, pos: ((t * rows) // rows_per_batch, 0, 0))],
            out_specs=pl.BlockSpec((rows, d), lambda t, pos: (t, 0)),
            scratch_shapes=[pltpu.VMEM((2, TOP_K, rows, d), F32), pltpu.SemaphoreType.DMA((2,))]),
        out_shape=jax.ShapeDtypeStruct((m, d), F32),
        compiler_params=_cparams(("arbitrary",), blk, (2 * TOP_K + 2) * _nbytes((rows, d), F32)),
        name="moe_combine",
    )(pos, y_sorted, x2, topw2, gate)


def _moe_schedule(route_i, counts, n_exp, bm, expert_base):
    m = route_i.shape[0]
    n_assign = m * TOP_K
    n_tiles = n_assign // bm + n_exp
    e = route_i[:, :TOP_K].reshape(n_assign)
    rank = route_i[:, TOP_K:2 * TOP_K].reshape(n_assign)
    onehot = (e[:, None] == jnp.arange(n_exp, dtype=I32)[None, :]).astype(I32)
    tiles_per = (counts + bm - 1) // bm
    tile_end = jnp.cumsum(tiles_per)
    row_start = (tile_end - tiles_per) * bm
    pos = jnp.sum(onehot * row_start[None, :], axis=1) + rank
    used = tile_end[-1]
    t = jnp.arange(n_tiles, dtype=I32)
    tr = jnp.minimum(t, used - 1)
    te = jnp.minimum(jnp.sum((tr[:, None] >= tile_end[None, :]).astype(I32), axis=1), n_exp - 1)
    tv = (t < used).astype(I32)
    prev_e = jnp.concatenate([jnp.full((1,), -1, I32), te[:-1]])
    tf = tv * ((te != prev_e).astype(I32))
    later_start = (t[None, :] > t[:, None]) & (tf[None, :] == 1)
    tn = jnp.min(jnp.where(later_start, t[None, :], n_tiles), axis=1)
    tn = jnp.where(tn == n_tiles, -1, tn)
    src = (jnp.arange(n_tiles * bm, dtype=I32) % m).at[pos].set(jnp.arange(n_assign, dtype=I32) // TOP_K)
    sched = (te.astype(I32) + expert_base, tf.astype(I32), tv, tr.astype(I32), tn.astype(I32))
    return sched, pos.astype(I32), src


def kernel(x, c, w_ada, b_ada, g_mix, g_ffn, w_in, conv_w, conv_b, conv_ln_g, conv_ln_b, w_conv_out,
           q_gain, k_gain, rel_bias, w_attn_out, w_out, ffn_w1, ffn_w3, ffn_w2, moe_router, moe_w1,
           moe_w3, moe_w2):
    b, s, d = x.shape
    m = b * s
    depth = w_ada.shape[0]
    in_w = w_in.shape[2]
    ch = conv_w.shape[2]
    hd = q_gain.shape[1]
    att_out_w = w_attn_out.shape[1]
    nh = att_out_w // hd
    ng = len(ATTN_GROUPS)
    att_w = rel_bias.shape[1] * hd
    q_off = 2 * ch
    gate_off = q_off + 3 * att_w
    assert in_w == gate_off + 2 * d and rel_bias.shape[1] == ng * nh

    bm = _fit(s, 1024)
    stack = lambda w: w.reshape((w.shape[0] * w.shape[1],) + w.shape[2:])
    bn = _fit(d, 512)
    aw = nh * hd
    dils = [dil for _, dil in ATTN_GROUPS]
    assert dils[0] == 1 and q_off % bn == 0 and aw % bn == 0 and q_off % aw == 0

    cb = lambda start, width, n=bn: list(range(start // n, (start + width) // n))
    qkv_cols = lambda g, n=bn: sum([cb(q_off + i * att_w + g * aw, aw, n) for i in range(3)], [])
    bn_in = _fit(math.gcd(q_off, aw, d), 1024)
    nat_cols = cb(0, q_off, bn_in) + qkv_cols(0, bn_in) + cb(gate_off, 2 * d, bn_in)
    nat_gate = q_off + 3 * aw
    bmh, bnw = _fit(s, 512), _fit(math.gcd(d, nat_gate), 1024)

    mod = _adaln(c, w_ada, b_ada)
    bias = _bias_tables(rel_bias, nh)

    for l in range(depth):
        sh1, sc1, gt1, sh2, sc2, gt2 = [mod[l, :, i * d:(i + 1) * d] for i in range(6)]
        gt1 = gt1.reshape(b, 1, d)
        gt2 = gt2.reshape(b, 1, d)

        h = _rmsmod(x, g_mix[l], sh1, sc1, BF16).reshape(m, d)
        sched = _dense_sched(m // bm, l)
        schedh = _dense_sched(m // bmh, l)
        p = _ws_matmul("in_proj", _plain_compute, [(h, d, 0)], [(w_in, d, 0)], [],
                       m, len(nat_cols) * bn_in, BF16, bm, bn_in, sched, w_cols=nat_cols)
        p3 = p.reshape(b, s, len(nat_cols) * bn_in)
        cu = _conv_branch(p3, conv_w[l], conv_b[l], conv_ln_g[l], conv_ln_b[l]).reshape(m, ch)
        outs, lses = [], []
        for g, dil in enumerate(dils):
            if dil == 1:
                pa, qb = p3, q_off // aw
            else:
                pa = _ws_matmul(f"in_proj_g{g}", _plain_compute, [(h, d, 0)], [(w_in, d, 0)], [],
                                m, 3 * aw, BF16, bmh, bn_in, schedh, w_cols=qkv_cols(g, bn_in), dil=dil,
                                rows_per_batch=s).reshape(b, s, 3 * aw)
                qb = 0
            o_g, lse_g = _attn_group(pa, g, dil, qb, qb + 1, qb + 2, nh, hd, q_gain[l], k_gain[l], bias[g])
            outs.append(o_g)
            lses.append(lse_g)
        o = _group_merge(outs, lses, dils, hd).reshape(m, att_out_w)
        merged = _ws_matmul(
            "branch_merge", _merge_compute, [(cu, ch, 0), (o, att_out_w, 0)],
            [(w_conv_out, ch, 0), (w_attn_out, att_out_w, 0)],
            [(p, _tile_extra(bmh, bnw, nat_gate // bnw)), (p, _tile_extra(bmh, bnw, (nat_gate + d) // bnw))],
            m, d, BF16, bmh, bnw, schedh)
        x2 = _ws_matmul(
            "out_proj", _residual_compute, [(merged, d, 0)], [(w_out, d, 0)],
            [(x.reshape(m, d), _tile_extra(bmh, bnw, 0)), (gt1, _batch_row_extra(bmh, bnw, s))],
            m, d, F32, bmh, bnw, schedh)
        x = x2.reshape(b, s, d)

        j = l // 2
        if l % 2 == 0:
            h = _rmsmod(x, g_ffn[l], sh2, sc2, BF16).reshape(m, d)
            d_ff = ffn_w1.shape[2]
            hid = _ws_matmul("ffn_up", _swiglu_compute, [(h, d, 0)],
                             [(ffn_w1, d, 0), (ffn_w3, d, 0)], [],
                             m, d_ff, BF16, bm, _fit(d_ff, 256), _dense_sched(m // bm, j))
            ksplit = 2 if d_ff % (2 * V7X_LANES) == 0 else 1
            kc = d_ff // ksplit
            bm2 = _fit(s, 1024)
            sched2 = _dense_sched(m // bm2, j)
            for kb in range(ksplit):
                x2 = _ws_matmul(
                    "ffn_down", _residual_compute, [(hid, kc, kb)], [(ffn_w2, kc, kb)],
                    [(x2, _tile_extra(bm2, bn, 0)), (gt2, _batch_row_extra(bm2, bn, s))],
                    m, d, F32, bm2, bn, sched2)
            x = x2.reshape(b, s, d)
        else:
            n_exp = moe_router.shape[2]
            d_fe = moe_w1.shape[3]
            hf, topw, topi, cnt = _rmsmod(x, g_ffn[l], sh2, sc2, F32, w_router=moe_router[j])
            bme = _fit(m, 512)
            msched, pos, src = _moe_schedule(topi.reshape(m, V7X_LANES), cnt[0, :n_exp].astype(I32), n_exp, bme,
                                             j * n_exp)
            n_rows = src.shape[0]
            xs = _moe_gather(hf.reshape(m, d), src, _fit(bme, 256, 16))
            hid = _ws_matmul("moe_up", _swiglu_compute, [(xs, d, 0)],
                             [(stack(moe_w1), d, 0), (stack(moe_w3), d, 0)], [],
                             n_rows, d_fe, BF16, bme, _fit(d_fe, 512), msched)
            ys = _ws_matmul("moe_down", _plain_compute, [(hid, d_fe, 0)], [(stack(moe_w2), d_fe, 0)], [],
                            n_rows, d, F32, bme, _fit(d, 1024), msched)
            x = _moe_combine(ys, pos, x2, topw.reshape(m, V7X_LANES), gt2, s).reshape(b, s, d)
    return x
```

```python
import functools
import math

import numpy as np
import jax
import jax.numpy as jnp
from jax import lax
from jax.experimental import pallas as pl
from jax.experimental.pallas import tpu as pltpu

F32 = jnp.float32
BF16 = jnp.bfloat16
I32 = jnp.int32

ATTN_GROUPS = ((128, 1), (512, 4), (2048, 16))
ATTN_BLOCK = 128
MAX_DISTANCE = 2048
TOP_K = 2

V7X_VMEM_BYTES = 64 * 1024 * 1024
V7X_LANES = 128
V7X_SUBLANES = 8
VMEM_REQUEST_CAP = V7X_VMEM_BYTES - 6 * 1024 * 1024
VMEM_SLACK = 8 * 1024 * 1024
NEG_INF = float("-inf")


def _fit(n, pref, unit=V7X_LANES):
    best = None
    for d in range(unit, min(n, pref) + 1, unit):
        if n % d == 0:
            best = d
    return best if best is not None else n


def _nbytes(shape, dtype):
    return int(np.prod(shape)) * jnp.dtype(dtype).itemsize


def _cparams(semantics, block_bytes, scratch_bytes=0):
    limit = min(VMEM_REQUEST_CAP, 2 * block_bytes + scratch_bytes + VMEM_SLACK)
    return pltpu.CompilerParams(dimension_semantics=semantics, vmem_limit_bytes=int(limit))


def _ada_kernel(c_ref, w_ref, b_ref, o_ref):
    c = c_ref[...]
    ca = (c * jax.nn.sigmoid(c)).astype(BF16)
    o_ref[...] = jnp.dot(ca, w_ref[...].astype(BF16), preferred_element_type=F32) + b_ref[...]


def _adaln(c, w_ada, b_ada):
    depth, d, n = w_ada.shape
    b = c.shape[0]
    rows = 8
    c8 = jnp.pad(c, ((0, rows - b), (0, 0)))
    bn = _fit(n, 1024)
    blk = _nbytes((rows, d), F32) + _nbytes((d, bn), F32) + 2 * _nbytes((rows, bn), F32)
    out = pl.pallas_call(
        _ada_kernel,
        grid=(depth, n // bn),
        in_specs=[pl.BlockSpec((rows, d), lambda l, j: (0, 0)),
                  pl.BlockSpec((None, d, bn), lambda l, j: (l, 0, j)),
                  pl.BlockSpec((None, 1, bn), lambda l, j: (l, 0, j))],
        out_specs=pl.BlockSpec((None, rows, bn), lambda l, j: (l, 0, j)),
        out_shape=jax.ShapeDtypeStruct((depth, rows, n), F32),
        compiler_params=_cparams(("arbitrary", "arbitrary"), blk, _nbytes((d, bn), BF16)),
        name="adaln",
    )(c8, w_ada, b_ada.reshape(depth, 1, n))
    return out[:, :b]


NORM_ROWS = 16
NORM_COLS = 512


def _normed_into(x_ref, g_ref, sh_ref, sc_ref, o_ref):
    ts, d = x_ref.shape
    rows, cols = min(NORM_ROWS, ts), min(NORM_COLS, d)
    for r0 in range(0, ts, rows):
        sq = jnp.zeros((rows, cols), F32)
        for c0 in range(0, d, cols):
            x = x_ref[r0:r0 + rows, c0:c0 + cols]
            sq = sq + x * x
        inv = lax.rsqrt(jnp.sum(sq, axis=-1, keepdims=True) * (1.0 / d) + 1e-6)
        for c0 in range(0, d, cols):
            cs = slice(c0, c0 + cols)
            y = x_ref[r0:r0 + rows, cs] * inv * g_ref[:, cs]
            o_ref[r0:r0 + rows, cs] = (y * (1.0 + sc_ref[:, cs]) + sh_ref[:, cs]).astype(o_ref.dtype)


def _rmsmod_kernel(x_ref, g_ref, sh_ref, sc_ref, o_ref):
    _normed_into(x_ref, g_ref, sh_ref, sc_ref, o_ref)


def _rmsmod_route_kernel(x_ref, g_ref, sh_ref, sc_ref, whi_ref, wlo_ref, h_ref, tw_ref, ti_ref, cnt_ref,
                         *, n_exp):
    _normed_into(x_ref, g_ref, sh_ref, sc_ref, h_ref)
    h = h_ref[...]
    h_hi = h.astype(BF16)
    h_lo = (h - h_hi.astype(F32)).astype(BF16)
    logits = jnp.dot(h_hi, whi_ref[...], preferred_element_type=F32) \
        + (jnp.dot(h_lo, whi_ref[...], preferred_element_type=F32)
           + jnp.dot(h_hi, wlo_ref[...], preferred_element_type=F32))
    lane = lax.broadcasted_iota(I32, logits.shape, 1)
    lanef = lane.astype(F32)
    sentinel = float(V7X_LANES)
    lg = jnp.where(lane < n_exp, logits, NEG_INF)
    m1 = jnp.max(lg, axis=-1, keepdims=True)
    i1 = jnp.min(jnp.where(lg == m1, lanef, sentinel), axis=-1, keepdims=True)
    lg2 = jnp.where(lanef == i1, NEG_INF, lg)
    m2 = jnp.max(lg2, axis=-1, keepdims=True)
    i2 = jnp.min(jnp.where(lg2 == m2, lanef, sentinel), axis=-1, keepdims=True)
    e2 = jnp.exp(m2 - m1)
    den = 1.0 + e2
    tw_ref[...] = jnp.where(lane == 0, 1.0 / den, jnp.where(lane == 1, e2 / den, 0.0))

    @pl.when((pl.program_id(0) == 0) & (pl.program_id(1) == 0))
    def _init_counts():
        cnt_ref[...] = jnp.zeros(cnt_ref.shape, F32)

    oh1 = (lanef == i1).astype(F32)
    oh2 = (lanef == i2).astype(F32)
    oh = oh1 + oh2
    ts = oh.shape[0]
    earlier = (lax.broadcasted_iota(I32, (ts, ts), 1) < lax.broadcasted_iota(I32, (ts, ts), 0))
    before = cnt_ref[0:1, :] + jnp.dot(earlier.astype(BF16), oh.astype(BF16), preferred_element_type=F32)
    r1 = jnp.sum(oh1 * before, axis=-1, keepdims=True)
    r2 = jnp.sum(oh2 * before, axis=-1, keepdims=True)
    cnt_ref[...] = cnt_ref[...] + jnp.sum(oh, axis=0, keepdims=True)
    ti_ref[...] = jnp.where(lane == 0, i1, jnp.where(lane == 1, i2, jnp.where(
        lane == 2, r1, jnp.where(lane == 3, r2, 0.0)))).astype(I32)


def _rmsmod(x, g, shift, scale, out_dtype, w_router=None):
    b, s, d = x.shape
    ts = _fit(s, 256, 8)
    specs = [pl.BlockSpec((None, ts, d), lambda i, j: (i, j, 0)),
             pl.BlockSpec((1, d), lambda i, j: (0, 0)),
             pl.BlockSpec((None, 1, d), lambda i, j: (i, 0, 0)),
             pl.BlockSpec((None, 1, d), lambda i, j: (i, 0, 0))]
    args = [x, g.reshape(1, d), shift.reshape(b, 1, d), scale.reshape(b, 1, d)]
    tile = pl.BlockSpec((None, ts, d), lambda i, j: (i, j, 0))
    blk = _nbytes((ts, d), F32) * 2 + 3 * _nbytes((1, d), F32)
    if w_router is None:
        return pl.pallas_call(
            _rmsmod_kernel, grid=(b, s // ts), in_specs=specs, out_specs=tile,
            out_shape=jax.ShapeDtypeStruct((b, s, d), out_dtype),
            compiler_params=_cparams(("arbitrary", "arbitrary"), blk, 4 * _nbytes((ts, d), F32)),
            name="rmsmod")(*args)
    n_exp = w_router.shape[1]
    wr = jnp.pad(w_router, ((0, 0), (0, V7X_LANES - n_exp)))
    wr_hi = wr.astype(BF16)
    wr_lo = (wr - wr_hi.astype(F32)).astype(BF16)
    lanes = pl.BlockSpec((None, ts, V7X_LANES), lambda i, j: (i, j, 0))
    blk += _nbytes((d, V7X_LANES), F32) + 2 * _nbytes((ts, V7X_LANES), F32)
    return pl.pallas_call(
        functools.partial(_rmsmod_route_kernel, n_exp=n_exp),
        grid=(b, s // ts),
        in_specs=specs + [pl.BlockSpec((d, V7X_LANES), lambda i, j: (0, 0))] * 2,
        out_specs=[tile, lanes, lanes, pl.BlockSpec((8, V7X_LANES), lambda i, j: (0, 0))],
        out_shape=[jax.ShapeDtypeStruct((b, s, d), F32),
                   jax.ShapeDtypeStruct((b, s, V7X_LANES), F32),
                   jax.ShapeDtypeStruct((b, s, V7X_LANES), I32),
                   jax.ShapeDtypeStruct((8, V7X_LANES), F32)],
        compiler_params=_cparams(("arbitrary", "arbitrary"), blk, 8 * _nbytes((ts, d), F32)),
        name="rmsmod_route")(*args, wr_hi, wr_lo)


def _ws_kernel(te_ref, tf_ref, tv_ref, tr_ref, tn_ref, wc_ref, *refs, nx, nw, ne, compute, dil, w_blocks, bn):
    del tr_ref
    x_refs = refs[:nx]
    w_refs = refs[nx:nx + nw]
    e_refs = refs[nx + nw:nx + nw + ne]
    o_ref = refs[nx + nw + ne]
    rest = refs[nx + nw + ne + 1:]
    wb_refs, stage_refs, sem = rest[:nw], rest[nw:2 * nw], rest[2 * nw]
    j = pl.program_id(0)
    t = pl.program_id(1)

    def weight_copy(i, expert, col_tile):
        k, kb = w_blocks[i]
        col = pl.multiple_of(wc_ref[col_tile] * bn, bn)
        return pltpu.make_async_copy(w_refs[i].at[expert, pl.ds(kb * k, k), pl.ds(col, bn)],
                                     stage_refs[i], sem.at[i])

    @pl.when(tf_ref[t] == 1)
    def _new_weights():
        @pl.when((j == 0) & (t == 0))
        def _first_block():
            for i in range(nw):
                weight_copy(i, te_ref[0], 0).start()

        for i in range(nw):
            weight_copy(i, te_ref[t], j).wait()
            wb_refs[i][...] = stage_refs[i][...].astype(BF16)

        nxt = tn_ref[t]

        @pl.when(nxt >= 0)
        def _next_run():
            for i in range(nw):
                weight_copy(i, te_ref[jnp.maximum(nxt, 0)], j).start()

        @pl.when((nxt < 0) & (j + 1 < pl.num_programs(0)))
        def _next_column_tile():
            for i in range(nw):
                weight_copy(i, te_ref[0], j + 1).start()

    @pl.when(tv_ref[t] == 1)
    def _compute():
        res = compute(x_refs, wb_refs, e_refs)
        if dil == 1:
            o_ref[...] = res.astype(o_ref.dtype)
        else:
            perm_ref = rest[-1]
            n_chunks, bm, lanes = perm_ref.shape
            rows = bm // dil
            for c in range(n_chunks):
                perm_ref[c] = res[:, c * lanes:(c + 1) * lanes]
            for r in range(dil):
                for c in range(n_chunks):
                    o_ref[r, :, c * lanes:(c + 1) * lanes] = \
                        perm_ref[c, pl.ds(r, rows, stride=dil), :].astype(o_ref.dtype)

    @pl.when(tv_ref[t] == 0)
    def _unused_tile():
        o_ref[...] = jnp.zeros(o_ref.shape, o_ref.dtype)


def _dense_sched(n_tiles, expert=0):
    first = np.zeros((n_tiles,), np.int32)
    first[0] = 1
    return (jnp.full((n_tiles,), expert, I32), jnp.asarray(first), jnp.ones((n_tiles,), I32),
            jnp.arange(n_tiles, dtype=I32), jnp.full((n_tiles,), -1, I32))


def _tile_extra(bm, bn, col_off):
    return pl.BlockSpec((bm, bn), lambda j, t, te, tf, tv, tr, *_: (tr[t], col_off + j))


def _batch_row_extra(bm, bn, rows_per_batch):
    return pl.BlockSpec((None, 1, bn), lambda j, t, te, tf, tv, tr, *_: ((tr[t] * bm) // rows_per_batch, 0, j))


def _ws_matmul(name, compute, xs, ws, extras, out_rows, out_cols, out_dtype, bm, bn, sched,
               w_cols=None, dil=1, rows_per_batch=None):
    n_tiles = sched[0].shape[0]
    nj = out_cols // bn
    if w_cols is None:
        w_cols = np.arange(nj)
    assert len(w_cols) == nj
    w_cols = jnp.asarray(np.asarray(w_cols, np.int32))
    in_specs, args, blk = [], [], 0
    for arr, k, kb in xs:
        in_specs.append(pl.BlockSpec((bm, k), lambda j, t, te, tf, tv, tr, *_, kb=kb: (tr[t], kb)))
        args.append(arr)
        blk += _nbytes((bm, k), arr.dtype)
    wb, stage, scratch_bytes = [], [], 0
    for arr, k, kb in ws:
        in_specs.append(pl.BlockSpec(memory_space=pl.ANY))
        args.append(arr)
        wb.append(pltpu.VMEM((k, bn), BF16))
        stage.append(pltpu.VMEM((k, bn), arr.dtype))
        scratch_bytes += _nbytes((k, bn), BF16) + _nbytes((k, bn), arr.dtype)
    scratch = wb + stage + [pltpu.SemaphoreType.DMA((len(ws),))]
    for arr, spec in extras:
        in_specs.append(spec)
        args.append(arr)
        blk += _nbytes([d for d in spec.block_shape if d is not None], arr.dtype)
    blk += _nbytes((bm, bn), out_dtype)
    scratch_bytes += 2 * _nbytes((bm, bn), F32)
    if dil == 1:
        out_spec = pl.BlockSpec((bm, bn), lambda j, t, *_: (t, j))
        out_shape = jax.ShapeDtypeStruct((out_rows, out_cols), out_dtype)
    else:
        tpb = rows_per_batch // bm
        assert rows_per_batch % bm == 0 and bm % dil == 0
        out_spec = pl.BlockSpec((None, dil, bm // dil, bn), lambda j, t, *_: (t // tpb, 0, t % tpb, j))
        out_shape = jax.ShapeDtypeStruct((out_rows // rows_per_batch, dil, rows_per_batch // dil, out_cols),
                                         out_dtype)
        assert bn % V7X_LANES == 0
        scratch.append(pltpu.VMEM((bn // V7X_LANES, bm, V7X_LANES), F32))
        scratch_bytes += _nbytes((bm, bn), F32)
    kern = functools.partial(_ws_kernel, nx=len(xs), nw=len(ws), ne=len(extras), compute=compute, dil=dil,
                             w_blocks=tuple((k, kb) for _, k, kb in ws), bn=bn)
    return pl.pallas_call(
        kern,
        grid_spec=pltpu.PrefetchScalarGridSpec(
            num_scalar_prefetch=6, grid=(nj, n_tiles), in_specs=in_specs,
            out_specs=out_spec, scratch_shapes=scratch),
        out_shape=out_shape,
        compiler_params=_cparams(("arbitrary", "arbitrary"), blk, scratch_bytes),
        name=name,
    )(*sched, w_cols, *args)


def _mm(x, wb_ref):
    return jnp.dot(x, wb_ref[...], preferred_element_type=F32)


def _plain_compute(xr, wr, er):
    return _mm(xr[0][...], wr[0])


def _swiglu_compute(xr, wr, er):
    x = xr[0][...]
    a = _mm(x, wr[0])
    return a * jax.nn.sigmoid(a) * _mm(x, wr[1])


def _merge_compute(xr, wr, er):
    conv_out = _mm(xr[0][...], wr[0])
    attn_out = _mm(xr[1][...], wr[1])
    gc = er[0][...].astype(F32)
    ga = er[1][...].astype(F32)
    return jax.nn.sigmoid(gc) * conv_out + jax.nn.sigmoid(ga) * attn_out


def _residual_compute(xr, wr, er):
    return er[0][...] + er[1][...] * _mm(xr[0][...], wr[0])


CONV_HALO = 32


def _conv_kernel(a_ref, b_ref, ah_ref, bh_ref, w_ref, cb_ref, g_ref, beta_ref, o_ref, ubuf, sbuf, cbuf,
                 *, ts, width, ch):
    i = pl.program_id(1)
    a = a_ref[...].astype(F32)
    b = b_ref[...].astype(F32)
    ubuf[CONV_HALO:CONV_HALO + ts, :] = a * jax.nn.sigmoid(b)
    ah = ah_ref[...].astype(F32)
    bh = bh_ref[...].astype(F32)
    ubuf[0:CONV_HALO, :] = jnp.where(i > 0, ah * jax.nn.sigmoid(bh), 0.0)
    span = sbuf.shape[1]
    for s in range(1, V7X_SUBLANES):
        sbuf[s - 1] = ubuf[s:s + span, :]
    off = CONV_HALO - (width - 1)
    rch = V7X_SUBLANES
    for r0 in range(0, ts, rch):
        acc = jnp.zeros((rch, ch), F32)
        for j in range(width):
            shift, row = (off + j) % V7X_SUBLANES, r0 + (off + j) // V7X_SUBLANES * V7X_SUBLANES
            src = ubuf if shift == 0 else sbuf.at[shift - 1]
            acc = acc + w_ref[j] * src[row:row + rch, :]
        cbuf[r0:r0 + rch, :] = acc + cb_ref[...]
    y = cbuf[...]
    mu = jnp.mean(y, axis=-1, keepdims=True)
    yc = y - mu
    var = jnp.mean(yc * yc, axis=-1, keepdims=True)
    yn = yc * lax.rsqrt(var + 1e-5) * g_ref[...] + beta_ref[...]
    o_ref[...] = (yn * jax.nn.sigmoid(yn)).astype(o_ref.dtype)


def _conv_branch(p3, conv_w, conv_b, ln_g, ln_b):
    b, s, _ = p3.shape
    width, ch = conv_w.shape
    assert width - 1 <= CONV_HALO
    ts = _fit(s, 128, CONV_HALO)
    hb = ts // CONV_HALO
    tile = lambda col: pl.BlockSpec((None, ts, ch), lambda bi, i, col=col: (bi, i, col))
    halo = lambda col: pl.BlockSpec((None, CONV_HALO, ch),
                                    lambda bi, i, col=col: (bi, jnp.maximum(i * hb - 1, 0), col))
    row = pl.BlockSpec((1, ch), lambda bi, i: (0, 0))
    blk = 2 * _nbytes((ts + CONV_HALO, ch), BF16) + _nbytes((width * V7X_SUBLANES + 3, ch), F32) + _nbytes((ts, ch), BF16)
    shifted = (V7X_SUBLANES - 1, ts + CONV_HALO - V7X_SUBLANES, ch)
    scratch_bytes = _nbytes((2 * ts + CONV_HALO, ch), F32) + _nbytes(shifted, F32)
    return pl.pallas_call(
        functools.partial(_conv_kernel, ts=ts, width=width, ch=ch),
        grid=(b, s // ts),
        in_specs=[tile(0), tile(1), halo(0), halo(1),
                  pl.BlockSpec((width, V7X_SUBLANES, ch), lambda bi, i: (0, 0, 0)), row, row, row],
        out_specs=pl.BlockSpec((None, ts, ch), lambda bi, i: (bi, i, 0)),
        out_shape=jax.ShapeDtypeStruct((b, s, ch), BF16),
        scratch_shapes=[pltpu.VMEM((ts + CONV_HALO, ch), F32), pltpu.VMEM(shifted, F32),
                        pltpu.VMEM((ts, ch), F32)],
        compiler_params=_cparams(("arbitrary", "arbitrary"), blk, scratch_bytes + 4 * _nbytes((ts, ch), F32)),
        name="conv_module",
    )(p3, p3, p3, p3, jnp.broadcast_to(conv_w[:, None, :], (width, V7X_SUBLANES, ch)),
      conv_b.reshape(1, ch), ln_g.reshape(1, ch), ln_b.reshape(1, ch))


def _bias_kernel(rel_ref, idx_ref, o_ref, *, n_buckets, nh, span, blk):
    g = pl.program_id(0)
    h = pl.program_id(1)
    idx = idx_ref[...]
    acc = jnp.zeros(idx.shape, F32)
    for bkt in range(n_buckets):
        acc = jnp.where(idx == bkt, rel_ref[bkt, g * nh + h], acc)
    qi = lax.broadcasted_iota(I32, idx.shape, 0) + blk
    kj = lax.broadcasted_iota(I32, idx.shape, 1)
    delta = qi - kj
    o_ref[...] = jnp.where((delta >= 0) & (delta <= span), acc, NEG_INF)


def _t5_bucket(dist, n_buckets):
    max_exact = n_buckets // 2
    n = jnp.maximum(dist, 1).astype(F32)
    large = max_exact + (jnp.log(n / max_exact) / math.log(MAX_DISTANCE / max_exact)
                         * (n_buckets - max_exact)).astype(I32)
    large = jnp.minimum(large, n_buckets - 1)
    return jnp.where(dist < max_exact, dist, large)


def _bias_tables(rel_bias, nh):
    n_buckets = rel_bias.shape[0]
    blk = ATTN_BLOCK
    span = ATTN_GROUPS[0][0] // ATTN_GROUPS[0][1]
    assert all(w // d == span for w, d in ATTN_GROUPS)
    delta = (jnp.arange(blk, dtype=I32)[:, None] + blk) - jnp.arange(2 * blk, dtype=I32)[None, :]
    idx = jnp.stack([_t5_bucket(jnp.maximum(delta, 0) * d, n_buckets) for _, d in ATTN_GROUPS])
    ng = len(ATTN_GROUPS)
    return pl.pallas_call(
        functools.partial(_bias_kernel, n_buckets=n_buckets, nh=nh, span=span, blk=blk),
        grid=(ng, nh),
        in_specs=[pl.BlockSpec(memory_space=pltpu.SMEM),
                  pl.BlockSpec((None, blk, 2 * blk), lambda g, h: (g, 0, 0))],
        out_specs=pl.BlockSpec((None, None, blk, 2 * blk), lambda g, h: (g, h, 0, 0)),
        out_shape=jax.ShapeDtypeStruct((ng, nh, blk, 2 * blk), F32),
        name="rel_bias_tables",
    )(rel_bias, idx)


ATTN_HEAD_PAIR = 2


def _attn_kernel(q_ref, k_ref, v_ref, qg_ref, kg_ref, bias_ref, seg_ref, o_ref, lse_ref, kprev, vprev,
                 qn_s, kn_s, s_s, p_s, m_s, *, nh, hd, blk):
    has_prev = pl.program_id(2) > 0
    scale = hd ** -0.5
    nt = (((1,), (1,)), ((), ()))
    pw = ATTN_HEAD_PAIR * hd
    seg = seg_ref[...]
    ones = jnp.ones((blk, hd), BF16)

    @pl.when(jnp.logical_not(has_prev))
    def _no_previous_block():
        kprev[...] = jnp.zeros(kprev.shape, kprev.dtype)
        vprev[...] = jnp.zeros(vprev.shape, vprev.dtype)

    def qk_norm(ref, c0, gain):
        x = ref[:, c0:c0 + pw].astype(F32)
        ss = jnp.dot((x * x).astype(BF16), seg, preferred_element_type=F32)
        return (x * lax.rsqrt(ss * (1.0 / hd) + 1e-6) * gain).astype(BF16)

    for c0 in range(0, nh * hd, pw):
        qn_s[:, c0:c0 + pw] = qk_norm(q_ref, c0, qg_ref[...])
        kn_s[:, c0:c0 + pw] = qk_norm(k_ref, c0, kg_ref[...])
    for h in range(nh):
        hs = slice(h * hd, (h + 1) * hd)
        bias = bias_ref[h]
        q = qn_s[:, hs]
        s_s[h, :, :blk] = lax.dot_general(q, kprev[:, hs], nt, preferred_element_type=F32) * scale \
            + jnp.where(has_prev, bias[:, :blk], NEG_INF)
        s_s[h, :, blk:] = lax.dot_general(q, kn_s[:, hs], nt, preferred_element_type=F32) * scale + bias[:, blk:]
    for h in range(nh):
        s = s_s[h]
        m = jnp.max(s, axis=-1, keepdims=True)
        m_s[h] = jnp.broadcast_to(m, (blk, hd))
        p_s[h] = jnp.exp(s - m).astype(BF16)
    assert hd == V7X_LANES and nh <= V7X_LANES
    head_lane = lax.broadcasted_iota(I32, (blk, hd), 1)
    lse = jnp.zeros((blk, hd), F32)
    for h in range(nh):
        hs = slice(h * hd, (h + 1) * hd)
        res = jnp.dot(p_s[h, :, :blk], jnp.concatenate([vprev[:, hs], ones], axis=1),
                      preferred_element_type=F32) \
            + jnp.dot(p_s[h, :, blk:], jnp.concatenate([v_ref[:, hs], ones], axis=1),
                      preferred_element_type=F32)
        l = res[:, hd:]
        o_ref[:, hs] = (res[:, :hd] / l).astype(o_ref.dtype)
        lse = jnp.where(head_lane == h, m_s[h] + jnp.log(l), lse)
    lse_ref[...] = lse
    kprev[...] = kn_s[...]
    vprev[...] = v_ref[...]


def _attn_group(pa, g, dil, qb, kb, vb, nh, hd, q_gain, k_gain, bias_g):
    b, s, _ = pa.shape
    blk = ATTN_BLOCK
    seq = s // dil
    assert s % dil == 0 and seq % blk == 0
    nb = seq // blk
    aw = nh * hd
    assert nh % ATTN_HEAD_PAIR == 0
    pw = ATTN_HEAD_PAIR * hd
    own = lambda c: pl.BlockSpec((None, blk, aw), lambda bi, r, n, c=c: (bi, r * nb + n, c))
    gain = pl.BlockSpec((1, pw), lambda bi, r, n: (0, 0))
    out_spec = pl.BlockSpec((None, blk, aw), lambda bi, r, n: (bi, r * nb + n, 0))
    lane_head = np.arange(pw) // hd
    seg = jnp.asarray((lane_head[:, None] == lane_head[None, :]).astype(np.float32), BF16)
    tile_gain = lambda gn: jnp.tile(gn, ATTN_HEAD_PAIR).reshape(1, pw)
    blk_bytes = 3 * _nbytes((blk, aw), BF16) + _nbytes((nh, blk, 2 * blk), F32) \
        + _nbytes((blk, aw), BF16) + _nbytes((blk, aw), F32) + _nbytes((pw, pw), BF16)
    o, lse = pl.pallas_call(
        functools.partial(_attn_kernel, nh=nh, hd=hd, blk=blk),
        grid=(b, dil, nb),
        in_specs=[own(qb), own(kb), own(vb), gain, gain,
                  pl.BlockSpec((nh, blk, 2 * blk), lambda bi, r, n: (0, 0, 0)),
                  pl.BlockSpec((pw, pw), lambda bi, r, n: (0, 0))],
        out_specs=[out_spec, pl.BlockSpec((None, blk, V7X_LANES), lambda bi, r, n: (bi, r * nb + n, 0))],
        out_shape=[jax.ShapeDtypeStruct((b, s, aw), BF16), jax.ShapeDtypeStruct((b, s, V7X_LANES), F32)],
        scratch_shapes=[pltpu.VMEM((blk, aw), BF16), pltpu.VMEM((blk, aw), BF16),
                        pltpu.VMEM((blk, aw), BF16), pltpu.VMEM((blk, aw), BF16),
                        pltpu.VMEM((nh, blk, 2 * blk), F32), pltpu.VMEM((nh, blk, 2 * blk), BF16),
                        pltpu.VMEM((nh, blk, hd), F32)],
        compiler_params=_cparams(("arbitrary", "arbitrary", "arbitrary"), blk_bytes,
                                 4 * _nbytes((blk, aw), BF16) + 2 * _nbytes((nh, blk, 2 * blk), F32)),
        name=f"dilated_attn_g{g}",
    )(pa, pa, pa, tile_gain(q_gain), tile_gain(k_gain), bias_g, seg)
    return o.reshape(b, dil, seq, aw), lse.reshape(b, dil, seq, V7X_LANES)


def _group_merge_kernel(*refs, dils, ts):
    ng = len(dils)
    o_refs, l_refs, expand_ref, out_ref = refs[:ng], refs[ng:2 * ng], refs[2 * ng], refs[2 * ng + 1]
    scratch = list(refs[2 * ng + 2:])

    def token_order(ref, dil):
        if dil == 1:
            return ref[0].astype(F32)
        buf = scratch.pop(0)
        n_chunks, _, lanes = buf.shape
        rows = ts // dil
        for r in range(dil):
            for c in range(n_chunks):
                buf[c, pl.ds(r, rows, stride=dil), :] = ref[r, :, c * lanes:(c + 1) * lanes].astype(F32)
        return jnp.concatenate([buf[c] for c in range(n_chunks)], axis=1)

    def spread(w):
        hi = w.astype(BF16)
        lo = (w - hi.astype(F32)).astype(BF16)
        return jnp.dot(hi, expand_ref[...], preferred_element_type=F32) \
            + jnp.dot(lo, expand_ref[...], preferred_element_type=F32)

    ls = [token_order(r, dil) for r, dil in zip(l_refs, dils)]
    m = functools.reduce(jnp.maximum, ls)
    es = [jnp.exp(l - m) for l in ls]
    den = functools.reduce(lambda a, c: a + c, es)
    num = None
    for e, r, dil in zip(es, o_refs, dils):
        term = spread(e / den) * token_order(r, dil)
        num = term if num is None else num + term
    out_ref[...] = num.astype(out_ref.dtype)


def _group_merge(outs, lses, dils, hd):
    b, _, _, aw = outs[0].shape
    s = outs[0].shape[1] * outs[0].shape[2]
    nh = aw // hd
    ts = _fit(s, 256, 16 * max(dils))
    spec = lambda dil, w: pl.BlockSpec((None, dil, ts // dil, w), lambda i, j: (i, 0, j, 0))
    n_perm = sum(1 for dil in dils if dil > 1)
    lane = np.arange(V7X_LANES)[:, None]
    expand = jnp.asarray((lane == (np.arange(aw)[None, :] // hd)).astype(np.float32), BF16)
    blk = len(dils) * (_nbytes((ts, aw), BF16) + _nbytes((ts, V7X_LANES), F32)) + _nbytes((ts, aw), BF16) \
        + _nbytes((V7X_LANES, aw), BF16)
    assert nh <= V7X_LANES
    return pl.pallas_call(
        functools.partial(_group_merge_kernel, dils=tuple(dils), ts=ts),
        grid=(b, s // ts),
        in_specs=[spec(dil, aw) for dil in dils] + [spec(dil, V7X_LANES) for dil in dils]
        + [pl.BlockSpec((V7X_LANES, aw), lambda i, j: (0, 0))],
        out_specs=pl.BlockSpec((None, ts, aw), lambda i, j: (i, j, 0)),
        out_shape=jax.ShapeDtypeStruct((b, s, aw), BF16),
        scratch_shapes=[pltpu.VMEM((1, ts, V7X_LANES), F32)] * n_perm
        + [pltpu.VMEM((aw // V7X_LANES, ts, V7X_LANES), F32)] * n_perm,
        compiler_params=_cparams(("arbitrary", "arbitrary"), blk, (n_perm + 4) * _nbytes((ts, aw), F32)),
        name="attn_group_merge")(*outs, *lses, expand)


def _row_copy(src_hbm, row, dst, slot, sem):
    return pltpu.make_async_copy(src_hbm.at[pl.ds(row, 1), :], dst.at[pl.ds(slot, 1), :], sem)


ROW_DMA_UNROLL = 8
DMA_PRIORITIES = 2


def _gather_kernel(src_ref, h_hbm, o_ref, buf, sem, *, rows):
    t = pl.program_id(0)
    slot = t % 2

    def fetch(step, dst_slot):
        def start(i, carry):
            for p in range(DMA_PRIORITIES):
                row = i * DMA_PRIORITIES + p
                _row_copy(h_hbm, src_ref[step * rows + row], buf.at[dst_slot], row,
                          sem.at[dst_slot]).start(priority=p)
            return carry
        lax.fori_loop(0, rows // DMA_PRIORITIES, start, 0, unroll=ROW_DMA_UNROLL // DMA_PRIORITIES)

    @pl.when(t == 0)
    def _first():
        fetch(0, 0)

    @pl.when(t + 1 < pl.num_programs(0))
    def _next():
        fetch(t + 1, 1 - slot)

    def wait(i, carry):
        _row_copy(h_hbm, 0, buf.at[slot], i, sem.at[slot]).wait()
        return carry

    lax.fori_loop(0, rows, wait, 0, unroll=ROW_DMA_UNROLL)
    o_ref[...] = buf[slot].astype(o_ref.dtype)


def _moe_gather(h2, src, rows):
    r_total = src.shape[0]
    d = h2.shape[1]
    return pl.pallas_call(
        functools.partial(_gather_kernel, rows=rows),
        grid_spec=pltpu.PrefetchScalarGridSpec(
            num_scalar_prefetch=1, grid=(r_total // rows,),
            in_specs=[pl.BlockSpec(memory_space=pl.ANY)],
            out_specs=pl.BlockSpec((rows, d), lambda t, src: (t, 0)),
            scratch_shapes=[pltpu.VMEM((2, rows, d), F32), pltpu.SemaphoreType.DMA((2,))]),
        out_shape=jax.ShapeDtypeStruct((r_total, d), BF16),
        compiler_params=_cparams(("arbitrary",), _nbytes((rows, d), BF16), 3 * _nbytes((rows, d), F32)),
        name="moe_gather",
    )(src, h2)


def _combine_kernel(pos_ref, y_hbm, x_ref, tw_ref, gt_ref, o_ref, buf, sem, *, rows):
    t = pl.program_id(0)
    slot = t % 2

    def fetch(step, dst_slot):
        def start(i, carry):
            for k in range(TOP_K):
                _row_copy(y_hbm, pos_ref[(step * rows + i) * TOP_K + k], buf.at[dst_slot, k], i,
                          sem.at[dst_slot, k]).start(priority=k % DMA_PRIORITIES)
            return carry
        lax.fori_loop(0, rows, start, 0, unroll=ROW_DMA_UNROLL // TOP_K)

    @pl.when(t == 0)
    def _first():
        fetch(0, 0)

    @pl.when(t + 1 < pl.num_programs(0))
    def _next():
        fetch(t + 1, 1 - slot)

    def wait(i, carry):
        for k in range(TOP_K):
            _row_copy(y_hbm, 0, buf.at[slot, k], i, sem.at[slot, k]).wait()
        return carry

    lax.fori_loop(0, rows, wait, 0, unroll=ROW_DMA_UNROLL // TOP_K)
    tw = tw_ref[...]
    y = tw[:, 0:1] * buf[slot, 0]
    for k in range(1, TOP_K):
        y = y + tw[:, k:k + 1] * buf[slot, k]
    o_ref[...] = x_ref[...] + gt_ref[...] * y


def _moe_combine(y_sorted, pos, x2, topw2, gate, rows_per_batch):
    m, d = x2.shape
    rows = _fit(rows_per_batch, 256, 8)
    blk = 2 * _nbytes((rows, d), F32) + _nbytes((rows, V7X_LANES), F32) + _nbytes((1, d), F32)
    return pl.pallas_call(
        functools.partial(_combine_kernel, rows=rows),
        grid_spec=pltpu.PrefetchScalarGridSpec(
            num_scalar_prefetch=1, grid=(m // rows,),
            in_specs=[pl.BlockSpec(memory_space=pl.ANY),
                      pl.BlockSpec((rows, d), lambda t, pos: (t, 0)),
                      pl.BlockSpec((rows, V7X_LANES), lambda t, pos: (t, 0)),
                      pl.BlockSpec((None, 1, d), lambda t, pos: ((t * rows) // rows_per_batch, 0, 0))],
            out_specs=pl.BlockSpec((rows, d), lambda t, pos: (t, 0)),
            scratch_shapes=[pltpu.VMEM((2, TOP_K, rows, d), F32), pltpu.SemaphoreType.DMA((2, TOP_K))]),
        out_shape=jax.ShapeDtypeStruct((m, d), F32),
        compiler_params=_cparams(("arbitrary",), blk, (2 * TOP_K + 2) * _nbytes((rows, d), F32)),
        name="moe_combine",
    )(pos, y_sorted, x2, topw2, gate)


def _moe_schedule(route_i, counts, n_exp, bm, expert_base):
    m = route_i.shape[0]
    n_assign = m * TOP_K
    n_tiles = n_assign // bm + n_exp
    e = route_i[:, :TOP_K].reshape(n_assign)
    rank = route_i[:, TOP_K:2 * TOP_K].reshape(n_assign)
    onehot = (e[:, None] == jnp.arange(n_exp, dtype=I32)[None, :]).astype(I32)
    tiles_per = (counts + bm - 1) // bm
    tile_end = jnp.cumsum(tiles_per)
    row_start = (tile_end - tiles_per) * bm
    pos = jnp.sum(onehot * row_start[None, :], axis=1) + rank
    used = tile_end[-1]
    t = jnp.arange(n_tiles, dtype=I32)
    tr = jnp.minimum(t, used - 1)
    te = jnp.minimum(jnp.sum((tr[:, None] >= tile_end[None, :]).astype(I32), axis=1), n_exp - 1)
    tv = (t < used).astype(I32)
    prev_e = jnp.concatenate([jnp.full((1,), -1, I32), te[:-1]])
    tf = tv * ((te != prev_e).astype(I32))
    later_start = (t[None, :] > t[:, None]) & (tf[None, :] == 1)
    tn = jnp.min(jnp.where(later_start, t[None, :], n_tiles), axis=1)
    tn = jnp.where(tn == n_tiles, -1, tn)
    src = (jnp.arange(n_tiles * bm, dtype=I32) % m).at[pos].set(jnp.arange(n_assign, dtype=I32) // TOP_K)
    sched = (te.astype(I32) + expert_base, tf.astype(I32), tv, tr.astype(I32), tn.astype(I32))
    return sched, pos.astype(I32), src


def kernel(x, c, w_ada, b_ada, g_mix, g_ffn, w_in, conv_w, conv_b, conv_ln_g, conv_ln_b, w_conv_out,
           q_gain, k_gain, rel_bias, w_attn_out, w_out, ffn_w1, ffn_w3, ffn_w2, moe_router, moe_w1,
           moe_w3, moe_w2):
    b, s, d = x.shape
    m = b * s
    depth = w_ada.shape[0]
    in_w = w_in.shape[2]
    ch = conv_w.shape[2]
    hd = q_gain.shape[1]
    att_out_w = w_attn_out.shape[1]
    nh = att_out_w // hd
    ng = len(ATTN_GROUPS)
    att_w = rel_bias.shape[1] * hd
    q_off = 2 * ch
    gate_off = q_off + 3 * att_w
    assert in_w == gate_off + 2 * d and rel_bias.shape[1] == ng * nh

    bm = _fit(s, 1024)
    stack = lambda w: w.reshape((w.shape[0] * w.shape[1],) + w.shape[2:])
    bn = _fit(d, 512)
    aw = nh * hd
    dils = [dil for _, dil in ATTN_GROUPS]
    assert dils[0] == 1 and q_off % bn == 0 and aw % bn == 0 and q_off % aw == 0

    cb = lambda start, width, n=bn: list(range(start // n, (start + width) // n))
    qkv_cols = lambda g, n=bn: sum([cb(q_off + i * att_w + g * aw, aw, n) for i in range(3)], [])
    bn_in = _fit(math.gcd(q_off, aw, d), 1024)
    nat_cols = cb(0, q_off, bn_in) + qkv_cols(0, bn_in) + cb(gate_off, 2 * d, bn_in)
    nat_gate = q_off + 3 * aw
    bmh, bnw = _fit(s, 512), _fit(math.gcd(d, nat_gate), 1024)

    mod = _adaln(c, w_ada, b_ada)
    bias = _bias_tables(rel_bias, nh)

    for l in range(depth):
        sh1, sc1, gt1, sh2, sc2, gt2 = [mod[l, :, i * d:(i + 1) * d] for i in range(6)]
        gt1 = gt1.reshape(b, 1, d)
        gt2 = gt2.reshape(b, 1, d)

        h = _rmsmod(x, g_mix[l], sh1, sc1, BF16).reshape(m, d)
        sched = _dense_sched(m // bm, l)
        schedh = _dense_sched(m // bmh, l)
        p = _ws_matmul("in_proj", _plain_compute, [(h, d, 0)], [(w_in, d, 0)], [],
                       m, len(nat_cols) * bn_in, BF16, bm, bn_in, sched, w_cols=nat_cols)
        p3 = p.reshape(b, s, len(nat_cols) * bn_in)
        cu = _conv_branch(p3, conv_w[l], conv_b[l], conv_ln_g[l], conv_ln_b[l]).reshape(m, ch)
        outs, lses = [], []
        for g, dil in enumerate(dils):
            if dil == 1:
                pa, qb = p3, q_off // aw
            else:
                pa = _ws_matmul(f"in_proj_g{g}", _plain_compute, [(h, d, 0)], [(w_in, d, 0)], [],
                                m, 3 * aw, BF16, bmh, bn_in, schedh, w_cols=qkv_cols(g, bn_in), dil=dil,
                                rows_per_batch=s).reshape(b, s, 3 * aw)
                qb = 0
            o_g, lse_g = _attn_group(pa, g, dil, qb, qb + 1, qb + 2, nh, hd, q_gain[l], k_gain[l], bias[g])
            outs.append(o_g)
            lses.append(lse_g)
        o = _group_merge(outs, lses, dils, hd).reshape(m, att_out_w)
        merged = _ws_matmul(
            "branch_merge", _merge_compute, [(cu, ch, 0), (o, att_out_w, 0)],
            [(w_conv_out, ch, 0), (w_attn_out, att_out_w, 0)],
            [(p, _tile_extra(bmh, bnw, nat_gate // bnw)), (p, _tile_extra(bmh, bnw, (nat_gate + d) // bnw))],
            m, d, BF16, bmh, bnw, schedh)
        x2 = _ws_matmul(
            "out_proj", _residual_compute, [(merged, d, 0)], [(w_out, d, 0)],
            [(x.reshape(m, d), _tile_extra(bmh, bnw, 0)), (gt1, _batch_row_extra(bmh, bnw, s))],
            m, d, F32, bmh, bnw, schedh)
        x = x2.reshape(b, s, d)

        j = l // 2
        if l % 2 == 0:
            h = _rmsmod(x, g_ffn[l], sh2, sc2, BF16).reshape(m, d)
            d_ff = ffn_w1.shape[2]
            hid = _ws_matmul("ffn_up", _swiglu_compute, [(h, d, 0)],
                             [(ffn_w1, d, 0), (ffn_w3, d, 0)], [],
                             m, d_ff, BF16, bm, _fit(d_ff, 256), _dense_sched(m // bm, j))
            ksplit = 2 if d_ff % (2 * V7X_LANES) == 0 else 1
            kc = d_ff // ksplit
            bm2 = _fit(s, 1024)
            sched2 = _dense_sched(m // bm2, j)
            for kb in range(ksplit):
                x2 = _ws_matmul(
                    "ffn_down", _residual_compute, [(hid, kc, kb)], [(ffn_w2, kc, kb)],
                    [(x2, _tile_extra(bm2, bn, 0)), (gt2, _batch_row_extra(bm2, bn, s))],
                    m, d, F32, bm2, bn, sched2)
            x = x2.reshape(b, s, d)
        else:
            n_exp = moe_router.shape[2]
            d_fe = moe_w1.shape[3]
            hf, topw, topi, cnt = _rmsmod(x, g_ffn[l], sh2, sc2, F32, w_router=moe_router[j])
            bme = _fit(m, 512)
            msched, pos, src = _moe_schedule(topi.reshape(m, V7X_LANES), cnt[0, :n_exp].astype(I32), n_exp, bme,
                                             j * n_exp)
            n_rows = src.shape[0]
            xs = _moe_gather(hf.reshape(m, d), src, _fit(bme, 256, 16))
            hid = _ws_matmul("moe_up", _swiglu_compute, [(xs, d, 0)],
                             [(stack(moe_w1), d, 0), (stack(moe_w3), d, 0)], [],
                             n_rows, d_fe, BF16, bme, _fit(d_fe, 512), msched)
            ys = _ws_matmul("moe_down", _plain_compute, [(hid, d_fe, 0)], [(stack(moe_w2), d_fe, 0)], [],
                            n_rows, d, F32, bme, _fit(d, 1024), msched)
            x = _moe_combine(ys, pos, x2, topw.reshape(m, V7X_LANES), gt2, s).reshape(b, s, d)
    return x
```
